```python
import math
import jax, jax.numpy as jnp
from jax import lax
import numpy as np

D_MODEL = 1024
BATCH = 16
SEQ = 4096
DEPTH = 2

W_LRU = D_MODEL // 2
LRU_BLOCKS = 8
LRU_C = 8.0
CONV_K = 4

SSM_HEAD_DIM = 64
SSM_D_INNER = D_MODEL
SSM_HEADS = SSM_D_INNER // SSM_HEAD_DIM
SSM_GROUPS = 2
SSM_STATE = 128
SSM_CHUNK = 128
SSM_CONV_CH = SSM_D_INNER + 2 * SSM_GROUPS * SSM_STATE

ATTN_HEAD_DIM = 64
ATTN_Q_HEADS = (D_MODEL // 2) // ATTN_HEAD_DIM
ATTN_KV_HEADS = 2
WINDOW = 128
ATTN_BLOCK = WINDOW

MIX_WIDTH = W_LRU + SSM_D_INNER + ATTN_Q_HEADS * ATTN_HEAD_DIM

_SEG = (W_LRU, W_LRU,
        SSM_D_INNER, SSM_CONV_CH, SSM_HEADS,
        ATTN_Q_HEADS * ATTN_HEAD_DIM,
        ATTN_KV_HEADS * ATTN_HEAD_DIM,
        ATTN_KV_HEADS * ATTN_HEAD_DIM)
IN_COLS = sum(_SEG)
SPLIT_POINTS = tuple(int(v) for v in np.cumsum(_SEG)[:-1])

PEER_HEADS = 8
PEER_NKEYS = 128
PEER_EXPERTS = PEER_NKEYS * PEER_NKEYS
PEER_DKEY = 128
PEER_TOPK = 16
PEER_CHUNK = 128

DN_ALPHA = (2 * DEPTH) ** 0.25
DN_BETA = (8 * DEPTH) ** -0.25
LN_EPS = 1e-5

kernel_name = "hymba_style_rglru_ssd_swa_peer_deepnorm"


def layer_norm(x, g, b):
    xf = x.astype(jnp.float32)
    mu = jnp.mean(xf, axis=-1, keepdims=True)
    var = jnp.mean(jnp.square(xf - mu), axis=-1, keepdims=True)
    return ((xf - mu) * lax.rsqrt(var + LN_EPS) * g + b).astype(x.dtype)


def causal_dwconv(x, w, b):
    k = w.shape[0]
    s = x.shape[1]
    xp = jnp.pad(x, ((0, 0), (k - 1, 0), (0, 0)))
    out = b + w[0] * xp[:, 0:s]
    for i in range(1, k):
        out = out + w[i] * xp[:, i:i + s]
    return out


def rg_lru(x, w_a, b_a, w_x, b_x, lam):
    bsz, s, w = x.shape
    xb = x.reshape(bsz, s, LRU_BLOCKS, w // LRU_BLOCKS)
    r = jax.nn.sigmoid((jnp.einsum('bsnc,ncd->bsnd', xb, w_a).reshape(bsz, s, w) + b_a).astype(jnp.float32))
    i = jax.nn.sigmoid((jnp.einsum('bsnc,ncd->bsnd', xb, w_x).reshape(bsz, s, w) + b_x).astype(jnp.float32))
    log_a = -LRU_C * r * jax.nn.softplus(-lam.astype(jnp.float32))
    a = jnp.exp(log_a)
    u = jnp.sqrt(-jnp.expm1(2.0 * log_a)) * (i * x.astype(jnp.float32))

    def combine(left, right):
        a_l, h_l = left
        a_r, h_r = right
        return a_l * a_r, a_r * h_l + h_r

    _, h = lax.associative_scan(combine, (a, u), axis=1)
    return h.astype(x.dtype)


def ssd_scan(xh, dt, a, bmat, cmat):
    f32 = jnp.float32
    bsz, s, h, p = xh.shape
    g, n = bmat.shape[2], bmat.shape[3]
    hg = h // g
    nc, l = s // SSM_CHUNK, SSM_CHUNK
    x = (xh.astype(f32) * dt[..., None]).reshape(bsz, nc, l, g, hg, p)
    da = (dt * a).reshape(bsz, nc, l, g, hg).transpose(0, 3, 4, 1, 2)
    bc = bmat.astype(f32).reshape(bsz, nc, l, g, n)
    cc = cmat.astype(f32).reshape(bsz, nc, l, g, n)
    a_cs = jnp.cumsum(da, axis=-1)
    causal = jnp.tril(jnp.ones((l, l), dtype=bool))
    seg = a_cs[..., :, None] - a_cs[..., None, :]
    decay = jnp.exp(jnp.where(causal, seg, -jnp.inf))
    cb = jnp.einsum('bclgn,bcsgn->bgcls', cc, bc)
    y_diag = jnp.einsum('bghcls,bcsghp->bclghp', cb[:, :, None] * decay, x)
    decay_states = jnp.exp(a_cs[..., -1:] - a_cs).transpose(0, 3, 4, 1, 2)
    states = jnp.einsum('bclgn,bclghp->bcghpn', bc, x * decay_states[..., None])
    chunk_decay = jnp.exp(a_cs[..., -1])

    def step(carry, inp):
        st, dec = inp
        return carry * dec[..., None, None] + st, carry

    init = jnp.zeros((bsz, g, hg, p, n), f32)
    _, prev = lax.scan(step, init, (jnp.moveaxis(states, 1, 0), jnp.moveaxis(chunk_decay, 3, 0)))
    prev = jnp.moveaxis(prev, 0, 1)
    out_decay = jnp.exp(a_cs).transpose(0, 3, 4, 1, 2)
    y_off = jnp.einsum('bclgn,bcghpn->bclghp', cc, prev) * out_decay[..., None]
    return (y_diag + y_off).reshape(bsz, s, h, p)


def mamba2_group(z, xbc, dt_raw, conv_w, conv_b, dt_bias, a_log, d_skip, norm_g):
    f32 = jnp.float32
    bsz, s, _ = z.shape
    xbc = jax.nn.silu(causal_dwconv(xbc, conv_w, conv_b))
    xs, bm, cm = jnp.split(xbc, [SSM_D_INNER, SSM_D_INNER + SSM_GROUPS * SSM_STATE], axis=-1)
    dt = jax.nn.softplus(dt_raw.astype(f32) + dt_bias.astype(f32))
    a = -jnp.exp(a_log.astype(f32))
    xh = xs.reshape(bsz, s, SSM_HEADS, SSM_HEAD_DIM)
    y = ssd_scan(xh, dt, a,
                 bm.reshape(bsz, s, SSM_GROUPS, SSM_STATE),
                 cm.reshape(bsz, s, SSM_GROUPS, SSM_STATE))
    y = y + d_skip.astype(f32)[:, None] * xh.astype(f32)
    y = y.reshape(bsz, s, SSM_D_INNER) * jax.nn.silu(z.astype(f32))
    yg = y.reshape(bsz, s, SSM_GROUPS, SSM_D_INNER // SSM_GROUPS)
    yg = yg * lax.rsqrt(jnp.mean(jnp.square(yg), axis=-1, keepdims=True) + LN_EPS)
    return (yg.reshape(bsz, s, SSM_D_INNER) * norm_g).astype(z.dtype)


def sliding_window_attention(q, k, v, sinks):
    f32 = jnp.float32
    bsz, s, _ = q.shape
    nb = s // ATTN_BLOCK
    rep = ATTN_Q_HEADS // ATTN_KV_HEADS
    qb = q.reshape(bsz, nb, ATTN_BLOCK, ATTN_KV_HEADS, rep, ATTN_HEAD_DIM)
    kb = k.reshape(bsz, nb, ATTN_BLOCK, ATTN_KV_HEADS, ATTN_HEAD_DIM)
    vb = v.reshape(bsz, nb, ATTN_BLOCK, ATTN_KV_HEADS, ATTN_HEAD_DIM)

    def with_prev(t):
        prev = jnp.pad(t, ((0, 0), (1, 0), (0, 0), (0, 0), (0, 0)))[:, :-1]
        return jnp.concatenate([prev, t], axis=2)

    kk, vv = with_prev(kb), with_prev(vb)
    logits = jnp.einsum('bnqhrd,bnkhd->bnhrqk', qb, kk).astype(f32) * (ATTN_HEAD_DIM ** -0.5)
    qi = jnp.arange(ATTN_BLOCK)[:, None]
    kj = jnp.arange(2 * ATTN_BLOCK)[None, :]
    rel = qi + ATTN_BLOCK - kj
    band = (rel >= 0) & (rel < WINDOW)
    blk = jnp.arange(nb)[:, None, None]
    valid = band[None] & ((blk > 0) | (kj[None] >= ATTN_BLOCK))
    logits = jnp.where(valid[None, :, None, None], logits, -jnp.inf)
    sink = sinks.astype(f32).reshape(ATTN_KV_HEADS, rep)[None, None, :, :, None, None]
    m = jnp.maximum(jnp.max(logits, axis=-1, keepdims=True), sink)
    p = jnp.exp(logits - m)
    probs = p / (jnp.sum(p, axis=-1, keepdims=True) + jnp.exp(sink - m))
    out = jnp.einsum('bnhrqk,bnkhd->bnqhrd', probs.astype(v.dtype), vv)
    return out.reshape(bsz, s, ATTN_Q_HEADS * ATTN_HEAD_DIM)


def peer_ffn(x, wq, keys, u, v):
    bsz, s, d = x.shape
    xt = x.reshape(-1, PEER_CHUNK, d)
    half = PEER_DKEY // 2

    def chunk(xc):
        c = xc.shape[0]
        q = (xc @ wq).reshape(c, PEER_HEADS, 2, half)
        sc = jnp.einsum('thid,ikd->thik', q, keys).astype(jnp.float32)
        top_s, top_i = lax.top_k(sc, PEER_TOPK)
        cand_s = top_s[:, :, 0, :, None] + top_s[:, :, 1, None, :]
        cand_i = top_i[:, :, 0, :, None] * PEER_NKEYS + top_i[:, :, 1, None, :]
        best_s, best_pos = lax.top_k(cand_s.reshape(c, PEER_HEADS, -1), PEER_TOPK)
        idx = jnp.take_along_axis(cand_i.reshape(c, PEER_HEADS, -1), best_pos, axis=-1)
        gate = jax.nn.softmax(best_s, axis=-1)
        ue = u[idx]
        ve = v[idx]
        act = jax.nn.gelu(jnp.einsum('td,thkd->thk', xc, ue).astype(jnp.float32), approximate=False)
        return jnp.einsum('thk,thkd->td', (gate * act).astype(xc.dtype), ve)

    return lax.map(chunk, xt).reshape(bsz, s, d)


def setup_inputs(seed: int = 0) -> dict:
    key = jax.random.key(seed)
    ks = jax.random.split(key, 32)
    f32 = jnp.float32

    def nrm(k, shape, scale):
        return jax.random.normal(k, shape, f32) * scale

    bw = W_LRU // LRU_BLOCKS
    a0 = jax.random.uniform(ks[10], (DEPTH, W_LRU), f32, 0.9, 0.999)
    a_base = a0 ** (1.0 / LRU_C)
    dt0 = jnp.exp(jax.random.uniform(ks[13], (DEPTH, SSM_HEADS), f32, math.log(1e-3), math.log(1e-1)))
    return {
        "x": nrm(ks[0], (BATCH, SEQ, D_MODEL), 1.0),
        "emb_ln_g": 1.0 + nrm(ks[1], (D_MODEL,), 0.02),
        "emb_ln_b": nrm(ks[2], (D_MODEL,), 0.02),
        "w_in": nrm(ks[3], (DEPTH, D_MODEL, IN_COLS), D_MODEL ** -0.5),
        "rg_conv_w": nrm(ks[4], (DEPTH, CONV_K, W_LRU), CONV_K ** -0.5),
        "rg_conv_b": nrm(ks[5], (DEPTH, W_LRU), 0.02),
        "rg_wa": nrm(ks[6], (DEPTH, LRU_BLOCKS, bw, bw), bw ** -0.5),
        "rg_ba": nrm(ks[7], (DEPTH, W_LRU), 0.02),
        "rg_wx": nrm(ks[8], (DEPTH, LRU_BLOCKS, bw, bw), bw ** -0.5),
        "rg_bx": nrm(ks[9], (DEPTH, W_LRU), 0.02),
        "rg_lambda": jnp.log(a_base) - jnp.log1p(-a_base),
        "ssm_conv_w": nrm(ks[11], (DEPTH, CONV_K, SSM_CONV_CH), CONV_K ** -0.5),
        "ssm_conv_b": nrm(ks[12], (DEPTH, SSM_CONV_CH), 0.02),
        "ssm_dt_bias": dt0 + jnp.log(-jnp.expm1(-dt0)),
        "ssm_a_log": jnp.log(jax.random.uniform(ks[14], (DEPTH, SSM_HEADS), f32, 1.0, 16.0)),
        "ssm_d": 1.0 + nrm(ks[15], (DEPTH, SSM_HEADS), 0.02),
        "ssm_norm_g": 1.0 + nrm(ks[16], (DEPTH, SSM_D_INNER), 0.02),
        "attn_sinks": nrm(ks[17], (DEPTH, ATTN_Q_HEADS), 1.0),
        "w_out": nrm(ks[18], (DEPTH, MIX_WIDTH, D_MODEL), DN_BETA * MIX_WIDTH ** -0.5),
        "ln1_g": 1.0 + nrm(ks[19], (DEPTH, D_MODEL), 0.02),
        "ln1_b": nrm(ks[20], (DEPTH, D_MODEL), 0.02),
        "peer_wq": nrm(ks[21], (DEPTH, D_MODEL, PEER_HEADS * PEER_DKEY), D_MODEL ** -0.5),
        "peer_keys": nrm(ks[22], (DEPTH, 2, PEER_NKEYS, PEER_DKEY // 2), (PEER_DKEY // 2) ** -0.5),
        "peer_u": nrm(ks[23], (DEPTH, PEER_EXPERTS, D_MODEL), D_MODEL ** -0.5),
        "peer_v": nrm(ks[24], (DEPTH, PEER_EXPERTS, D_MODEL), DN_BETA * PEER_HEADS ** -0.5),
        "ln2_g": 1.0 + nrm(ks[25], (DEPTH, D_MODEL), 0.02),
        "ln2_b": nrm(ks[26], (DEPTH, D_MODEL), 0.02),
    }


def reference(x, emb_ln_g, emb_ln_b, w_in, rg_conv_w, rg_conv_b, rg_wa, rg_ba, rg_wx, rg_bx,
              rg_lambda, ssm_conv_w, ssm_conv_b, ssm_dt_bias, ssm_a_log, ssm_d, ssm_norm_g,
              attn_sinks, w_out, ln1_g, ln1_b, peer_wq, peer_keys, peer_u, peer_v, ln2_g, ln2_b):
    h = layer_norm(x, emb_ln_g, emb_ln_b)
    for l in range(DEPTH):
        proj = h @ w_in[l]
        rg_x, rg_gate, z, xbc, dt_raw, q, k, v = jnp.split(proj, SPLIT_POINTS, axis=-1)
        y_a = jax.nn.gelu(rg_gate) * rg_lru(causal_dwconv(rg_x, rg_conv_w[l], rg_conv_b[l]),
                                            rg_wa[l], rg_ba[l], rg_wx[l], rg_bx[l], rg_lambda[l])
        y_b = mamba2_group(z, xbc, dt_raw, ssm_conv_w[l], ssm_conv_b[l], ssm_dt_bias[l],
                           ssm_a_log[l], ssm_d[l], ssm_norm_g[l])
        y_c = sliding_window_attention(q, k, v, attn_sinks[l])
        mix = jnp.concatenate([y_a.astype(h.dtype), y_b, y_c.astype(h.dtype)], axis=-1) @ w_out[l]
        h = layer_norm(DN_ALPHA * h + mix, ln1_g[l], ln1_b[l])
        ffn = peer_ffn(h, peer_wq[l], peer_keys[l], peer_u[l], peer_v[l])
        h = layer_norm(DN_ALPHA * h + ffn, ln2_g[l], ln2_b[l])
    return h
```

```python
import math

import jax
import jax.numpy as jnp
import numpy as np
from jax import lax
from jax.experimental import pallas as pl
from jax.experimental.pallas import tpu as pltpu

D_MODEL = 1024
DEPTH = 2

W_LRU = D_MODEL // 2
LRU_BLOCKS = 8
LRU_C = 8.0
CONV_K = 4

SSM_HEAD_DIM = 64
SSM_D_INNER = D_MODEL
SSM_HEADS = SSM_D_INNER // SSM_HEAD_DIM
SSM_GROUPS = 2
SSM_STATE = 128
SSM_CHUNK = 128
SSM_CONV_CH = SSM_D_INNER + 2 * SSM_GROUPS * SSM_STATE

ATTN_HEAD_DIM = 64
ATTN_Q_HEADS = (D_MODEL // 2) // ATTN_HEAD_DIM
ATTN_KV_HEADS = 2
WINDOW = 128
ATTN_BLOCK = WINDOW

MIX_WIDTH = W_LRU + SSM_D_INNER + ATTN_Q_HEADS * ATTN_HEAD_DIM

_SEG = (W_LRU, W_LRU, SSM_D_INNER, SSM_CONV_CH, SSM_HEADS,
        ATTN_Q_HEADS * ATTN_HEAD_DIM, ATTN_KV_HEADS * ATTN_HEAD_DIM, ATTN_KV_HEADS * ATTN_HEAD_DIM)
IN_COLS = sum(_SEG)
SPLIT_POINTS = tuple(int(v) for v in np.cumsum(_SEG)[:-1])

PEER_HEADS = 8
PEER_NKEYS = 128
PEER_EXPERTS = PEER_NKEYS * PEER_NKEYS
PEER_DKEY = 128
PEER_TOPK = 16
PEER_PAIRS = PEER_HEADS * PEER_TOPK

DN_ALPHA = (2 * DEPTH) ** 0.25
LN_EPS = 1e-5

SUBLANES = 8
LANES = 128
ROW_WORDS_SUBLANES = D_MODEL // 2 // LANES
TOKEN_SUBLANES = D_MODEL // LANES

GATHER_TOKENS = 64
ROUTE_TOKENS = 256
PEER_VMEM_LIMIT = 56 * 1024 * 1024


def layer_norm(x, g, b):
    xf = x.astype(jnp.float32)
    mu = jnp.mean(xf, axis=-1, keepdims=True)
    var = jnp.mean(jnp.square(xf - mu), axis=-1, keepdims=True)
    return ((xf - mu) * lax.rsqrt(var + LN_EPS) * g + b).astype(x.dtype)


def causal_dwconv(x, w, b):
    k = w.shape[0]
    s = x.shape[1]
    xp = jnp.pad(x, ((0, 0), (k - 1, 0), (0, 0)))
    out = b + w[0] * xp[:, 0:s]
    for i in range(1, k):
        out = out + w[i] * xp[:, i:i + s]
    return out


def rg_lru(x, w_a, b_a, w_x, b_x, lam):
    bsz, s, w = x.shape
    xb = x.reshape(bsz, s, LRU_BLOCKS, w // LRU_BLOCKS)
    r = jax.nn.sigmoid((jnp.einsum('bsnc,ncd->bsnd', xb, w_a).reshape(bsz, s, w) + b_a).astype(jnp.float32))
    i = jax.nn.sigmoid((jnp.einsum('bsnc,ncd->bsnd', xb, w_x).reshape(bsz, s, w) + b_x).astype(jnp.float32))
    log_a = -LRU_C * r * jax.nn.softplus(-lam.astype(jnp.float32))
    a = jnp.exp(log_a)
    u = jnp.sqrt(-jnp.expm1(2.0 * log_a)) * (i * x.astype(jnp.float32))

    def combine(left, right):
        a_l, h_l = left
        a_r, h_r = right
        return a_l * a_r, a_r * h_l + h_r

    _, h = lax.associative_scan(combine, (a, u), axis=1)
    return h.astype(x.dtype)


def ssd_scan(xh, dt, a, bmat, cmat):
    f32 = jnp.float32
    bsz, s, h, p = xh.shape
    g, n = bmat.shape[2], bmat.shape[3]
    hg = h // g
    nc, l = s // SSM_CHUNK, SSM_CHUNK
    x = (xh.astype(f32) * dt[..., None]).reshape(bsz, nc, l, g, hg, p)
    da = (dt * a).reshape(bsz, nc, l, g, hg).transpose(0, 3, 4, 1, 2)
    bc = bmat.astype(f32).reshape(bsz, nc, l, g, n)
    cc = cmat.astype(f32).reshape(bsz, nc, l, g, n)
    a_cs = jnp.cumsum(da, axis=-1)
    causal = jnp.tril(jnp.ones((l, l), dtype=bool))
    seg = a_cs[..., :, None] - a_cs[..., None, :]
    decay = jnp.exp(jnp.where(causal, seg, -jnp.inf))
    cb = jnp.einsum('bclgn,bcsgn->bgcls', cc, bc)
    y_diag = jnp.einsum('bghcls,bcsghp->bclghp', cb[:, :, None] * decay, x)
    decay_states = jnp.exp(a_cs[..., -1:] - a_cs).transpose(0, 3, 4, 1, 2)
    states = jnp.einsum('bclgn,bclghp->bcghpn', bc, x * decay_states[..., None])
    chunk_decay = jnp.exp(a_cs[..., -1])

    def step(carry, inp):
        st, dec = inp
        return carry * dec[..., None, None] + st, carry

    init = jnp.zeros((bsz, g, hg, p, n), f32)
    _, prev = lax.scan(step, init, (jnp.moveaxis(states, 1, 0), jnp.moveaxis(chunk_decay, 3, 0)))
    prev = jnp.moveaxis(prev, 0, 1)
    out_decay = jnp.exp(a_cs).transpose(0, 3, 4, 1, 2)
    y_off = jnp.einsum('bclgn,bcghpn->bclghp', cc, prev) * out_decay[..., None]
    return (y_diag + y_off).reshape(bsz, s, h, p)


def mamba2_group(z, xbc, dt_raw, conv_w, conv_b, dt_bias, a_log, d_skip, norm_g):
    f32 = jnp.float32
    bsz, s, _ = z.shape
    xbc = jax.nn.silu(causal_dwconv(xbc, conv_w, conv_b))
    xs, bm, cm = jnp.split(xbc, [SSM_D_INNER, SSM_D_INNER + SSM_GROUPS * SSM_STATE], axis=-1)
    dt = jax.nn.softplus(dt_raw.astype(f32) + dt_bias.astype(f32))
    a = -jnp.exp(a_log.astype(f32))
    xh = xs.reshape(bsz, s, SSM_HEADS, SSM_HEAD_DIM)
    y = ssd_scan(xh, dt, a,
                 bm.reshape(bsz, s, SSM_GROUPS, SSM_STATE),
                 cm.reshape(bsz, s, SSM_GROUPS, SSM_STATE))
    y = y + d_skip.astype(f32)[:, None] * xh.astype(f32)
    y = y.reshape(bsz, s, SSM_D_INNER) * jax.nn.silu(z.astype(f32))
    yg = y.reshape(bsz, s, SSM_GROUPS, SSM_D_INNER // SSM_GROUPS)
    yg = yg * lax.rsqrt(jnp.mean(jnp.square(yg), axis=-1, keepdims=True) + LN_EPS)
    return (yg.reshape(bsz, s, SSM_D_INNER) * norm_g).astype(z.dtype)


def sliding_window_attention(q, k, v, sinks):
    f32 = jnp.float32
    bsz, s, _ = q.shape
    nb = s // ATTN_BLOCK
    rep = ATTN_Q_HEADS // ATTN_KV_HEADS
    qb = q.reshape(bsz, nb, ATTN_BLOCK, ATTN_KV_HEADS, rep, ATTN_HEAD_DIM)
    kb = k.reshape(bsz, nb, ATTN_BLOCK, ATTN_KV_HEADS, ATTN_HEAD_DIM)
    vb = v.reshape(bsz, nb, ATTN_BLOCK, ATTN_KV_HEADS, ATTN_HEAD_DIM)

    def with_prev(t):
        prev = jnp.pad(t, ((0, 0), (1, 0), (0, 0), (0, 0), (0, 0)))[:, :-1]
        return jnp.concatenate([prev, t], axis=2)

    kk, vv = with_prev(kb), with_prev(vb)
    logits = jnp.einsum('bnqhrd,bnkhd->bnhrqk', qb, kk).astype(f32) * (ATTN_HEAD_DIM ** -0.5)
    qi = jnp.arange(ATTN_BLOCK)[:, None]
    kj = jnp.arange(2 * ATTN_BLOCK)[None, :]
    rel = qi + ATTN_BLOCK - kj
    band = (rel >= 0) & (rel < WINDOW)
    blk = jnp.arange(nb)[:, None, None]
    valid = band[None] & ((blk > 0) | (kj[None] >= ATTN_BLOCK))
    logits = jnp.where(valid[None, :, None, None], logits, -jnp.inf)
    sink = sinks.astype(f32).reshape(ATTN_KV_HEADS, rep)[None, None, :, :, None, None]
    m = jnp.maximum(jnp.max(logits, axis=-1, keepdims=True), sink)
    p = jnp.exp(logits - m)
    probs = p / (jnp.sum(p, axis=-1, keepdims=True) + jnp.exp(sink - m))
    out = jnp.einsum('bnhrqk,bnkhd->bnqhrd', probs.astype(v.dtype), vv)
    return out.reshape(bsz, s, ATTN_Q_HEADS * ATTN_HEAD_DIM)


def _unpack_bf16_pair(w):
    lo = lax.bitcast_convert_type(w << 16, jnp.float32)
    hi = lax.bitcast_convert_type(w & jnp.uint32(0xFFFF0000), jnp.float32)
    return lo, hi


def _gather_row(tbl_ref, idx):
    r = pl.multiple_of(idx * ROW_WORDS_SUBLANES, ROW_WORDS_SUBLANES)
    return tbl_ref[pl.ds(r, ROW_WORDS_SUBLANES), :]


def _peer_in_kernel(idx_ref, x_ref, gate_ref, tbl_ref, o_ref, prod_ref):
    ones = jnp.ones((LANES, LANES), jnp.bfloat16)
    eye = (lax.broadcasted_iota(jnp.int32, (PEER_PAIRS, LANES), 0)
           == lax.broadcasted_iota(jnp.int32, (PEER_PAIRS, LANES), 1))
    rs = ROW_WORDS_SUBLANES

    def token(t, carry):
        base = pl.multiple_of(t * TOKEN_SUBLANES, TOKEN_SUBLANES)
        x_lo = x_ref[pl.ds(base, rs), :]
        x_hi = x_ref[pl.ds(base + rs, rs), :]
        for p in range(PEER_PAIRS):
            lo, hi = _unpack_bf16_pair(_gather_row(tbl_ref, idx_ref[t, p]))
            prod_ref[pl.ds(rs * p, rs), :] = lo * x_lo + hi * x_hi
        q = prod_ref[pl.ds(0, PEER_PAIRS, stride=rs), :]
        for s in range(1, rs):
            q = q + prod_ref[pl.ds(s, PEER_PAIRS, stride=rs), :]
        q_hi = q.astype(jnp.bfloat16)
        q_lo = (q - q_hi.astype(jnp.float32)).astype(jnp.bfloat16)
        r = (jnp.dot(q_hi, ones, preferred_element_type=jnp.float32)
             + jnp.dot(q_lo, ones, preferred_element_type=jnp.float32))
        o_ref[pl.ds(t, 1), :] = jnp.sum(jnp.where(eye, r, 0.0), axis=0, keepdims=True)
        return carry

    lax.fori_loop(0, GATHER_TOKENS, token, 0)
    a = o_ref[...]
    o_ref[...] = gate_ref[...] * (0.5 * a * (1.0 + lax.erf(a * (2.0 ** -0.5))))


def _peer_in(idx, x8, gate, tbl):
    t = idx.shape[0]
    tt = GATHER_TOKENS
    return pl.pallas_call(
        _peer_in_kernel,
        out_shape=jax.ShapeDtypeStruct((t, PEER_PAIRS), jnp.float32),
        grid=(t // tt,),
        in_specs=[
            pl.BlockSpec((tt, PEER_PAIRS), lambda i: (i, 0), memory_space=pltpu.SMEM),
            pl.BlockSpec((tt * TOKEN_SUBLANES, LANES), lambda i: (i, 0)),
            pl.BlockSpec((tt, PEER_PAIRS), lambda i: (i, 0)),
            pl.BlockSpec(memory_space=pltpu.VMEM),
        ],
        out_specs=pl.BlockSpec((tt, PEER_PAIRS), lambda i: (i, 0)),
        scratch_shapes=[pltpu.VMEM((ROW_WORDS_SUBLANES * PEER_PAIRS, LANES), jnp.float32)],
        compiler_params=pltpu.CompilerParams(vmem_limit_bytes=PEER_VMEM_LIMIT),
        name="peer_in",
    )(idx, x8, gate, tbl)


def _peer_out_kernel(idx_ref, w_ref, tbl_ref, o_ref):
    rs = ROW_WORDS_SUBLANES
    n_acc = 4

    def token(t, carry):
        base = pl.multiple_of(t * TOKEN_SUBLANES, TOKEN_SUBLANES)
        acc_lo = [jnp.zeros((rs, LANES), jnp.float32) for _ in range(n_acc)]
        acc_hi = [jnp.zeros((rs, LANES), jnp.float32) for _ in range(n_acc)]
        for p in range(PEER_PAIRS):
            lo, hi = _unpack_bf16_pair(_gather_row(tbl_ref, idx_ref[t, p]))
            g = w_ref[t, p]
            acc_lo[p % n_acc] = acc_lo[p % n_acc] + g * lo
            acc_hi[p % n_acc] = acc_hi[p % n_acc] + g * hi
        o_ref[pl.ds(base, rs), :] = (acc_lo[0] + acc_lo[1]) + (acc_lo[2] + acc_lo[3])
        o_ref[pl.ds(base + rs, rs), :] = (acc_hi[0] + acc_hi[1]) + (acc_hi[2] + acc_hi[3])
        return carry

    lax.fori_loop(0, GATHER_TOKENS, token, 0)


def _peer_out(idx, w, tbl):
    t = idx.shape[0]
    tt = GATHER_TOKENS
    return pl.pallas_call(
        _peer_out_kernel,
        out_shape=jax.ShapeDtypeStruct((t * TOKEN_SUBLANES, LANES), jnp.float32),
        grid=(t // tt,),
        in_specs=[
            pl.BlockSpec((tt, PEER_PAIRS), lambda i: (i, 0), memory_space=pltpu.SMEM),
            pl.BlockSpec((tt, PEER_PAIRS), lambda i: (i, 0), memory_space=pltpu.SMEM),
            pl.BlockSpec(memory_space=pltpu.VMEM),
        ],
        out_specs=pl.BlockSpec((tt * TOKEN_SUBLANES, LANES), lambda i: (i, 0)),
        compiler_params=pltpu.CompilerParams(vmem_limit_bytes=PEER_VMEM_LIMIT),
        name="peer_out",
    )(idx, w, tbl)


def _extract_topk(s, pos, k, payload=None):
    big = jnp.int32(2 ** 30)
    vals, poss, pays = [], [], []
    for _ in range(k):
        m = jnp.max(s, axis=0, keepdims=True)
        j = jnp.min(jnp.where(s == m, pos, big), axis=0, keepdims=True)
        sel = pos == j
        vals.append(m)
        poss.append(j)
        if payload is not None:
            pays.append(jnp.sum(jnp.where(sel, payload, 0), axis=0, keepdims=True))
        s = jnp.where(sel, -jnp.inf, s)
    out = [jnp.concatenate(vals, axis=0), jnp.concatenate(poss, axis=0)]
    if payload is not None:
        out.append(jnp.concatenate(pays, axis=0))
    return out


def _peer_route_kernel(h_ref, wqt_ref, keys_ref, idx_ref, gate_ref):
    half_dim = PEER_DKEY // 2
    h = h_ref[...]
    n = h.shape[0]
    qt = lax.dot_general(wqt_ref[...], h, (((1,), (1,)), ((), ())), preferred_element_type=jnp.float32)
    key_iota = lax.broadcasted_iota(jnp.int32, (PEER_NKEYS, n), 0)
    iota16 = lax.broadcasted_iota(jnp.int32, (PEER_TOPK, n), 0)
    iota8 = lax.broadcasted_iota(jnp.int32, (SUBLANES, n), 0)
    idx_rows, gate_rows = [], []
    for hd in range(PEER_HEADS):
        tv, ti = [], []
        for half in range(2):
            r0 = hd * PEER_DKEY + half * half_dim
            s = jnp.dot(keys_ref[half], qt[r0:r0 + half_dim, :], preferred_element_type=jnp.float32)
            v, i = _extract_topk(s, key_iota, PEER_TOPK)
            tv.append(v)
            ti.append(i)
        cs = [tv[0][0:1, :] + tv[1]]
        cp = [iota16]
        ce = [ti[0][0:1, :] * PEER_NKEYS + ti[1]]
        for a in range(1, SUBLANES):
            cs.append(tv[0][a:a + 1, :] + tv[1][0:SUBLANES, :])
            cp.append(iota8 + a * PEER_TOPK)
            ce.append(ti[0][a:a + 1, :] * PEER_NKEYS + ti[1][0:SUBLANES, :])
        cs.append(tv[0][SUBLANES:, :] + tv[1][0:1, :])
        cp.append((iota8 + SUBLANES) * PEER_TOPK)
        ce.append(ti[0][SUBLANES:, :] * PEER_NKEYS + ti[1][0:1, :])
        best_s, _, best_e = _extract_topk(jnp.concatenate(cs, axis=0), jnp.concatenate(cp, axis=0),
                                          PEER_TOPK, payload=jnp.concatenate(ce, axis=0))
        ex = jnp.exp(best_s - best_s[0:1, :])
        gate_rows.append(ex / jnp.sum(ex, axis=0, keepdims=True))
        idx_rows.append(best_e)
    idx_ref[...] = jnp.concatenate(idx_rows, axis=0).T
    gate_ref[...] = jnp.concatenate(gate_rows, axis=0).T


def _peer_route(h, wqt, keys):
    t = h.shape[0]
    rt = ROUTE_TOKENS
    return pl.pallas_call(
        _peer_route_kernel,
        out_shape=(jax.ShapeDtypeStruct((t, PEER_PAIRS), jnp.int32),
                   jax.ShapeDtypeStruct((t, PEER_PAIRS), jnp.float32)),
        grid=(t // rt,),
        in_specs=[
            pl.BlockSpec((rt, D_MODEL), lambda i: (i, 0)),
            pl.BlockSpec((D_MODEL, D_MODEL), lambda i: (0, 0)),
            pl.BlockSpec((2, PEER_NKEYS, PEER_DKEY // 2), lambda i: (0, 0, 0)),
        ],
        out_specs=(pl.BlockSpec((rt, PEER_PAIRS), lambda i: (i, 0)),
                   pl.BlockSpec((rt, PEER_PAIRS), lambda i: (i, 0))),
        compiler_params=pltpu.CompilerParams(vmem_limit_bytes=48 * 1024 * 1024),
        name="peer_route",
    )(h, wqt, keys)


def _pack_table(u):
    ub = u.astype(jnp.bfloat16)
    half = D_MODEL // 2
    lo = lax.bitcast_convert_type(ub[:, :half], jnp.uint16).astype(jnp.uint32)
    hi = lax.bitcast_convert_type(ub[:, half:], jnp.uint16).astype(jnp.uint32)
    return (lo | (hi << 16)).reshape(u.shape[0] * ROW_WORDS_SUBLANES, LANES)


def _peer_ffn(h2, wq, keys, u, v):
    t = h2.shape[0]
    idx, gate = _peer_route(h2, wq.T, keys)
    w = _peer_in(idx, h2.reshape(t * TOKEN_SUBLANES, LANES), gate, _pack_table(u))
    return _peer_out(idx, w, _pack_table(v)).reshape(t, D_MODEL)


def kernel(x, emb_ln_g, emb_ln_b, w_in, rg_conv_w, rg_conv_b, rg_wa, rg_ba, rg_wx, rg_bx, rg_lambda, ssm_conv_w, ssm_conv_b, ssm_dt_bias, ssm_a_log, ssm_d, ssm_norm_g, attn_sinks, w_out, ln1_g, ln1_b, peer_wq, peer_keys, peer_u, peer_v, ln2_g, ln2_b):
    bsz, seq, d = x.shape
    h = layer_norm(x, emb_ln_g, emb_ln_b)
    for l in range(DEPTH):
        proj = h @ w_in[l]
        rg_x, rg_gate, z, xbc, dt_raw, q, k, v = jnp.split(proj, SPLIT_POINTS, axis=-1)
        y_a = jax.nn.gelu(rg_gate) * rg_lru(causal_dwconv(rg_x, rg_conv_w[l], rg_conv_b[l]),
                                            rg_wa[l], rg_ba[l], rg_wx[l], rg_bx[l], rg_lambda[l])
        y_b = mamba2_group(z, xbc, dt_raw, ssm_conv_w[l], ssm_conv_b[l], ssm_dt_bias[l],
                           ssm_a_log[l], ssm_d[l], ssm_norm_g[l])
        y_c = sliding_window_attention(q, k, v, attn_sinks[l])
        mix = jnp.concatenate([y_a.astype(h.dtype), y_b, y_c.astype(h.dtype)], axis=-1) @ w_out[l]
        h = layer_norm(DN_ALPHA * h + mix, ln1_g[l], ln1_b[l])
        ffn = _peer_ffn(h.reshape(bsz * seq, d), peer_wq[l], peer_keys[l], peer_u[l], peer_v[l])
        h = layer_norm(DN_ALPHA * h + ffn.reshape(bsz, seq, d), ln2_g[l], ln2_b[l])
    return h
```

```python
import math

import jax
import jax.numpy as jnp
from jax import lax
from jax.experimental import pallas as pl
from jax.experimental.pallas import tpu as pltpu

D_MODEL = 1024
DEPTH = 2

W_LRU = D_MODEL // 2
LRU_BLOCKS = 8
LRU_C = 8.0
CONV_K = 4

SSM_HEAD_DIM = 64
SSM_D_INNER = D_MODEL
SSM_HEADS = SSM_D_INNER // SSM_HEAD_DIM
SSM_GROUPS = 2
SSM_STATE = 128
SSM_CHUNK = 128
SSM_CONV_CH = SSM_D_INNER + 2 * SSM_GROUPS * SSM_STATE
SSD_GROUP_W = SSM_D_INNER // SSM_GROUPS

ATTN_HEAD_DIM = 64
ATTN_Q_HEADS = (D_MODEL // 2) // ATTN_HEAD_DIM
ATTN_KV_HEADS = 2
ATTN_REP = ATTN_Q_HEADS // ATTN_KV_HEADS
ATTN_BLOCK = 128
ATTN_Q_W = ATTN_Q_HEADS * ATTN_HEAD_DIM
ATTN_KV_W = 2 * ATTN_KV_HEADS * ATTN_HEAD_DIM

MIX_WIDTH = W_LRU + SSM_D_INNER + ATTN_Q_W

PEER_HEADS = 8
PEER_NKEYS = 128
PEER_DKEY = 128
PEER_TOPK = 16
PEER_PAIRS = PEER_HEADS * PEER_TOPK

DN_ALPHA = (2 * DEPTH) ** 0.25
LN_EPS = 1e-5

SUBLANES = 8
LANES = 128
HALO = SUBLANES
ROW_WORDS_SUBLANES = D_MODEL // 2 // LANES
TOKEN_SUBLANES = D_MODEL // LANES

ROW_TILE = 512
GATHER_TOKENS = 64
ROUTE_TOKENS = 256
VMEM_LIMIT = 48 * 1024 * 1024
PEER_VMEM_LIMIT = 56 * 1024 * 1024

BF = jnp.bfloat16
F32 = jnp.float32
HI = lax.Precision.HIGHEST

PROJ_SEGS = (2 * W_LRU, SSM_D_INNER, SSM_CONV_CH, LANES, ATTN_Q_W, ATTN_KV_W)


def _cparams(sem):
    return pltpu.CompilerParams(dimension_semantics=sem, vmem_limit_bytes=VMEM_LIMIT)


def _softplus(x):
    return jnp.maximum(x, 0.0) + jnp.log1p(jnp.exp(-jnp.abs(x)))


def _sigmoid(x):
    return 1.0 / (1.0 + jnp.exp(-x))


def _silu(x):
    return x * _sigmoid(x)


def _gelu_tanh(x):
    return 0.5 * x * (1.0 + jnp.tanh(math.sqrt(2.0 / math.pi) * (x + 0.044715 * (x * x * x))))


def _ln_math(x, g, b):
    mu = jnp.mean(x, axis=-1, keepdims=True)
    xc = x - mu
    var = jnp.mean(xc * xc, axis=-1, keepdims=True)
    return xc * lax.rsqrt(var + LN_EPS) * g + b


def _ln_kernel(a_ref, g_ref, b_ref, o_ref):
    o_ref[...] = _ln_math(a_ref[...], g_ref[...], b_ref[...])


def _res_ln_kernel(a_ref, r_ref, g_ref, b_ref, o_ref):
    o_ref[...] = _ln_math(DN_ALPHA * a_ref[...] + r_ref[...], g_ref[...], b_ref[...])


def _layer_norm(a, g, b, res=None):
    t, d = a.shape
    row = pl.BlockSpec((ROW_TILE, d), lambda i: (i, 0))
    vec = pl.BlockSpec((1, d), lambda i: (0, 0))
    if res is None:
        kern, args, specs = _ln_kernel, (a,), [row]
    else:
        kern, args, specs = _res_ln_kernel, (a, res), [row, row]
    return pl.pallas_call(
        kern, out_shape=jax.ShapeDtypeStruct((t, d), F32), grid=(t // ROW_TILE,),
        in_specs=specs + [vec, vec], out_specs=row, compiler_params=_cparams(("parallel",)),
        name="layer_norm",
    )(*args, g.reshape(1, d), b.reshape(1, d))


def _inproj_kernel(h_ref, *refs):
    n = len(PROJ_SEGS)
    hb = h_ref[...].astype(BF)
    for w_ref, o_ref in zip(refs[:n], refs[n:]):
        o_ref[...] = jnp.dot(hb, w_ref[...], preferred_element_type=F32)


def _in_proj(h, ws):
    t = h.shape[0]
    return pl.pallas_call(
        _inproj_kernel,
        out_shape=tuple(jax.ShapeDtypeStruct((t, w), F32) for w in PROJ_SEGS),
        grid=(t // ROW_TILE,),
        in_specs=[pl.BlockSpec((ROW_TILE, D_MODEL), lambda i: (i, 0))]
        + [pl.BlockSpec((D_MODEL, w), lambda i: (0, 0)) for w in PROJ_SEGS],
        out_specs=tuple(pl.BlockSpec((ROW_TILE, w), lambda i: (i, 0)) for w in PROJ_SEGS),
        compiler_params=_cparams(("parallel",)),
        name="in_proj",
    )(h, *ws)


def _split_w_in(w_in):
    w = w_in.astype(BF)
    c = [0]
    for width in (2 * W_LRU, SSM_D_INNER, SSM_CONV_CH, SSM_HEADS, ATTN_Q_W, ATTN_KV_W):
        c.append(c[-1] + width)
    segs = [w[:, c[i]:c[i + 1]] for i in range(6)]
    segs[3] = jnp.pad(segs[3], ((0, 0), (0, LANES - SSM_HEADS)))
    return tuple(segs)


def _shift_rows(x, s, fill):
    n = x.shape[0]
    if s % SUBLANES == 0:
        return jnp.concatenate([jnp.full((s,) + x.shape[1:], fill, x.dtype), x[:n - s]], axis=0)
    rolled = pltpu.roll(x, s, axis=0)
    row = lax.broadcasted_iota(jnp.int32, x.shape, 0)
    return jnp.where(row < s, fill, rolled)


def _causal_conv(x, halo, w_ref, b_ref):
    n = x.shape[0]
    xp = jnp.concatenate([halo, x], axis=0)
    out = b_ref[...] + w_ref[CONV_K - 1:CONV_K, :] * x
    for s in range(1, CONV_K):
        out = out + w_ref[CONV_K - 1 - s:CONV_K - s, :] * pltpu.roll(xp, s, axis=0)[HALO:HALO + n]
    return out


def _rglru_kernel(rg_ref, cw_ref, cb_ref, wa_ref, ba_ref, wx_ref, bx_ref, lam_ref, o_ref, halo_ref, carry_ref):
    n = rg_ref.shape[0]

    @pl.when(pl.program_id(1) == 0)
    def _():
        halo_ref[...] = jnp.zeros_like(halo_ref)
        carry_ref[...] = jnp.zeros_like(carry_ref)

    x = rg_ref[:, 0:W_LRU]
    xc = _causal_conv(x, halo_ref[...], cw_ref, cb_ref)
    halo_ref[...] = x[n - HALO:n]
    xb = xc.astype(BF)
    r = _sigmoid(jnp.dot(xb, wa_ref[...], preferred_element_type=F32) + ba_ref[...])
    i = _sigmoid(jnp.dot(xb, wx_ref[...], preferred_element_type=F32) + bx_ref[...])
    log_a = (-LRU_C * r) * _softplus(-lam_ref[...])
    a = jnp.exp(log_a)
    u = jnp.sqrt(-jnp.tanh(log_a) * (a * a + 1.0)) * (i * xc)
    k = 1
    while k < n:
        u = a * _shift_rows(u, k, 0.0) + u
        a = a * _shift_rows(a, k, 1.0)
        k *= 2
    h = u + a * carry_ref[0:1, :]
    carry_ref[...] = jnp.broadcast_to(h[n - 1:n, :], carry_ref.shape)
    o_ref[...] = _gelu_tanh(rg_ref[:, W_LRU:2 * W_LRU]) * h


def _rg_lru(rg, seq, conv_w, conv_b, wa, ba, wx, bx, lam):
    t = rg.shape[0]
    nb = seq // ROW_TILE
    row = lambda b, j: (b * nb + j, 0)
    const = lambda b, j: (0, 0)
    vec = pl.BlockSpec((1, W_LRU), const)
    return pl.pallas_call(
        _rglru_kernel,
        out_shape=jax.ShapeDtypeStruct((t, W_LRU), F32),
        grid=(t // seq, nb),
        in_specs=[pl.BlockSpec((ROW_TILE, 2 * W_LRU), row),
                  pl.BlockSpec((CONV_K, W_LRU), const), vec,
                  pl.BlockSpec((W_LRU, W_LRU), const), vec,
                  pl.BlockSpec((W_LRU, W_LRU), const), vec, vec],
        out_specs=pl.BlockSpec((ROW_TILE, W_LRU), row),
        scratch_shapes=[pltpu.VMEM((HALO, W_LRU), F32), pltpu.VMEM((SUBLANES, W_LRU), F32)],
        compiler_params=_cparams(("parallel", "arbitrary")),
        name="rg_lru",
    )(rg, conv_w, conv_b.reshape(1, -1), wa, ba.reshape(1, -1), wx, bx.reshape(1, -1), lam.reshape(1, -1))


def _block_diag(w):
    nb, c, _ = w.shape
    eye = jnp.eye(nb, dtype=w.dtype)
    return (eye[:, None, :, None] * w[:, :, None, :]).reshape(nb * c, nb * c).astype(BF)


def _ssd_kernel(z_ref, xbc_ref, dt_ref, cw_ref, cb_ref, dtb_ref, alog_ref, dskip_ref, ng_ref, expand_ref,
                o_ref, halo_ref, state_ref):
    L = SSM_CHUNK

    @pl.when(pl.program_id(1) == 0)
    def _():
        halo_ref[...] = jnp.zeros_like(halo_ref)
        state_ref[...] = jnp.zeros_like(state_ref)

    xbc = xbc_ref[...]
    conv = _silu(_causal_conv(xbc, halo_ref[...], cw_ref, cb_ref))
    halo_ref[...] = xbc[L - HALO:L]
    xs = conv[:, 0:SSM_D_INNER]
    bm = conv[:, SSM_D_INNER:SSM_D_INNER + SSM_GROUPS * SSM_STATE]
    cm = conv[:, SSM_D_INNER + SSM_GROUPS * SSM_STATE:]

    dt = _softplus(dt_ref[...] + dtb_ref[...])
    da = dt * (-jnp.exp(alog_ref[...]))
    row = lax.broadcasted_iota(jnp.int32, (L, L), 0)
    col = lax.broadcasted_iota(jnp.int32, (L, L), 1)
    causal = col <= row
    a_cs = jnp.dot(causal.astype(F32), da, precision=HI, preferred_element_type=F32)
    a_cs_t = a_cs.T
    expand = expand_ref[...]
    dt_full = jnp.dot(dt, expand, precision=HI, preferred_element_type=F32)
    acs_full = jnp.dot(a_cs, expand, precision=HI, preferred_element_type=F32)
    alast_full = acs_full[L - 1:L, :]
    xdt = xs * dt_full
    xdt_b = xdt.astype(BF)
    xst_b = (xdt * jnp.exp(alast_full - acs_full)).astype(BF)
    left = lax.broadcasted_iota(jnp.int32, (L, LANES), 1) < SSM_HEAD_DIM

    y_parts = []
    for g in range(SSM_GROUPS):
        cg = cm[:, g * SSM_STATE:(g + 1) * SSM_STATE].astype(BF)
        bg = bm[:, g * SSM_STATE:(g + 1) * SSM_STATE].astype(BF)
        cb = lax.dot_general(cg, bg, (((1,), (1,)), ((), ())), preferred_element_type=F32)
        st = state_ref[:, g * SSD_GROUP_W:(g + 1) * SSD_GROUP_W]
        y_off = jnp.dot(cg, st.astype(BF), preferred_element_type=F32)
        for j in range(SSD_GROUP_W // LANES):
            h0 = g * (SSM_HEADS // SSM_GROUPS) + 2 * j
            ms = []
            for h in (h0, h0 + 1):
                seg = a_cs[:, h:h + 1] - a_cs_t[h:h + 1, :]
                ms.append((cb * jnp.exp(jnp.where(causal, seg, -jnp.inf))).astype(BF))
            c0 = h0 * SSM_HEAD_DIM
            x2 = xdt_b[:, c0:c0 + LANES]
            zero = jnp.zeros_like(x2)
            xblk = jnp.concatenate([jnp.where(left, x2, zero), jnp.where(left, zero, x2)], axis=0)
            y_parts.append(jnp.dot(jnp.concatenate(ms, axis=1), xblk, preferred_element_type=F32)
                           + y_off[:, j * LANES:(j + 1) * LANES] * jnp.exp(acs_full[:, c0:c0 + LANES]))
        new = lax.dot_general(bg, xst_b[:, g * SSD_GROUP_W:(g + 1) * SSD_GROUP_W], (((0,), (0,)), ((), ())),
                              preferred_element_type=F32)
        state_ref[:, g * SSD_GROUP_W:(g + 1) * SSD_GROUP_W] = (
            st * jnp.exp(alast_full[:, g * SSD_GROUP_W:(g + 1) * SSD_GROUP_W]) + new)
    y = jnp.concatenate(y_parts, axis=1) + dskip_ref[...] * xs
    y = y * _silu(z_ref[...])
    outs = []
    for g in range(SSM_GROUPS):
        yg = y[:, g * SSD_GROUP_W:(g + 1) * SSD_GROUP_W]
        outs.append(yg * lax.rsqrt(jnp.mean(yg * yg, axis=-1, keepdims=True) + LN_EPS))
    o_ref[...] = jnp.concatenate(outs, axis=1) * ng_ref[...]


def _ssd(z, xbc, dt, seq, conv_w, conv_b, dt_bias, a_log, d_skip, norm_g):
    t = z.shape[0]
    nc = seq // SSM_CHUNK
    row = lambda b, c: (b * nc + c, 0)
    const = lambda b, c: (0, 0)
    pad = LANES - SSM_HEADS
    expand = (jnp.arange(LANES)[:, None] == (jnp.arange(SSM_D_INNER)[None, :] // SSM_HEAD_DIM)).astype(F32)
    return pl.pallas_call(
        _ssd_kernel,
        out_shape=jax.ShapeDtypeStruct((t, SSM_D_INNER), F32),
        grid=(t // seq, nc),
        in_specs=[pl.BlockSpec((SSM_CHUNK, SSM_D_INNER), row),
                  pl.BlockSpec((SSM_CHUNK, SSM_CONV_CH), row),
                  pl.BlockSpec((SSM_CHUNK, LANES), row),
                  pl.BlockSpec((CONV_K, SSM_CONV_CH), const),
                  pl.BlockSpec((1, SSM_CONV_CH), const),
                  pl.BlockSpec((1, LANES), const),
                  pl.BlockSpec((1, LANES), const),
                  pl.BlockSpec((1, SSM_D_INNER), const),
                  pl.BlockSpec((1, SSM_D_INNER), const),
                  pl.BlockSpec((LANES, SSM_D_INNER), const)],
        out_specs=pl.BlockSpec((SSM_CHUNK, SSM_D_INNER), row),
        scratch_shapes=[pltpu.VMEM((HALO, SSM_CONV_CH), F32), pltpu.VMEM((SSM_STATE, SSM_D_INNER), F32)],
        compiler_params=_cparams(("parallel", "arbitrary")),
        name="ssd",
    )(z, xbc, dt, conv_w, conv_b.reshape(1, -1), jnp.pad(dt_bias, (0, pad)).reshape(1, -1),
      jnp.pad(a_log, (0, pad)).reshape(1, -1), jnp.repeat(d_skip, SSM_HEAD_DIM).reshape(1, -1),
      norm_g.reshape(1, -1), expand)


def _swa_kernel(sink_ref, q_ref, kv_ref, kvp_ref, o_ref):
    L = ATTN_BLOCK
    hd = ATTN_HEAD_DIM
    first = pl.program_id(1) == 0
    kv = jnp.concatenate([kvp_ref[...], kv_ref[...]], axis=0)
    kk = kv[:, 0:LANES]
    vv = kv[:, LANES:2 * LANES].astype(BF)
    lane_k = lax.broadcasted_iota(jnp.int32, (2 * L, LANES), 1)
    qi = lax.broadcasted_iota(jnp.int32, (2 * L, 2 * L), 0) % L
    kj = lax.broadcasted_iota(jnp.int32, (2 * L, 2 * L), 1)
    rel = qi + L - kj
    valid = (rel >= 0) & (rel < L) & (jnp.logical_not(first) | (kj >= L))
    top = lax.broadcasted_iota(jnp.int32, (2 * L, 1), 0) < L
    lane_o = lax.broadcasted_iota(jnp.int32, (L, LANES), 1)
    scale = hd ** -0.5
    for j in range(ATTN_Q_HEADS // 2):
        g = (2 * j) // ATTN_REP
        q2 = q_ref[:, j * LANES:(j + 1) * LANES]
        q2r = pltpu.roll(q2, hd, axis=1)
        kg = jnp.where((lane_k >= g * hd) & (lane_k < (g + 1) * hd), kk, 0.0).astype(BF)
        qa, qb = (q2, q2r) if g == 0 else (q2r, q2)
        qs = jnp.concatenate([qa, qb], axis=0).astype(BF)
        logits = lax.dot_general(qs, kg, (((1,), (1,)), ((), ())), preferred_element_type=F32) * scale
        logits = jnp.where(valid, logits, -jnp.inf)
        sink = jnp.where(top, sink_ref[2 * j], sink_ref[2 * j + 1])
        m = jnp.maximum(jnp.max(logits, axis=-1, keepdims=True), sink)
        p = jnp.exp(logits - m)
        probs = p / (jnp.sum(p, axis=-1, keepdims=True) + jnp.exp(sink - m))
        o = jnp.dot(probs.astype(BF), vv, preferred_element_type=F32)
        oa, ob = o[0:L], o[L:2 * L]
        if g == 0:
            out2 = jnp.where(lane_o < hd, oa, pltpu.roll(ob, hd, axis=1))
        else:
            out2 = jnp.where(lane_o < hd, pltpu.roll(oa, hd, axis=1), ob)
        o_ref[:, j * LANES:(j + 1) * LANES] = out2


def _swa(q, kv, seq, sinks):
    t = q.shape[0]
    nb = seq // ATTN_BLOCK
    return pl.pallas_call(
        _swa_kernel,
        out_shape=jax.ShapeDtypeStruct((t, ATTN_Q_W), F32),
        grid=(t // seq, nb),
        in_specs=[pl.BlockSpec(memory_space=pltpu.SMEM),
                  pl.BlockSpec((ATTN_BLOCK, ATTN_Q_W), lambda b, n: (b * nb + n, 0)),
                  pl.BlockSpec((ATTN_BLOCK, ATTN_KV_W), lambda b, n: (b * nb + n, 0)),
                  pl.BlockSpec((ATTN_BLOCK, ATTN_KV_W), lambda b, n: (b * nb + jnp.maximum(n - 1, 0), 0))],
        out_specs=pl.BlockSpec((ATTN_BLOCK, ATTN_Q_W), lambda b, n: (b * nb + n, 0)),
        compiler_params=_cparams(("parallel", "parallel")),
        name="swa",
    )(sinks, q, kv, kv)


def _outproj_kernel(ya_ref, yb_ref, yc_ref, h_ref, wa_ref, wb_ref, wc_ref, g_ref, b_ref, o_ref):
    mix = (jnp.dot(ya_ref[...].astype(BF), wa_ref[...], preferred_element_type=F32)
           + jnp.dot(yb_ref[...].astype(BF), wb_ref[...], preferred_element_type=F32)
           + jnp.dot(yc_ref[...].astype(BF), wc_ref[...], preferred_element_type=F32))
    o_ref[...] = _ln_math(DN_ALPHA * h_ref[...] + mix, g_ref[...], b_ref[...])


def _out_proj(ya, yb, yc, h, w_out, g, b):
    t = h.shape[0]
    w = w_out.astype(BF)
    row = lambda i: (i, 0)
    const = lambda i: (0, 0)
    c1, c2 = W_LRU, W_LRU + SSM_D_INNER
    return pl.pallas_call(
        _outproj_kernel,
        out_shape=jax.ShapeDtypeStruct((t, D_MODEL), F32),
        grid=(t // ROW_TILE,),
        in_specs=[pl.BlockSpec((ROW_TILE, W_LRU), row), pl.BlockSpec((ROW_TILE, SSM_D_INNER), row),
                  pl.BlockSpec((ROW_TILE, ATTN_Q_W), row), pl.BlockSpec((ROW_TILE, D_MODEL), row),
                  pl.BlockSpec((W_LRU, D_MODEL), const), pl.BlockSpec((SSM_D_INNER, D_MODEL), const),
                  pl.BlockSpec((ATTN_Q_W, D_MODEL), const),
                  pl.BlockSpec((1, D_MODEL), const), pl.BlockSpec((1, D_MODEL), const)],
        out_specs=pl.BlockSpec((ROW_TILE, D_MODEL), row),
        compiler_params=_cparams(("parallel",)),
        name="out_proj",
    )(ya, yb, yc, h, w[0:c1], w[c1:c2], w[c2:MIX_WIDTH], g.reshape(1, -1), b.reshape(1, -1))


def _unpack_bf16_pair(w):
    lo = lax.bitcast_convert_type(w << 16, jnp.float32)
    hi = lax.bitcast_convert_type(w & jnp.uint32(0xFFFF0000), jnp.float32)
    return lo, hi


def _gather_row(tbl_ref, idx):
    r = pl.multiple_of(idx * ROW_WORDS_SUBLANES, ROW_WORDS_SUBLANES)
    return tbl_ref[pl.ds(r, ROW_WORDS_SUBLANES), :]


def _peer_in_kernel(idx_ref, x_ref, gate_ref, tbl_ref, o_ref, prod_ref):
    ones = jnp.ones((LANES, LANES), jnp.bfloat16)
    eye = (lax.broadcasted_iota(jnp.int32, (PEER_PAIRS, LANES), 0)
           == lax.broadcasted_iota(jnp.int32, (PEER_PAIRS, LANES), 1))
    rs = ROW_WORDS_SUBLANES

    def token(t, carry):
        base = pl.multiple_of(t * TOKEN_SUBLANES, TOKEN_SUBLANES)
        x_lo = x_ref[pl.ds(base, rs), :]
        x_hi = x_ref[pl.ds(base + rs, rs), :]
        for p in range(PEER_PAIRS):
            lo, hi = _unpack_bf16_pair(_gather_row(tbl_ref, idx_ref[t, p]))
            prod_ref[pl.ds(rs * p, rs), :] = lo * x_lo + hi * x_hi
        q = prod_ref[pl.ds(0, PEER_PAIRS, stride=rs), :]
        for s in range(1, rs):
            q = q + prod_ref[pl.ds(s, PEER_PAIRS, stride=rs), :]
        q_hi = q.astype(jnp.bfloat16)
        q_lo = (q - q_hi.astype(jnp.float32)).astype(jnp.bfloat16)
        r = (jnp.dot(q_hi, ones, preferred_element_type=jnp.float32)
             + jnp.dot(q_lo, ones, preferred_element_type=jnp.float32))
        o_ref[pl.ds(t, 1), :] = jnp.sum(jnp.where(eye, r, 0.0), axis=0, keepdims=True)
        return carry

    lax.fori_loop(0, GATHER_TOKENS, token, 0)
    a = o_ref[...]
    o_ref[...] = gate_ref[...] * (0.5 * a * (1.0 + lax.erf(a * (2.0 ** -0.5))))


def _peer_in(idx, x8, gate, tbl):
    t = idx.shape[0]
    tt = GATHER_TOKENS
    return pl.pallas_call(
        _peer_in_kernel,
        out_shape=jax.ShapeDtypeStruct((t, PEER_PAIRS), jnp.float32),
        grid=(t // tt,),
        in_specs=[
            pl.BlockSpec((tt, PEER_PAIRS), lambda i: (i, 0), memory_space=pltpu.SMEM),
            pl.BlockSpec((tt * TOKEN_SUBLANES, LANES), lambda i: (i, 0)),
            pl.BlockSpec((tt, PEER_PAIRS), lambda i: (i, 0)),
            pl.BlockSpec(memory_space=pltpu.VMEM),
        ],
        out_specs=pl.BlockSpec((tt, PEER_PAIRS), lambda i: (i, 0)),
        scratch_shapes=[pltpu.VMEM((ROW_WORDS_SUBLANES * PEER_PAIRS, LANES), jnp.float32)],
        compiler_params=pltpu.CompilerParams(vmem_limit_bytes=PEER_VMEM_LIMIT),
        name="peer_in",
    )(idx, x8, gate, tbl)


def _peer_out_kernel(idx_ref, w_ref, tbl_ref, o_ref):
    rs = ROW_WORDS_SUBLANES
    n_acc = 4

    def token(t, carry):
        base = pl.multiple_of(t * TOKEN_SUBLANES, TOKEN_SUBLANES)
        acc_lo = [jnp.zeros((rs, LANES), jnp.float32) for _ in range(n_acc)]
        acc_hi = [jnp.zeros((rs, LANES), jnp.float32) for _ in range(n_acc)]
        for p in range(PEER_PAIRS):
            lo, hi = _unpack_bf16_pair(_gather_row(tbl_ref, idx_ref[t, p]))
            g = w_ref[t, p]
            acc_lo[p % n_acc] = acc_lo[p % n_acc] + g * lo
            acc_hi[p % n_acc] = acc_hi[p % n_acc] + g * hi
        o_ref[pl.ds(base, rs), :] = (acc_lo[0] + acc_lo[1]) + (acc_lo[2] + acc_lo[3])
        o_ref[pl.ds(base + rs, rs), :] = (acc_hi[0] + acc_hi[1]) + (acc_hi[2] + acc_hi[3])
        return carry

    lax.fori_loop(0, GATHER_TOKENS, token, 0)


def _peer_out(idx, w, tbl):
    t = idx.shape[0]
    tt = GATHER_TOKENS
    return pl.pallas_call(
        _peer_out_kernel,
        out_shape=jax.ShapeDtypeStruct((t * TOKEN_SUBLANES, LANES), jnp.float32),
        grid=(t // tt,),
        in_specs=[
            pl.BlockSpec((tt, PEER_PAIRS), lambda i: (i, 0), memory_space=pltpu.SMEM),
            pl.BlockSpec((tt, PEER_PAIRS), lambda i: (i, 0), memory_space=pltpu.SMEM),
            pl.BlockSpec(memory_space=pltpu.VMEM),
        ],
        out_specs=pl.BlockSpec((tt * TOKEN_SUBLANES, LANES), lambda i: (i, 0)),
        compiler_params=pltpu.CompilerParams(vmem_limit_bytes=PEER_VMEM_LIMIT),
        name="peer_out",
    )(idx, w, tbl)


def _extract_topk(s, pos, k, payload=None):
    big = jnp.int32(2 ** 30)
    vals, poss, pays = [], [], []
    for _ in range(k):
        m = jnp.max(s, axis=0, keepdims=True)
        j = jnp.min(jnp.where(s == m, pos, big), axis=0, keepdims=True)
        sel = pos == j
        vals.append(m)
        poss.append(j)
        if payload is not None:
            pays.append(jnp.sum(jnp.where(sel, payload, 0), axis=0, keepdims=True))
        s = jnp.where(sel, -jnp.inf, s)
    out = [jnp.concatenate(vals, axis=0), jnp.concatenate(poss, axis=0)]
    if payload is not None:
        out.append(jnp.concatenate(pays, axis=0))
    return out


def _peer_route_kernel(h_ref, wqt_ref, keys_ref, idx_ref, gate_ref):
    half_dim = PEER_DKEY // 2
    h = h_ref[...]
    n = h.shape[0]
    qt = lax.dot_general(wqt_ref[...], h, (((1,), (1,)), ((), ())), preferred_element_type=jnp.float32)
    key_iota = lax.broadcasted_iota(jnp.int32, (PEER_NKEYS, n), 0)
    iota16 = lax.broadcasted_iota(jnp.int32, (PEER_TOPK, n), 0)
    iota8 = lax.broadcasted_iota(jnp.int32, (SUBLANES, n), 0)
    idx_rows, gate_rows = [], []
    for hd in range(PEER_HEADS):
        tv, ti = [], []
        for half in range(2):
            r0 = hd * PEER_DKEY + half * half_dim
            s = jnp.dot(keys_ref[half], qt[r0:r0 + half_dim, :], preferred_element_type=jnp.float32)
            v, i = _extract_topk(s, key_iota, PEER_TOPK)
            tv.append(v)
            ti.append(i)
        cs = [tv[0][0:1, :] + tv[1]]
        cp = [iota16]
        ce = [ti[0][0:1, :] * PEER_NKEYS + ti[1]]
        for a in range(1, SUBLANES):
            cs.append(tv[0][a:a + 1, :] + tv[1][0:SUBLANES, :])
            cp.append(iota8 + a * PEER_TOPK)
            ce.append(ti[0][a:a + 1, :] * PEER_NKEYS + ti[1][0:SUBLANES, :])
        cs.append(tv[0][SUBLANES:, :] + tv[1][0:1, :])
        cp.append((iota8 + SUBLANES) * PEER_TOPK)
        ce.append(ti[0][SUBLANES:, :] * PEER_NKEYS + ti[1][0:1, :])
        best_s, _, best_e = _extract_topk(jnp.concatenate(cs, axis=0), jnp.concatenate(cp, axis=0),
                                          PEER_TOPK, payload=jnp.concatenate(ce, axis=0))
        ex = jnp.exp(best_s - best_s[0:1, :])
        gate_rows.append(ex / jnp.sum(ex, axis=0, keepdims=True))
        idx_rows.append(best_e)
    idx_ref[...] = jnp.concatenate(idx_rows, axis=0).T
    gate_ref[...] = jnp.concatenate(gate_rows, axis=0).T


def _peer_route(h, wqt, keys):
    t = h.shape[0]
    rt = ROUTE_TOKENS
    return pl.pallas_call(
        _peer_route_kernel,
        out_shape=(jax.ShapeDtypeStruct((t, PEER_PAIRS), jnp.int32),
                   jax.ShapeDtypeStruct((t, PEER_PAIRS), jnp.float32)),
        grid=(t // rt,),
        in_specs=[
            pl.BlockSpec((rt, D_MODEL), lambda i: (i, 0)),
            pl.BlockSpec((D_MODEL, D_MODEL), lambda i: (0, 0)),
            pl.BlockSpec((2, PEER_NKEYS, PEER_DKEY // 2), lambda i: (0, 0, 0)),
        ],
        out_specs=(pl.BlockSpec((rt, PEER_PAIRS), lambda i: (i, 0)),
                   pl.BlockSpec((rt, PEER_PAIRS), lambda i: (i, 0))),
        compiler_params=_cparams(("parallel",)),
        name="peer_route",
    )(h, wqt, keys)


def _pack_table(u):
    ub = u.astype(jnp.bfloat16)
    half = D_MODEL // 2
    lo = lax.bitcast_convert_type(ub[:, :half], jnp.uint16).astype(jnp.uint32)
    hi = lax.bitcast_convert_type(ub[:, half:], jnp.uint16).astype(jnp.uint32)
    return (lo | (hi << 16)).reshape(u.shape[0] * ROW_WORDS_SUBLANES, LANES)


def _peer_ffn(h, wq, keys, u, v):
    t = h.shape[0]
    idx, gate = _peer_route(h, wq.T, keys)
    w = _peer_in(idx, h.reshape(t * TOKEN_SUBLANES, LANES), gate, _pack_table(u))
    return _peer_out(idx, w, _pack_table(v)).reshape(t, D_MODEL)


def kernel(x, emb_ln_g, emb_ln_b, w_in, rg_conv_w, rg_conv_b, rg_wa, rg_ba, rg_wx, rg_bx, rg_lambda, ssm_conv_w, ssm_conv_b, ssm_dt_bias, ssm_a_log, ssm_d, ssm_norm_g, attn_sinks, w_out, ln1_g, ln1_b, peer_wq, peer_keys, peer_u, peer_v, ln2_g, ln2_b):
    bsz, seq, d = x.shape
    assert d == D_MODEL and seq % ROW_TILE == 0 and (bsz * seq) % ROUTE_TOKENS == 0
    h = _layer_norm(x.reshape(bsz * seq, d), emb_ln_g, emb_ln_b)
    for l in range(DEPTH):
        rg, z, xbc, dt, q, kv = _in_proj(h, _split_w_in(w_in[l]))
        y_a = _rg_lru(rg, seq, rg_conv_w[l], rg_conv_b[l], _block_diag(rg_wa[l]), rg_ba[l],
                      _block_diag(rg_wx[l]), rg_bx[l], rg_lambda[l])
        y_b = _ssd(z, xbc, dt, seq, ssm_conv_w[l], ssm_conv_b[l], ssm_dt_bias[l], ssm_a_log[l], ssm_d[l],
                   ssm_norm_g[l])
        y_c = _swa(q, kv, seq, attn_sinks[l])
        h = _out_proj(y_a, y_b, y_c, h, w_out[l], ln1_g[l], ln1_b[l])
        ffn = _peer_ffn(h, peer_wq[l], peer_keys[l], peer_u[l], peer_v[l])
        h = _layer_norm(h, ln2_g[l], ln2_b[l], res=ffn)
    return h.reshape(bsz, seq, d)
```

```python
import math

import jax
import jax.numpy as jnp
from jax import lax
from jax.experimental import pallas as pl
from jax.experimental.pallas import tpu as pltpu

D_MODEL = 1024
DEPTH = 2

W_LRU = D_MODEL // 2
LRU_BLOCKS = 8
LRU_C = 8.0
CONV_K = 4

SSM_HEAD_DIM = 64
SSM_D_INNER = D_MODEL
SSM_HEADS = SSM_D_INNER // SSM_HEAD_DIM
SSM_GROUPS = 2
SSM_STATE = 128
SSM_CHUNK = 128
SSM_CONV_CH = SSM_D_INNER + 2 * SSM_GROUPS * SSM_STATE
SSD_GROUP_W = SSM_D_INNER // SSM_GROUPS

ATTN_HEAD_DIM = 64
ATTN_Q_HEADS = (D_MODEL // 2) // ATTN_HEAD_DIM
ATTN_KV_HEADS = 2
ATTN_REP = ATTN_Q_HEADS // ATTN_KV_HEADS
ATTN_BLOCK = 128
ATTN_Q_W = ATTN_Q_HEADS * ATTN_HEAD_DIM
ATTN_KV_W = 2 * ATTN_KV_HEADS * ATTN_HEAD_DIM

MIX_WIDTH = W_LRU + SSM_D_INNER + ATTN_Q_W

PEER_HEADS = 8
PEER_NKEYS = 128
PEER_DKEY = 128
PEER_TOPK = 16
PEER_PAIRS = PEER_HEADS * PEER_TOPK

DN_ALPHA = (2 * DEPTH) ** 0.25
LN_EPS = 1e-5

SUBLANES = 8
LANES = 128
HALO = SUBLANES
ROW_WORDS_SUBLANES = D_MODEL // 2 // LANES
TOKEN_SUBLANES = D_MODEL // LANES

ROW_TILE = 512
GATHER_TOKENS = 64
GATHER_GROUP = 2
ROUTE_TOKENS = 256
VMEM_LIMIT = 48 * 1024 * 1024
PEER_VMEM_LIMIT = 56 * 1024 * 1024

BF = jnp.bfloat16
F32 = jnp.float32
HI = lax.Precision.HIGHEST

PROJ_SEGS = (2 * W_LRU, SSM_D_INNER, SSM_CONV_CH, LANES, ATTN_Q_W, ATTN_KV_W)


def _cparams(sem):
    return pltpu.CompilerParams(dimension_semantics=sem, vmem_limit_bytes=VMEM_LIMIT)


def _softplus(x):
    return jnp.maximum(x, 0.0) + jnp.log1p(jnp.exp(-jnp.abs(x)))


def _sigmoid(x):
    return 1.0 / (1.0 + jnp.exp(-x))


def _silu(x):
    return x * _sigmoid(x)


def _gelu_tanh(x):
    return 0.5 * x * (1.0 + jnp.tanh(math.sqrt(2.0 / math.pi) * (x + 0.044715 * (x * x * x))))


def _ln_math(x, g, b):
    mu = jnp.mean(x, axis=-1, keepdims=True)
    xc = x - mu
    var = jnp.mean(xc * xc, axis=-1, keepdims=True)
    return xc * lax.rsqrt(var + LN_EPS) * g + b


def _ln_kernel(a_ref, g_ref, b_ref, o_ref):
    o_ref[...] = _ln_math(a_ref[...], g_ref[...], b_ref[...])


def _res_ln_kernel(a_ref, r_ref, g_ref, b_ref, o_ref):
    o_ref[...] = _ln_math(DN_ALPHA * a_ref[...] + r_ref[...], g_ref[...], b_ref[...])


def _layer_norm(a, g, b, res=None):
    t, d = a.shape
    row = pl.BlockSpec((ROW_TILE, d), lambda i: (i, 0))
    vec = pl.BlockSpec((1, d), lambda i: (0, 0))
    if res is None:
        kern, args, specs = _ln_kernel, (a,), [row]
    else:
        kern, args, specs = _res_ln_kernel, (a, res), [row, row]
    return pl.pallas_call(
        kern, out_shape=jax.ShapeDtypeStruct((t, d), F32), grid=(t // ROW_TILE,),
        in_specs=specs + [vec, vec], out_specs=row, compiler_params=_cparams(("parallel",)),
        name="layer_norm",
    )(*args, g.reshape(1, d), b.reshape(1, d))


def _inproj_kernel(h_ref, *refs):
    n = len(PROJ_SEGS)
    hb = h_ref[...].astype(BF)
    for w_ref, o_ref in zip(refs[:n], refs[n:]):
        o_ref[...] = jnp.dot(hb, w_ref[...], preferred_element_type=F32)


def _in_proj(h, ws):
    t = h.shape[0]
    return pl.pallas_call(
        _inproj_kernel,
        out_shape=tuple(jax.ShapeDtypeStruct((t, w), F32) for w in PROJ_SEGS),
        grid=(t // ROW_TILE,),
        in_specs=[pl.BlockSpec((ROW_TILE, D_MODEL), lambda i: (i, 0))]
        + [pl.BlockSpec((D_MODEL, w), lambda i: (0, 0)) for w in PROJ_SEGS],
        out_specs=tuple(pl.BlockSpec((ROW_TILE, w), lambda i: (i, 0)) for w in PROJ_SEGS),
        compiler_params=_cparams(("parallel",)),
        name="in_proj",
    )(h, *ws)


def _split_w_in(w_in):
    w = w_in.astype(BF)
    c = [0]
    for width in (2 * W_LRU, SSM_D_INNER, SSM_CONV_CH, SSM_HEADS, ATTN_Q_W, ATTN_KV_W):
        c.append(c[-1] + width)
    segs = [w[:, c[i]:c[i + 1]] for i in range(6)]
    segs[3] = jnp.pad(segs[3], ((0, 0), (0, LANES - SSM_HEADS)))
    return tuple(segs)


def _shift_rows(x, s, fill):
    n = x.shape[0]
    if s % SUBLANES == 0:
        return jnp.concatenate([jnp.full((s,) + x.shape[1:], fill, x.dtype), x[:n - s]], axis=0)
    rolled = pltpu.roll(x, s, axis=0)
    row = lax.broadcasted_iota(jnp.int32, x.shape, 0)
    return jnp.where(row < s, fill, rolled)


def _causal_conv(x, halo, w_ref, b_ref):
    n = x.shape[0]
    xp = jnp.concatenate([halo, x], axis=0)
    out = b_ref[...] + w_ref[CONV_K - 1:CONV_K, :] * x
    for s in range(1, CONV_K):
        out = out + w_ref[CONV_K - 1 - s:CONV_K - s, :] * pltpu.roll(xp, s, axis=0)[HALO:HALO + n]
    return out


def _rglru_kernel(rg_ref, cw_ref, cb_ref, wa_ref, ba_ref, wx_ref, bx_ref, lam_ref, o_ref, halo_ref, carry_ref):
    n = rg_ref.shape[0]

    @pl.when(pl.program_id(1) == 0)
    def _():
        halo_ref[...] = jnp.zeros_like(halo_ref)
        carry_ref[...] = jnp.zeros_like(carry_ref)

    x = rg_ref[:, 0:W_LRU]
    xc = _causal_conv(x, halo_ref[...], cw_ref, cb_ref)
    halo_ref[...] = x[n - HALO:n]
    xb = xc.astype(BF)
    r = _sigmoid(jnp.dot(xb, wa_ref[...], preferred_element_type=F32) + ba_ref[...])
    i = _sigmoid(jnp.dot(xb, wx_ref[...], preferred_element_type=F32) + bx_ref[...])
    log_a = (-LRU_C * r) * _softplus(-lam_ref[...])
    a = jnp.exp(log_a)
    u = jnp.sqrt(-jnp.tanh(log_a) * (a * a + 1.0)) * (i * xc)
    k = 1
    while k < n:
        u = a * _shift_rows(u, k, 0.0) + u
        a = a * _shift_rows(a, k, 1.0)
        k *= 2
    h = u + a * carry_ref[0:1, :]
    carry_ref[...] = jnp.broadcast_to(h[n - 1:n, :], carry_ref.shape)
    o_ref[...] = _gelu_tanh(rg_ref[:, W_LRU:2 * W_LRU]) * h


def _rg_lru(rg, seq, conv_w, conv_b, wa, ba, wx, bx, lam):
    t = rg.shape[0]
    nb = seq // ROW_TILE
    row = lambda b, j: (b * nb + j, 0)
    const = lambda b, j: (0, 0)
    vec = pl.BlockSpec((1, W_LRU), const)
    return pl.pallas_call(
        _rglru_kernel,
        out_shape=jax.ShapeDtypeStruct((t, W_LRU), F32),
        grid=(t // seq, nb),
        in_specs=[pl.BlockSpec((ROW_TILE, 2 * W_LRU), row),
                  pl.BlockSpec((CONV_K, W_LRU), const), vec,
                  pl.BlockSpec((W_LRU, W_LRU), const), vec,
                  pl.BlockSpec((W_LRU, W_LRU), const), vec, vec],
        out_specs=pl.BlockSpec((ROW_TILE, W_LRU), row),
        scratch_shapes=[pltpu.VMEM((HALO, W_LRU), F32), pltpu.VMEM((SUBLANES, W_LRU), F32)],
        compiler_params=_cparams(("parallel", "arbitrary")),
        name="rg_lru",
    )(rg, conv_w, conv_b.reshape(1, -1), wa, ba.reshape(1, -1), wx, bx.reshape(1, -1), lam.reshape(1, -1))


def _block_diag(w):
    nb, c, _ = w.shape
    eye = jnp.eye(nb, dtype=w.dtype)
    return (eye[:, None, :, None] * w[:, :, None, :]).reshape(nb * c, nb * c).astype(BF)


def _ssd_kernel(z_ref, xbc_ref, dt_ref, cw_ref, cb_ref, dtb_ref, alog_ref, dskip_ref, ng_ref, expand_ref,
                o_ref, halo_ref, state_ref):
    L = SSM_CHUNK

    @pl.when(pl.program_id(1) == 0)
    def _():
        halo_ref[...] = jnp.zeros_like(halo_ref)
        state_ref[...] = jnp.zeros_like(state_ref)

    xbc = xbc_ref[...]
    conv = _silu(_causal_conv(xbc, halo_ref[...], cw_ref, cb_ref))
    halo_ref[...] = xbc[L - HALO:L]
    xs = conv[:, 0:SSM_D_INNER]
    bm = conv[:, SSM_D_INNER:SSM_D_INNER + SSM_GROUPS * SSM_STATE]
    cm = conv[:, SSM_D_INNER + SSM_GROUPS * SSM_STATE:]

    dt = _softplus(dt_ref[...] + dtb_ref[...])
    da = dt * (-jnp.exp(alog_ref[...]))
    row = lax.broadcasted_iota(jnp.int32, (L, L), 0)
    col = lax.broadcasted_iota(jnp.int32, (L, L), 1)
    causal = col <= row
    a_cs = jnp.dot(causal.astype(F32), da, precision=HI, preferred_element_type=F32)
    a_cs_t = a_cs.T
    expand = expand_ref[...]
    dt_full = jnp.dot(dt, expand, precision=HI, preferred_element_type=F32)
    acs_full = jnp.dot(a_cs, expand, precision=HI, preferred_element_type=F32)
    alast_full = acs_full[L - 1:L, :]
    xdt = xs * dt_full
    xdt_b = xdt.astype(BF)
    xst_b = (xdt * jnp.exp(alast_full - acs_full)).astype(BF)
    left = lax.broadcasted_iota(jnp.int32, (L, LANES), 1) < SSM_HEAD_DIM

    y_parts = []
    for g in range(SSM_GROUPS):
        cg = cm[:, g * SSM_STATE:(g + 1) * SSM_STATE].astype(BF)
        bg = bm[:, g * SSM_STATE:(g + 1) * SSM_STATE].astype(BF)
        cb = lax.dot_general(cg, bg, (((1,), (1,)), ((), ())), preferred_element_type=F32)
        st = state_ref[:, g * SSD_GROUP_W:(g + 1) * SSD_GROUP_W]
        y_off = jnp.dot(cg, st.astype(BF), preferred_element_type=F32)
        for j in range(SSD_GROUP_W // LANES):
            h0 = g * (SSM_HEADS // SSM_GROUPS) + 2 * j
            ms = []
            for h in (h0, h0 + 1):
                seg = a_cs[:, h:h + 1] - a_cs_t[h:h + 1, :]
                ms.append((cb * jnp.exp(jnp.where(causal, seg, -jnp.inf))).astype(BF))
            c0 = h0 * SSM_HEAD_DIM
            x2 = xdt_b[:, c0:c0 + LANES]
            zero = jnp.zeros_like(x2)
            xblk = jnp.concatenate([jnp.where(left, x2, zero), jnp.where(left, zero, x2)], axis=0)
            y_parts.append(jnp.dot(jnp.concatenate(ms, axis=1), xblk, preferred_element_type=F32)
                           + y_off[:, j * LANES:(j + 1) * LANES] * jnp.exp(acs_full[:, c0:c0 + LANES]))
        new = lax.dot_general(bg, xst_b[:, g * SSD_GROUP_W:(g + 1) * SSD_GROUP_W], (((0,), (0,)), ((), ())),
                              preferred_element_type=F32)
        state_ref[:, g * SSD_GROUP_W:(g + 1) * SSD_GROUP_W] = (
            st * jnp.exp(alast_full[:, g * SSD_GROUP_W:(g + 1) * SSD_GROUP_W]) + new)
    y = jnp.concatenate(y_parts, axis=1) + dskip_ref[...] * xs
    y = y * _silu(z_ref[...])
    outs = []
    for g in range(SSM_GROUPS):
        yg = y[:, g * SSD_GROUP_W:(g + 1) * SSD_GROUP_W]
        outs.append(yg * lax.rsqrt(jnp.mean(yg * yg, axis=-1, keepdims=True) + LN_EPS))
    o_ref[...] = jnp.concatenate(outs, axis=1) * ng_ref[...]


def _ssd(z, xbc, dt, seq, conv_w, conv_b, dt_bias, a_log, d_skip, norm_g):
    t = z.shape[0]
    nc = seq // SSM_CHUNK
    row = lambda b, c: (b * nc + c, 0)
    const = lambda b, c: (0, 0)
    pad = LANES - SSM_HEADS
    expand = (jnp.arange(LANES)[:, None] == (jnp.arange(SSM_D_INNER)[None, :] // SSM_HEAD_DIM)).astype(F32)
    return pl.pallas_call(
        _ssd_kernel,
        out_shape=jax.ShapeDtypeStruct((t, SSM_D_INNER), F32),
        grid=(t // seq, nc),
        in_specs=[pl.BlockSpec((SSM_CHUNK, SSM_D_INNER), row),
                  pl.BlockSpec((SSM_CHUNK, SSM_CONV_CH), row),
                  pl.BlockSpec((SSM_CHUNK, LANES), row),
                  pl.BlockSpec((CONV_K, SSM_CONV_CH), const),
                  pl.BlockSpec((1, SSM_CONV_CH), const),
                  pl.BlockSpec((1, LANES), const),
                  pl.BlockSpec((1, LANES), const),
                  pl.BlockSpec((1, SSM_D_INNER), const),
                  pl.BlockSpec((1, SSM_D_INNER), const),
                  pl.BlockSpec((LANES, SSM_D_INNER), const)],
        out_specs=pl.BlockSpec((SSM_CHUNK, SSM_D_INNER), row),
        scratch_shapes=[pltpu.VMEM((HALO, SSM_CONV_CH), F32), pltpu.VMEM((SSM_STATE, SSM_D_INNER), F32)],
        compiler_params=_cparams(("parallel", "arbitrary")),
        name="ssd",
    )(z, xbc, dt, conv_w, conv_b.reshape(1, -1), jnp.pad(dt_bias, (0, pad)).reshape(1, -1),
      jnp.pad(a_log, (0, pad)).reshape(1, -1), jnp.repeat(d_skip, SSM_HEAD_DIM).reshape(1, -1),
      norm_g.reshape(1, -1), expand)


def _swa_kernel(sink_ref, q_ref, kv_ref, kvp_ref, o_ref):
    L = ATTN_BLOCK
    hd = ATTN_HEAD_DIM
    first = pl.program_id(1) == 0
    kv = jnp.concatenate([kvp_ref[...], kv_ref[...]], axis=0)
    kk = kv[:, 0:LANES]
    vv = kv[:, LANES:2 * LANES].astype(BF)
    lane_k = lax.broadcasted_iota(jnp.int32, (2 * L, LANES), 1)
    qi = lax.broadcasted_iota(jnp.int32, (2 * L, 2 * L), 0) % L
    kj = lax.broadcasted_iota(jnp.int32, (2 * L, 2 * L), 1)
    rel = qi + L - kj
    valid = (rel >= 0) & (rel < L) & (jnp.logical_not(first) | (kj >= L))
    top = lax.broadcasted_iota(jnp.int32, (2 * L, 1), 0) < L
    lane_o = lax.broadcasted_iota(jnp.int32, (L, LANES), 1)
    scale = hd ** -0.5
    for j in range(ATTN_Q_HEADS // 2):
        g = (2 * j) // ATTN_REP
        q2 = q_ref[:, j * LANES:(j + 1) * LANES]
        q2r = pltpu.roll(q2, hd, axis=1)
        kg = jnp.where((lane_k >= g * hd) & (lane_k < (g + 1) * hd), kk, 0.0).astype(BF)
        qa, qb = (q2, q2r) if g == 0 else (q2r, q2)
        qs = jnp.concatenate([qa, qb], axis=0).astype(BF)
        logits = lax.dot_general(qs, kg, (((1,), (1,)), ((), ())), preferred_element_type=F32) * scale
        logits = jnp.where(valid, logits, -jnp.inf)
        sink = jnp.where(top, sink_ref[2 * j], sink_ref[2 * j + 1])
        m = jnp.maximum(jnp.max(logits, axis=-1, keepdims=True), sink)
        p = jnp.exp(logits - m)
        probs = p / (jnp.sum(p, axis=-1, keepdims=True) + jnp.exp(sink - m))
        o = jnp.dot(probs.astype(BF), vv, preferred_element_type=F32)
        oa, ob = o[0:L], o[L:2 * L]
        if g == 0:
            out2 = jnp.where(lane_o < hd, oa, pltpu.roll(ob, hd, axis=1))
        else:
            out2 = jnp.where(lane_o < hd, pltpu.roll(oa, hd, axis=1), ob)
        o_ref[:, j * LANES:(j + 1) * LANES] = out2


def _swa(q, kv, seq, sinks):
    t = q.shape[0]
    nb = seq // ATTN_BLOCK
    return pl.pallas_call(
        _swa_kernel,
        out_shape=jax.ShapeDtypeStruct((t, ATTN_Q_W), F32),
        grid=(t // seq, nb),
        in_specs=[pl.BlockSpec(memory_space=pltpu.SMEM),
                  pl.BlockSpec((ATTN_BLOCK, ATTN_Q_W), lambda b, n: (b * nb + n, 0)),
                  pl.BlockSpec((ATTN_BLOCK, ATTN_KV_W), lambda b, n: (b * nb + n, 0)),
                  pl.BlockSpec((ATTN_BLOCK, ATTN_KV_W), lambda b, n: (b * nb + jnp.maximum(n - 1, 0), 0))],
        out_specs=pl.BlockSpec((ATTN_BLOCK, ATTN_Q_W), lambda b, n: (b * nb + n, 0)),
        compiler_params=_cparams(("parallel", "parallel")),
        name="swa",
    )(sinks, q, kv, kv)


def _outproj_kernel(ya_ref, yb_ref, yc_ref, h_ref, wa_ref, wb_ref, wc_ref, g_ref, b_ref, o_ref):
    mix = (jnp.dot(ya_ref[...].astype(BF), wa_ref[...], preferred_element_type=F32)
           + jnp.dot(yb_ref[...].astype(BF), wb_ref[...], preferred_element_type=F32)
           + jnp.dot(yc_ref[...].astype(BF), wc_ref[...], preferred_element_type=F32))
    o_ref[...] = _ln_math(DN_ALPHA * h_ref[...] + mix, g_ref[...], b_ref[...])


def _out_proj(ya, yb, yc, h, w_out, g, b):
    t = h.shape[0]
    w = w_out.astype(BF)
    row = lambda i: (i, 0)
    const = lambda i: (0, 0)
    c1, c2 = W_LRU, W_LRU + SSM_D_INNER
    return pl.pallas_call(
        _outproj_kernel,
        out_shape=jax.ShapeDtypeStruct((t, D_MODEL), F32),
        grid=(t // ROW_TILE,),
        in_specs=[pl.BlockSpec((ROW_TILE, W_LRU), row), pl.BlockSpec((ROW_TILE, SSM_D_INNER), row),
                  pl.BlockSpec((ROW_TILE, ATTN_Q_W), row), pl.BlockSpec((ROW_TILE, D_MODEL), row),
                  pl.BlockSpec((W_LRU, D_MODEL), const), pl.BlockSpec((SSM_D_INNER, D_MODEL), const),
                  pl.BlockSpec((ATTN_Q_W, D_MODEL), const),
                  pl.BlockSpec((1, D_MODEL), const), pl.BlockSpec((1, D_MODEL), const)],
        out_specs=pl.BlockSpec((ROW_TILE, D_MODEL), row),
        compiler_params=_cparams(("parallel",)),
        name="out_proj",
    )(ya, yb, yc, h, w[0:c1], w[c1:c2], w[c2:MIX_WIDTH], g.reshape(1, -1), b.reshape(1, -1))


def _gather_token(idx_ref, tbl_ref, buf_ref, t):
    rs = ROW_WORDS_SUBLANES
    for p in range(PEER_PAIRS):
        if p % SUBLANES == 0:
            window = idx_ref.at[pl.ds(t * PEER_PAIRS + p, SUBLANES)]
        r = pl.multiple_of(window[p % SUBLANES], rs)
        buf_ref[pl.ds(rs * p, rs), :] = tbl_ref[pl.ds(r, rs), :]


def _pair_rows_bf16(buf_ref, s):
    return pltpu.bitcast(buf_ref[pl.ds(s, PEER_PAIRS, stride=ROW_WORDS_SUBLANES), :], BF)


def _split_hi_lo(x):
    hi = x.astype(BF)
    lo = (x - hi.astype(F32)).astype(BF)
    return hi, lo


def _gather_pipeline(idx_ref, tbl_ref, bufs, consume):
    u = GATHER_GROUP
    halves = (bufs[:u], bufs[u:])
    n_groups = GATHER_TOKENS // u
    for k in range(u):
        _gather_token(idx_ref, tbl_ref, halves[0][k], k)
    for g in range(n_groups):
        cur, nxt = halves[g % 2], halves[(g + 1) % 2]
        for k in range(u):
            if g + 1 < n_groups:
                _gather_token(idx_ref, tbl_ref, nxt[k], (g + 1) * u + k)
            consume(cur[k], g * u + k)


def _peer_in_kernel(idx_ref, x_ref, gate_ref, tbl_ref, o_ref, m_ref, *bufs):
    rs = ROW_WORDS_SUBLANES
    two_p = 2 * PEER_PAIRS
    row = lax.broadcasted_iota(jnp.int32, (2 * SUBLANES, two_p), 0) % SUBLANES
    even = lax.broadcasted_iota(jnp.int32, (2 * SUBLANES, two_p), 1) % 2 == 0
    masks = [((row == s) & even) | ((row == rs + s) & jnp.logical_not(even)) for s in range(rs)]

    def consume(buf_ref, t):
        x_hi, x_lo = _split_hi_lo(x_ref[pl.ds(t * TOKEN_SUBLANES, TOKEN_SUBLANES), :])
        x16 = jnp.concatenate([x_hi, x_lo], axis=0)
        acc = jnp.zeros((2 * SUBLANES, two_p), F32)
        for s in range(rs):
            r = lax.dot_general(x16, _pair_rows_bf16(buf_ref, s), (((1,), (1,)), ((), ())),
                                preferred_element_type=F32)
            acc = jnp.where(masks[s], r, acc)
        m_ref[pl.ds(t, 1), :] = jnp.sum(acc, axis=0, keepdims=True)

    _gather_pipeline(idx_ref, tbl_ref, bufs, consume)
    fold = (lax.broadcasted_iota(jnp.int32, (two_p, PEER_PAIRS), 0) // 2
            == lax.broadcasted_iota(jnp.int32, (two_p, PEER_PAIRS), 1)).astype(BF)
    m_hi, m_lo = _split_hi_lo(m_ref[...])
    a = jnp.dot(m_hi, fold, preferred_element_type=F32) + jnp.dot(m_lo, fold, preferred_element_type=F32)
    o_ref[...] = gate_ref[...] * (0.5 * a * (1.0 + lax.erf(a * (2.0 ** -0.5))))


def _gather_scratch():
    return [pltpu.VMEM((ROW_WORDS_SUBLANES * PEER_PAIRS, LANES), jnp.uint32)] * (2 * GATHER_GROUP)


def _peer_in(idx_flat, x8, gate, tbl):
    t = gate.shape[0]
    tt = GATHER_TOKENS
    return pl.pallas_call(
        _peer_in_kernel,
        out_shape=jax.ShapeDtypeStruct((t, PEER_PAIRS), F32),
        grid=(t // tt,),
        in_specs=[
            pl.BlockSpec((tt * PEER_PAIRS,), lambda i: (i,), memory_space=pltpu.SMEM),
            pl.BlockSpec((tt * TOKEN_SUBLANES, LANES), lambda i: (i, 0)),
            pl.BlockSpec((tt, PEER_PAIRS), lambda i: (i, 0)),
            pl.BlockSpec(memory_space=pltpu.VMEM),
        ],
        out_specs=pl.BlockSpec((tt, PEER_PAIRS), lambda i: (i, 0)),
        scratch_shapes=[pltpu.VMEM((tt, 2 * PEER_PAIRS), F32)] + _gather_scratch(),
        compiler_params=pltpu.CompilerParams(vmem_limit_bytes=PEER_VMEM_LIMIT),
        name="peer_in",
    )(idx_flat, x8, gate, tbl)


def _peer_out_kernel(idx_ref, w_ref, tbl_ref, o_ref, spread_ref, *bufs):
    rs = ROW_WORDS_SUBLANES
    two_p = 2 * PEER_PAIRS
    lane = lax.broadcasted_iota(jnp.int32, (PEER_PAIRS, two_p), 1)
    pair = lax.broadcasted_iota(jnp.int32, (PEER_PAIRS, two_p), 0)
    to_even = (lane == 2 * pair).astype(BF)
    to_odd = (lane == 2 * pair + 1).astype(BF)
    w_hi, w_lo = _split_hi_lo(w_ref[...])
    for k, (wv, place) in enumerate(((w_hi, to_even), (w_hi, to_odd), (w_lo, to_even), (w_lo, to_odd))):
        spread_ref[k] = jnp.dot(wv, place, preferred_element_type=F32)
    row4 = lax.broadcasted_iota(jnp.int32, (4 * rs, two_p), 0) % rs

    def consume(buf_ref, t):
        w16 = jnp.concatenate([jnp.broadcast_to(spread_ref[k, pl.ds(t, 1), :], (rs, two_p)) for k in range(4)],
                              axis=0)
        acc = jnp.zeros((4 * rs, LANES), F32)
        for s in range(rs):
            lhs = jnp.where(row4 == s, w16, 0.0).astype(BF)
            acc = acc + jnp.dot(lhs, _pair_rows_bf16(buf_ref, s), preferred_element_type=F32)
        o_ref[pl.ds(t * TOKEN_SUBLANES, TOKEN_SUBLANES), :] = acc[0:TOKEN_SUBLANES] + acc[TOKEN_SUBLANES:]

    _gather_pipeline(idx_ref, tbl_ref, bufs, consume)


def _peer_out(idx_flat, w, tbl):
    t = w.shape[0]
    tt = GATHER_TOKENS
    return pl.pallas_call(
        _peer_out_kernel,
        out_shape=jax.ShapeDtypeStruct((t * TOKEN_SUBLANES, LANES), F32),
        grid=(t // tt,),
        in_specs=[
            pl.BlockSpec((tt * PEER_PAIRS,), lambda i: (i,), memory_space=pltpu.SMEM),
            pl.BlockSpec((tt, PEER_PAIRS), lambda i: (i, 0)),
            pl.BlockSpec(memory_space=pltpu.VMEM),
        ],
        out_specs=pl.BlockSpec((tt * TOKEN_SUBLANES, LANES), lambda i: (i, 0)),
        scratch_shapes=[pltpu.VMEM((4, tt, 2 * PEER_PAIRS), F32)] + _gather_scratch(),
        compiler_params=pltpu.CompilerParams(vmem_limit_bytes=PEER_VMEM_LIMIT),
        name="peer_out",
    )(idx_flat, w, tbl)


def _extract_topk(s, pos, k, payload=None):
    big = jnp.int32(2 ** 30)
    vals, poss, pays = [], [], []
    for _ in range(k):
        m = jnp.max(s, axis=0, keepdims=True)
        j = jnp.min(jnp.where(s == m, pos, big), axis=0, keepdims=True)
        sel = pos == j
        vals.append(m)
        poss.append(j)
        if payload is not None:
            pays.append(jnp.sum(jnp.where(sel, payload, 0), axis=0, keepdims=True))
        s = jnp.where(sel, -jnp.inf, s)
    out = [jnp.concatenate(vals, axis=0), jnp.concatenate(poss, axis=0)]
    if payload is not None:
        out.append(jnp.concatenate(pays, axis=0))
    return out


def _peer_route_kernel(h_ref, wqt_ref, keys_ref, idx_ref, gate_ref):
    half_dim = PEER_DKEY // 2
    h = h_ref[...]
    n = h.shape[0]
    qt = lax.dot_general(wqt_ref[...], h, (((1,), (1,)), ((), ())), preferred_element_type=jnp.float32)
    key_iota = lax.broadcasted_iota(jnp.int32, (PEER_NKEYS, n), 0)
    iota16 = lax.broadcasted_iota(jnp.int32, (PEER_TOPK, n), 0)
    iota8 = lax.broadcasted_iota(jnp.int32, (SUBLANES, n), 0)
    idx_rows, gate_rows = [], []
    for hd in range(PEER_HEADS):
        tv, ti = [], []
        for half in range(2):
            r0 = hd * PEER_DKEY + half * half_dim
            s = jnp.dot(keys_ref[half], qt[r0:r0 + half_dim, :], preferred_element_type=jnp.float32)
            v, i = _extract_topk(s, key_iota, PEER_TOPK)
            tv.append(v)
            ti.append(i)
        cs = [tv[0][0:1, :] + tv[1]]
        cp = [iota16]
        ce = [ti[0][0:1, :] * PEER_NKEYS + ti[1]]
        for a in range(1, SUBLANES):
            cs.append(tv[0][a:a + 1, :] + tv[1][0:SUBLANES, :])
            cp.append(iota8 + a * PEER_TOPK)
            ce.append(ti[0][a:a + 1, :] * PEER_NKEYS + ti[1][0:SUBLANES, :])
        cs.append(tv[0][SUBLANES:, :] + tv[1][0:1, :])
        cp.append((iota8 + SUBLANES) * PEER_TOPK)
        ce.append(ti[0][SUBLANES:, :] * PEER_NKEYS + ti[1][0:1, :])
        best_s, _, best_e = _extract_topk(jnp.concatenate(cs, axis=0), jnp.concatenate(cp, axis=0),
                                          PEER_TOPK, payload=jnp.concatenate(ce, axis=0))
        ex = jnp.exp(best_s - best_s[0:1, :])
        gate_rows.append(ex / jnp.sum(ex, axis=0, keepdims=True))
        idx_rows.append(best_e)
    idx_ref[...] = (jnp.concatenate(idx_rows, axis=0) * ROW_WORDS_SUBLANES).T
    gate_ref[...] = jnp.concatenate(gate_rows, axis=0).T


def _peer_route(h, wqt, keys):
    t = h.shape[0]
    rt = ROUTE_TOKENS
    return pl.pallas_call(
        _peer_route_kernel,
        out_shape=(jax.ShapeDtypeStruct((t, PEER_PAIRS), jnp.int32),
                   jax.ShapeDtypeStruct((t, PEER_PAIRS), jnp.float32)),
        grid=(t // rt,),
        in_specs=[
            pl.BlockSpec((rt, D_MODEL), lambda i: (i, 0)),
            pl.BlockSpec((D_MODEL, D_MODEL), lambda i: (0, 0)),
            pl.BlockSpec((2, PEER_NKEYS, PEER_DKEY // 2), lambda i: (0, 0, 0)),
        ],
        out_specs=(pl.BlockSpec((rt, PEER_PAIRS), lambda i: (i, 0)),
                   pl.BlockSpec((rt, PEER_PAIRS), lambda i: (i, 0))),
        compiler_params=_cparams(("parallel",)),
        name="peer_route",
    )(h, wqt, keys)


def _pack_table(u):
    ub = u.astype(jnp.bfloat16)
    half = D_MODEL // 2
    lo = lax.bitcast_convert_type(ub[:, :half], jnp.uint16).astype(jnp.uint32)
    hi = lax.bitcast_convert_type(ub[:, half:], jnp.uint16).astype(jnp.uint32)
    return (lo | (hi << 16)).reshape(u.shape[0] * ROW_WORDS_SUBLANES, LANES)


def _peer_ffn(h, wq, keys, u, v):
    t = h.shape[0]
    rows, gate = _peer_route(h, wq.T, keys)
    rows = rows.reshape(t * PEER_PAIRS)
    w = _peer_in(rows, h.reshape(t * TOKEN_SUBLANES, LANES), gate, _pack_table(u))
    return _peer_out(rows, w, _pack_table(v)).reshape(t, D_MODEL)


def kernel(x, emb_ln_g, emb_ln_b, w_in, rg_conv_w, rg_conv_b, rg_wa, rg_ba, rg_wx, rg_bx, rg_lambda, ssm_conv_w, ssm_conv_b, ssm_dt_bias, ssm_a_log, ssm_d, ssm_norm_g, attn_sinks, w_out, ln1_g, ln1_b, peer_wq, peer_keys, peer_u, peer_v, ln2_g, ln2_b):
    bsz, seq, d = x.shape
    assert d == D_MODEL and seq % ROW_TILE == 0 and (bsz * seq) % ROUTE_TOKENS == 0
    h = _layer_norm(x.reshape(bsz * seq, d), emb_ln_g, emb_ln_b)
    for l in range(DEPTH):
        rg, z, xbc, dt, q, kv = _in_proj(h, _split_w_in(w_in[l]))
        y_a = _rg_lru(rg, seq, rg_conv_w[l], rg_conv_b[l], _block_diag(rg_wa[l]), rg_ba[l],
                      _block_diag(rg_wx[l]), rg_bx[l], rg_lambda[l])
        y_b = _ssd(z, xbc, dt, seq, ssm_conv_w[l], ssm_conv_b[l], ssm_dt_bias[l], ssm_a_log[l], ssm_d[l],
                   ssm_norm_g[l])
        y_c = _swa(q, kv, seq, attn_sinks[l])
        h = _out_proj(y_a, y_b, y_c, h, w_out[l], ln1_g[l], ln1_b[l])
        ffn = _peer_ffn(h, peer_wq[l], peer_keys[l], peer_u[l], peer_v[l])
        h = _layer_norm(h, ln2_g[l], ln2_b[l], res=ffn)
    return h.reshape(bsz, seq, d)
```

```python
import math

import jax
import jax.numpy as jnp
from jax import lax
from jax.experimental import pallas as pl
from jax.experimental.pallas import tpu as pltpu

D_MODEL = 1024
DEPTH = 2

W_LRU = D_MODEL // 2
LRU_BLOCKS = 8
LRU_C = 8.0
CONV_K = 4

SSM_HEAD_DIM = 64
SSM_D_INNER = D_MODEL
SSM_HEADS = SSM_D_INNER // SSM_HEAD_DIM
SSM_GROUPS = 2
SSM_STATE = 128
SSM_CHUNK = 128
SSM_CONV_CH = SSM_D_INNER + 2 * SSM_GROUPS * SSM_STATE
SSD_GROUP_W = SSM_D_INNER // SSM_GROUPS

ATTN_HEAD_DIM = 64
ATTN_Q_HEADS = (D_MODEL // 2) // ATTN_HEAD_DIM
ATTN_KV_HEADS = 2
ATTN_REP = ATTN_Q_HEADS // ATTN_KV_HEADS
ATTN_BLOCK = 128
ATTN_Q_W = ATTN_Q_HEADS * ATTN_HEAD_DIM
ATTN_KV_W = 2 * ATTN_KV_HEADS * ATTN_HEAD_DIM

MIX_WIDTH = W_LRU + SSM_D_INNER + ATTN_Q_W

PEER_HEADS = 8
PEER_NKEYS = 128
PEER_DKEY = 128
PEER_TOPK = 16
PEER_PAIRS = PEER_HEADS * PEER_TOPK

DN_ALPHA = (2 * DEPTH) ** 0.25
LN_EPS = 1e-5

SUBLANES = 8
LANES = 128
HALO = SUBLANES
ROW_WORDS_SUBLANES = D_MODEL // 2 // LANES
TOKEN_SUBLANES = D_MODEL // LANES

ROW_TILE = 512
GATHER_TOKENS = 64
GATHER_GROUP = 2
ROUTE_TOKENS = 256
VMEM_LIMIT = 48 * 1024 * 1024
PEER_VMEM_LIMIT = 56 * 1024 * 1024

BF = jnp.bfloat16
F32 = jnp.float32
HI = lax.Precision.HIGHEST

PROJ_SEGS = (2 * W_LRU, SSM_D_INNER, SSM_CONV_CH, LANES, ATTN_Q_W, ATTN_KV_W)


def _cparams(sem):
    return pltpu.CompilerParams(dimension_semantics=sem, vmem_limit_bytes=VMEM_LIMIT)


def _softplus(x):
    return jnp.maximum(x, 0.0) + jnp.log1p(jnp.exp(-jnp.abs(x)))


def _sigmoid(x):
    return 1.0 / (1.0 + jnp.exp(-x))


def _silu(x):
    return x * _sigmoid(x)


def _gelu_tanh(x):
    return 0.5 * x * (1.0 + jnp.tanh(math.sqrt(2.0 / math.pi) * (x + 0.044715 * (x * x * x))))


def _ln_math(x, g, b):
    mu = jnp.mean(x, axis=-1, keepdims=True)
    xc = x - mu
    var = jnp.mean(xc * xc, axis=-1, keepdims=True)
    return xc * lax.rsqrt(var + LN_EPS) * g + b


def _ln_kernel(a_ref, g_ref, b_ref, o_ref):
    o_ref[...] = _ln_math(a_ref[...], g_ref[...], b_ref[...])


def _res_ln_kernel(a_ref, r_ref, g_ref, b_ref, o_ref):
    o_ref[...] = _ln_math(DN_ALPHA * a_ref[...] + r_ref[...], g_ref[...], b_ref[...])


def _layer_norm(a, g, b, res=None):
    t, d = a.shape
    row = pl.BlockSpec((ROW_TILE, d), lambda i: (i, 0))
    vec = pl.BlockSpec((1, d), lambda i: (0, 0))
    if res is None:
        kern, args, specs = _ln_kernel, (a,), [row]
    else:
        kern, args, specs = _res_ln_kernel, (a, res), [row, row]
    return pl.pallas_call(
        kern, out_shape=jax.ShapeDtypeStruct((t, d), F32), grid=(t // ROW_TILE,),
        in_specs=specs + [vec, vec], out_specs=row, compiler_params=_cparams(("parallel",)),
        name="layer_norm",
    )(*args, g.reshape(1, d), b.reshape(1, d))


def _inproj_kernel(h_ref, *refs):
    n = len(PROJ_SEGS)
    hb = h_ref[...].astype(BF)
    for w_ref, o_ref in zip(refs[:n], refs[n:]):
        o_ref[...] = jnp.dot(hb, w_ref[...], preferred_element_type=F32)


def _in_proj(h, ws):
    t = h.shape[0]
    return pl.pallas_call(
        _inproj_kernel,
        out_shape=tuple(jax.ShapeDtypeStruct((t, w), F32) for w in PROJ_SEGS),
        grid=(t // ROW_TILE,),
        in_specs=[pl.BlockSpec((ROW_TILE, D_MODEL), lambda i: (i, 0))]
        + [pl.BlockSpec((D_MODEL, w), lambda i: (0, 0)) for w in PROJ_SEGS],
        out_specs=tuple(pl.BlockSpec((ROW_TILE, w), lambda i: (i, 0)) for w in PROJ_SEGS),
        compiler_params=_cparams(("parallel",)),
        name="in_proj",
    )(h, *ws)


def _split_w_in(w_in):
    w = w_in.astype(BF)
    c = [0]
    for width in (2 * W_LRU, SSM_D_INNER, SSM_CONV_CH, SSM_HEADS, ATTN_Q_W, ATTN_KV_W):
        c.append(c[-1] + width)
    segs = [w[:, c[i]:c[i + 1]] for i in range(6)]
    segs[3] = jnp.pad(segs[3], ((0, 0), (0, LANES - SSM_HEADS)))
    return tuple(segs)


def _shift_rows(x, s, fill):
    n = x.shape[0]
    if s % SUBLANES == 0:
        return jnp.concatenate([jnp.full((s,) + x.shape[1:], fill, x.dtype), x[:n - s]], axis=0)
    rolled = pltpu.roll(x, s, axis=0)
    row = lax.broadcasted_iota(jnp.int32, x.shape, 0)
    return jnp.where(row < s, fill, rolled)


def _causal_conv(x, halo, w_ref, b_ref):
    n = x.shape[0]
    xp = jnp.concatenate([halo, x], axis=0)
    out = b_ref[...] + w_ref[CONV_K - 1:CONV_K, :] * x
    for s in range(1, CONV_K):
        out = out + w_ref[CONV_K - 1 - s:CONV_K - s, :] * pltpu.roll(xp, s, axis=0)[HALO:HALO + n]
    return out


def _rglru_kernel(rg_ref, cw_ref, cb_ref, wa_ref, ba_ref, wx_ref, bx_ref, lam_ref, o_ref, halo_ref, carry_ref):
    n = rg_ref.shape[0]

    @pl.when(pl.program_id(1) == 0)
    def _():
        halo_ref[...] = jnp.zeros_like(halo_ref)
        carry_ref[...] = jnp.zeros_like(carry_ref)

    x = rg_ref[:, 0:W_LRU]
    xc = _causal_conv(x, halo_ref[...], cw_ref, cb_ref)
    halo_ref[...] = x[n - HALO:n]
    xb = xc.astype(BF)
    r = _sigmoid(jnp.dot(xb, wa_ref[...], preferred_element_type=F32) + ba_ref[...])
    i = _sigmoid(jnp.dot(xb, wx_ref[...], preferred_element_type=F32) + bx_ref[...])
    log_a = (-LRU_C * r) * _softplus(-lam_ref[...])
    a = jnp.exp(log_a)
    u = jnp.sqrt(-jnp.tanh(log_a) * (a * a + 1.0)) * (i * xc)
    k = 1
    while k < n:
        u = a * _shift_rows(u, k, 0.0) + u
        a = a * _shift_rows(a, k, 1.0)
        k *= 2
    h = u + a * carry_ref[0:1, :]
    carry_ref[...] = jnp.broadcast_to(h[n - 1:n, :], carry_ref.shape)
    o_ref[...] = _gelu_tanh(rg_ref[:, W_LRU:2 * W_LRU]) * h


def _rg_lru(rg, seq, conv_w, conv_b, wa, ba, wx, bx, lam):
    t = rg.shape[0]
    nb = seq // ROW_TILE
    row = lambda b, j: (b * nb + j, 0)
    const = lambda b, j: (0, 0)
    vec = pl.BlockSpec((1, W_LRU), const)
    return pl.pallas_call(
        _rglru_kernel,
        out_shape=jax.ShapeDtypeStruct((t, W_LRU), F32),
        grid=(t // seq, nb),
        in_specs=[pl.BlockSpec((ROW_TILE, 2 * W_LRU), row),
                  pl.BlockSpec((CONV_K, W_LRU), const), vec,
                  pl.BlockSpec((W_LRU, W_LRU), const), vec,
                  pl.BlockSpec((W_LRU, W_LRU), const), vec, vec],
        out_specs=pl.BlockSpec((ROW_TILE, W_LRU), row),
        scratch_shapes=[pltpu.VMEM((HALO, W_LRU), F32), pltpu.VMEM((SUBLANES, W_LRU), F32)],
        compiler_params=_cparams(("parallel", "arbitrary")),
        name="rg_lru",
    )(rg, conv_w, conv_b.reshape(1, -1), wa, ba.reshape(1, -1), wx, bx.reshape(1, -1), lam.reshape(1, -1))


def _block_diag(w):
    nb, c, _ = w.shape
    eye = jnp.eye(nb, dtype=w.dtype)
    return (eye[:, None, :, None] * w[:, :, None, :]).reshape(nb * c, nb * c).astype(BF)


def _ssd_kernel(z_ref, xbc_ref, dt_ref, cw_ref, cb_ref, dtb_ref, alog_ref, dskip_ref, ng_ref, expand_ref,
                o_ref, halo_ref, state_ref):
    L = SSM_CHUNK

    @pl.when(pl.program_id(1) == 0)
    def _():
        halo_ref[...] = jnp.zeros_like(halo_ref)
        state_ref[...] = jnp.zeros_like(state_ref)

    xbc = xbc_ref[...]
    conv = _silu(_causal_conv(xbc, halo_ref[...], cw_ref, cb_ref))
    halo_ref[...] = xbc[L - HALO:L]
    xs = conv[:, 0:SSM_D_INNER]
    bm = conv[:, SSM_D_INNER:SSM_D_INNER + SSM_GROUPS * SSM_STATE]
    cm = conv[:, SSM_D_INNER + SSM_GROUPS * SSM_STATE:]

    dt = _softplus(dt_ref[...] + dtb_ref[...])
    da = dt * (-jnp.exp(alog_ref[...]))
    row = lax.broadcasted_iota(jnp.int32, (L, L), 0)
    col = lax.broadcasted_iota(jnp.int32, (L, L), 1)
    causal = col <= row
    a_cs = jnp.dot(causal.astype(F32), da, precision=HI, preferred_element_type=F32)
    a_cs_t = a_cs.T
    expand = expand_ref[...]
    dt_full = jnp.dot(dt, expand, precision=HI, preferred_element_type=F32)
    acs_full = jnp.dot(a_cs, expand, precision=HI, preferred_element_type=F32)
    alast_full = acs_full[L - 1:L, :]
    xdt = xs * dt_full
    xdt_b = xdt.astype(BF)
    xst_b = (xdt * jnp.exp(alast_full - acs_full)).astype(BF)
    left = lax.broadcasted_iota(jnp.int32, (L, LANES), 1) < SSM_HEAD_DIM

    y_parts = []
    for g in range(SSM_GROUPS):
        cg = cm[:, g * SSM_STATE:(g + 1) * SSM_STATE].astype(BF)
        bg = bm[:, g * SSM_STATE:(g + 1) * SSM_STATE].astype(BF)
        cb = lax.dot_general(cg, bg, (((1,), (1,)), ((), ())), preferred_element_type=F32)
        st = state_ref[:, g * SSD_GROUP_W:(g + 1) * SSD_GROUP_W]
        y_off = jnp.dot(cg, st.astype(BF), preferred_element_type=F32)
        for j in range(SSD_GROUP_W // LANES):
            h0 = g * (SSM_HEADS // SSM_GROUPS) + 2 * j
            ms = []
            for h in (h0, h0 + 1):
                seg = a_cs[:, h:h + 1] - a_cs_t[h:h + 1, :]
                ms.append((cb * jnp.exp(jnp.where(causal, seg, -jnp.inf))).astype(BF))
            c0 = h0 * SSM_HEAD_DIM
            x2 = xdt_b[:, c0:c0 + LANES]
            zero = jnp.zeros_like(x2)
            xblk = jnp.concatenate([jnp.where(left, x2, zero), jnp.where(left, zero, x2)], axis=0)
            y_parts.append(jnp.dot(jnp.concatenate(ms, axis=1), xblk, preferred_element_type=F32)
                           + y_off[:, j * LANES:(j + 1) * LANES] * jnp.exp(acs_full[:, c0:c0 + LANES]))
        new = lax.dot_general(bg, xst_b[:, g * SSD_GROUP_W:(g + 1) * SSD_GROUP_W], (((0,), (0,)), ((), ())),
                              preferred_element_type=F32)
        state_ref[:, g * SSD_GROUP_W:(g + 1) * SSD_GROUP_W] = (
            st * jnp.exp(alast_full[:, g * SSD_GROUP_W:(g + 1) * SSD_GROUP_W]) + new)
    y = jnp.concatenate(y_parts, axis=1) + dskip_ref[...] * xs
    y = y * _silu(z_ref[...])
    outs = []
    for g in range(SSM_GROUPS):
        yg = y[:, g * SSD_GROUP_W:(g + 1) * SSD_GROUP_W]
        outs.append(yg * lax.rsqrt(jnp.mean(yg * yg, axis=-1, keepdims=True) + LN_EPS))
    o_ref[...] = jnp.concatenate(outs, axis=1) * ng_ref[...]


def _ssd(z, xbc, dt, seq, conv_w, conv_b, dt_bias, a_log, d_skip, norm_g):
    t = z.shape[0]
    nc = seq // SSM_CHUNK
    row = lambda b, c: (b * nc + c, 0)
    const = lambda b, c: (0, 0)
    pad = LANES - SSM_HEADS
    expand = (jnp.arange(LANES)[:, None] == (jnp.arange(SSM_D_INNER)[None, :] // SSM_HEAD_DIM)).astype(F32)
    return pl.pallas_call(
        _ssd_kernel,
        out_shape=jax.ShapeDtypeStruct((t, SSM_D_INNER), F32),
        grid=(t // seq, nc),
        in_specs=[pl.BlockSpec((SSM_CHUNK, SSM_D_INNER), row),
                  pl.BlockSpec((SSM_CHUNK, SSM_CONV_CH), row),
                  pl.BlockSpec((SSM_CHUNK, LANES), row),
                  pl.BlockSpec((CONV_K, SSM_CONV_CH), const),
                  pl.BlockSpec((1, SSM_CONV_CH), const),
                  pl.BlockSpec((1, LANES), const),
                  pl.BlockSpec((1, LANES), const),
                  pl.BlockSpec((1, SSM_D_INNER), const),
                  pl.BlockSpec((1, SSM_D_INNER), const),
                  pl.BlockSpec((LANES, SSM_D_INNER), const)],
        out_specs=pl.BlockSpec((SSM_CHUNK, SSM_D_INNER), row),
        scratch_shapes=[pltpu.VMEM((HALO, SSM_CONV_CH), F32), pltpu.VMEM((SSM_STATE, SSM_D_INNER), F32)],
        compiler_params=_cparams(("parallel", "arbitrary")),
        name="ssd",
    )(z, xbc, dt, conv_w, conv_b.reshape(1, -1), jnp.pad(dt_bias, (0, pad)).reshape(1, -1),
      jnp.pad(a_log, (0, pad)).reshape(1, -1), jnp.repeat(d_skip, SSM_HEAD_DIM).reshape(1, -1),
      norm_g.reshape(1, -1), expand)


def _swa_kernel(sink_ref, q_ref, kv_ref, kvp_ref, o_ref):
    L = ATTN_BLOCK
    hd = ATTN_HEAD_DIM
    first = pl.program_id(1) == 0
    kv = jnp.concatenate([kvp_ref[...], kv_ref[...]], axis=0)
    kk = kv[:, 0:LANES]
    vv = kv[:, LANES:2 * LANES].astype(BF)
    lane_k = lax.broadcasted_iota(jnp.int32, (2 * L, LANES), 1)
    qi = lax.broadcasted_iota(jnp.int32, (2 * L, 2 * L), 0) % L
    kj = lax.broadcasted_iota(jnp.int32, (2 * L, 2 * L), 1)
    rel = qi + L - kj
    valid = (rel >= 0) & (rel < L) & (jnp.logical_not(first) | (kj >= L))
    top = lax.broadcasted_iota(jnp.int32, (2 * L, 1), 0) < L
    lane_o = lax.broadcasted_iota(jnp.int32, (L, LANES), 1)
    scale = hd ** -0.5
    for j in range(ATTN_Q_HEADS // 2):
        g = (2 * j) // ATTN_REP
        q2 = q_ref[:, j * LANES:(j + 1) * LANES]
        q2r = pltpu.roll(q2, hd, axis=1)
        kg = jnp.where((lane_k >= g * hd) & (lane_k < (g + 1) * hd), kk, 0.0).astype(BF)
        qa, qb = (q2, q2r) if g == 0 else (q2r, q2)
        qs = jnp.concatenate([qa, qb], axis=0).astype(BF)
        logits = lax.dot_general(qs, kg, (((1,), (1,)), ((), ())), preferred_element_type=F32) * scale
        logits = jnp.where(valid, logits, -jnp.inf)
        sink = jnp.where(top, sink_ref[2 * j], sink_ref[2 * j + 1])
        m = jnp.maximum(jnp.max(logits, axis=-1, keepdims=True), sink)
        p = jnp.exp(logits - m)
        probs = p / (jnp.sum(p, axis=-1, keepdims=True) + jnp.exp(sink - m))
        o = jnp.dot(probs.astype(BF), vv, preferred_element_type=F32)
        oa, ob = o[0:L], o[L:2 * L]
        if g == 0:
            out2 = jnp.where(lane_o < hd, oa, pltpu.roll(ob, hd, axis=1))
        else:
            out2 = jnp.where(lane_o < hd, pltpu.roll(oa, hd, axis=1), ob)
        o_ref[:, j * LANES:(j + 1) * LANES] = out2


def _swa(q, kv, seq, sinks):
    t = q.shape[0]
    nb = seq // ATTN_BLOCK
    return pl.pallas_call(
        _swa_kernel,
        out_shape=jax.ShapeDtypeStruct((t, ATTN_Q_W), F32),
        grid=(t // seq, nb),
        in_specs=[pl.BlockSpec(memory_space=pltpu.SMEM),
                  pl.BlockSpec((ATTN_BLOCK, ATTN_Q_W), lambda b, n: (b * nb + n, 0)),
                  pl.BlockSpec((ATTN_BLOCK, ATTN_KV_W), lambda b, n: (b * nb + n, 0)),
                  pl.BlockSpec((ATTN_BLOCK, ATTN_KV_W), lambda b, n: (b * nb + jnp.maximum(n - 1, 0), 0))],
        out_specs=pl.BlockSpec((ATTN_BLOCK, ATTN_Q_W), lambda b, n: (b * nb + n, 0)),
        compiler_params=_cparams(("parallel", "parallel")),
        name="swa",
    )(sinks, q, kv, kv)


def _outproj_kernel(ya_ref, yb_ref, yc_ref, h_ref, wa_ref, wb_ref, wc_ref, g_ref, b_ref, o_ref):
    mix = (jnp.dot(ya_ref[...].astype(BF), wa_ref[...], preferred_element_type=F32)
           + jnp.dot(yb_ref[...].astype(BF), wb_ref[...], preferred_element_type=F32)
           + jnp.dot(yc_ref[...].astype(BF), wc_ref[...], preferred_element_type=F32))
    o_ref[...] = _ln_math(DN_ALPHA * h_ref[...] + mix, g_ref[...], b_ref[...])


def _out_proj(ya, yb, yc, h, w_out, g, b):
    t = h.shape[0]
    w = w_out.astype(BF)
    row = lambda i: (i, 0)
    const = lambda i: (0, 0)
    c1, c2 = W_LRU, W_LRU + SSM_D_INNER
    return pl.pallas_call(
        _outproj_kernel,
        out_shape=jax.ShapeDtypeStruct((t, D_MODEL), F32),
        grid=(t // ROW_TILE,),
        in_specs=[pl.BlockSpec((ROW_TILE, W_LRU), row), pl.BlockSpec((ROW_TILE, SSM_D_INNER), row),
                  pl.BlockSpec((ROW_TILE, ATTN_Q_W), row), pl.BlockSpec((ROW_TILE, D_MODEL), row),
                  pl.BlockSpec((W_LRU, D_MODEL), const), pl.BlockSpec((SSM_D_INNER, D_MODEL), const),
                  pl.BlockSpec((ATTN_Q_W, D_MODEL), const),
                  pl.BlockSpec((1, D_MODEL), const), pl.BlockSpec((1, D_MODEL), const)],
        out_specs=pl.BlockSpec((ROW_TILE, D_MODEL), row),
        compiler_params=_cparams(("parallel",)),
        name="out_proj",
    )(ya, yb, yc, h, w[0:c1], w[c1:c2], w[c2:MIX_WIDTH], g.reshape(1, -1), b.reshape(1, -1))


def _gather_token(idx_ref, tbl_ref, buf_ref, t):
    rs = ROW_WORDS_SUBLANES
    for p in range(PEER_PAIRS):
        if p % SUBLANES == 0:
            window = idx_ref.at[pl.ds(t * PEER_PAIRS + p, SUBLANES)]
        r = pl.multiple_of(window[p % SUBLANES], rs)
        buf_ref[pl.ds(rs * p, rs), :] = tbl_ref[pl.ds(r, rs), :]


def _pair_rows_bf16(buf_ref, s):
    return pltpu.bitcast(buf_ref[pl.ds(s, PEER_PAIRS, stride=ROW_WORDS_SUBLANES), :], BF)


def _split_hi_lo(x):
    hi = x.astype(BF)
    lo = (x - hi.astype(F32)).astype(BF)
    return hi, lo


def _gather_pipeline(idx_ref, tbl_ref, bufs, consume):
    u = GATHER_GROUP
    halves = (bufs[:u], bufs[u:])
    n_groups = GATHER_TOKENS // u
    for k in range(u):
        _gather_token(idx_ref, tbl_ref, halves[0][k], k)
    for g in range(n_groups):
        cur, nxt = halves[g % 2], halves[(g + 1) % 2]
        for k in range(u):
            if g + 1 < n_groups:
                _gather_token(idx_ref, tbl_ref, nxt[k], (g + 1) * u + k)
            consume(cur[k], g * u + k)


def _peer_in_kernel(idx_ref, x_ref, gate_ref, tbl_ref, o_ref, m_ref, *bufs):
    rs = ROW_WORDS_SUBLANES
    two_p = 2 * PEER_PAIRS
    row = lax.broadcasted_iota(jnp.int32, (2 * SUBLANES, two_p), 0) % SUBLANES
    even = lax.broadcasted_iota(jnp.int32, (2 * SUBLANES, two_p), 1) % 2 == 0
    masks = [((row == s) & even) | ((row == rs + s) & jnp.logical_not(even)) for s in range(rs)]

    def consume(buf_ref, t):
        x_row = x_ref[pl.ds(t, 1), :]
        x8 = jnp.concatenate([x_row[:, c * LANES:(c + 1) * LANES] for c in range(TOKEN_SUBLANES)], axis=0)
        x_hi, x_lo = _split_hi_lo(x8)
        x16 = jnp.concatenate([x_hi, x_lo], axis=0)
        acc = jnp.zeros((2 * SUBLANES, two_p), F32)
        for s in range(rs):
            r = lax.dot_general(x16, _pair_rows_bf16(buf_ref, s), (((1,), (1,)), ((), ())),
                                preferred_element_type=F32)
            acc = jnp.where(masks[s], r, acc)
        m_ref[pl.ds(t, 1), :] = jnp.sum(acc, axis=0, keepdims=True)

    _gather_pipeline(idx_ref, tbl_ref, bufs, consume)
    fold = (lax.broadcasted_iota(jnp.int32, (two_p, PEER_PAIRS), 0) // 2
            == lax.broadcasted_iota(jnp.int32, (two_p, PEER_PAIRS), 1)).astype(BF)
    m_hi, m_lo = _split_hi_lo(m_ref[...])
    a = jnp.dot(m_hi, fold, preferred_element_type=F32) + jnp.dot(m_lo, fold, preferred_element_type=F32)
    o_ref[...] = gate_ref[...] * (0.5 * a * (1.0 + lax.erf(a * (2.0 ** -0.5))))


def _gather_scratch():
    return [pltpu.VMEM((ROW_WORDS_SUBLANES * PEER_PAIRS, LANES), jnp.uint32)] * (2 * GATHER_GROUP)


def _peer_in(idx_flat, x8, gate, tbl):
    t = gate.shape[0]
    tt = GATHER_TOKENS
    return pl.pallas_call(
        _peer_in_kernel,
        out_shape=jax.ShapeDtypeStruct((t, PEER_PAIRS), F32),
        grid=(t // tt,),
        in_specs=[
            pl.BlockSpec((tt * PEER_PAIRS,), lambda i: (i,), memory_space=pltpu.SMEM),
            pl.BlockSpec((tt, D_MODEL), lambda i: (i, 0)),
            pl.BlockSpec((tt, PEER_PAIRS), lambda i: (i, 0)),
            pl.BlockSpec(memory_space=pltpu.VMEM),
        ],
        out_specs=pl.BlockSpec((tt, PEER_PAIRS), lambda i: (i, 0)),
        scratch_shapes=[pltpu.VMEM((tt, 2 * PEER_PAIRS), F32)] + _gather_scratch(),
        compiler_params=pltpu.CompilerParams(vmem_limit_bytes=PEER_VMEM_LIMIT),
        name="peer_in",
    )(idx_flat, x8, gate, tbl)


def _peer_out_kernel(idx_ref, w_ref, tbl_ref, h_ref, g_ref, b_ref, o_ref, spread_ref, ffn_ref, *bufs):
    rs = ROW_WORDS_SUBLANES
    two_p = 2 * PEER_PAIRS
    lane = lax.broadcasted_iota(jnp.int32, (PEER_PAIRS, two_p), 1)
    pair = lax.broadcasted_iota(jnp.int32, (PEER_PAIRS, two_p), 0)
    to_even = (lane == 2 * pair).astype(BF)
    to_odd = (lane == 2 * pair + 1).astype(BF)
    w_hi, w_lo = _split_hi_lo(w_ref[...])
    for k, (wv, place) in enumerate(((w_hi, to_even), (w_hi, to_odd), (w_lo, to_even), (w_lo, to_odd))):
        spread_ref[k] = jnp.dot(wv, place, preferred_element_type=F32)
    row4 = lax.broadcasted_iota(jnp.int32, (4 * rs, two_p), 0) % rs

    def consume(buf_ref, t):
        w16 = jnp.concatenate([jnp.broadcast_to(spread_ref[k, pl.ds(t, 1), :], (rs, two_p)) for k in range(4)],
                              axis=0)
        acc = jnp.zeros((4 * rs, LANES), F32)
        for s in range(rs):
            lhs = jnp.where(row4 == s, w16, 0.0).astype(BF)
            acc = acc + jnp.dot(lhs, _pair_rows_bf16(buf_ref, s), preferred_element_type=F32)
        out8 = acc[0:TOKEN_SUBLANES] + acc[TOKEN_SUBLANES:]
        for c in range(TOKEN_SUBLANES):
            ffn_ref[pl.ds(t, 1), c * LANES:(c + 1) * LANES] = out8[c:c + 1, :]

    _gather_pipeline(idx_ref, tbl_ref, bufs, consume)
    o_ref[...] = _ln_math(DN_ALPHA * h_ref[...] + ffn_ref[...], g_ref[...], b_ref[...])


def _peer_out(idx_flat, w, tbl, h, g, b):
    t = w.shape[0]
    tt = GATHER_TOKENS
    row = pl.BlockSpec((tt, D_MODEL), lambda i: (i, 0))
    vec = pl.BlockSpec((1, D_MODEL), lambda i: (0, 0))
    return pl.pallas_call(
        _peer_out_kernel,
        out_shape=jax.ShapeDtypeStruct((t, D_MODEL), F32),
        grid=(t // tt,),
        in_specs=[
            pl.BlockSpec((tt * PEER_PAIRS,), lambda i: (i,), memory_space=pltpu.SMEM),
            pl.BlockSpec((tt, PEER_PAIRS), lambda i: (i, 0)),
            pl.BlockSpec(memory_space=pltpu.VMEM),
            row, vec, vec,
        ],
        out_specs=row,
        scratch_shapes=[pltpu.VMEM((4, tt, 2 * PEER_PAIRS), F32), pltpu.VMEM((tt, D_MODEL), F32)]
        + _gather_scratch(),
        compiler_params=pltpu.CompilerParams(vmem_limit_bytes=PEER_VMEM_LIMIT),
        name="peer_out",
    )(idx_flat, w, tbl, h, g.reshape(1, D_MODEL), b.reshape(1, D_MODEL))


def _extract_topk(s, pos, k, payload=None):
    big = jnp.asarray(2 ** 30, pos.dtype)
    vals, poss, pays = [], [], []
    for _ in range(k):
        m = jnp.max(s, axis=0, keepdims=True)
        j = jnp.min(jnp.where(s == m, pos, big), axis=0, keepdims=True)
        sel = pos == j
        vals.append(m)
        poss.append(j)
        if payload is not None:
            pays.append(jnp.sum(jnp.where(sel, payload, jnp.zeros_like(payload)), axis=0, keepdims=True))
        s = jnp.where(sel, -jnp.inf, s)
    out = [jnp.concatenate(vals, axis=0), jnp.concatenate(poss, axis=0)]
    if payload is not None:
        out.append(jnp.concatenate(pays, axis=0))
    return out


def _peer_route_kernel(h_ref, wqt_ref, keys_ref, idx_ref, gate_ref):
    half_dim = PEER_DKEY // 2
    h = h_ref[...]
    n = h.shape[0]
    qt = lax.dot_general(wqt_ref[...], h, (((1,), (1,)), ((), ())), preferred_element_type=jnp.float32)
    key_iota = lax.broadcasted_iota(jnp.int32, (PEER_NKEYS, n), 0).astype(F32)
    iota16 = lax.broadcasted_iota(jnp.int32, (PEER_TOPK, n), 0).astype(F32)
    iota8 = lax.broadcasted_iota(jnp.int32, (SUBLANES, n), 0).astype(F32)
    idx_rows, gate_rows = [], []
    for hd in range(PEER_HEADS):
        tv, ti = [], []
        for half in range(2):
            r0 = hd * PEER_DKEY + half * half_dim
            s = jnp.dot(keys_ref[half], qt[r0:r0 + half_dim, :], preferred_element_type=jnp.float32)
            v, i = _extract_topk(s, key_iota, PEER_TOPK)
            tv.append(v)
            ti.append(i)
        cs = [tv[0][0:1, :] + tv[1]]
        cp = [iota16]
        ce = [ti[0][0:1, :] * PEER_NKEYS + ti[1]]
        for a in range(1, SUBLANES):
            cs.append(tv[0][a:a + 1, :] + tv[1][0:SUBLANES, :])
            cp.append(iota8 + a * PEER_TOPK)
            ce.append(ti[0][a:a + 1, :] * PEER_NKEYS + ti[1][0:SUBLANES, :])
        cs.append(tv[0][SUBLANES:, :] + tv[1][0:1, :])
        cp.append((iota8 + SUBLANES) * PEER_TOPK)
        ce.append(ti[0][SUBLANES:, :] * PEER_NKEYS + ti[1][0:1, :])
        best_s, _, best_e = _extract_topk(jnp.concatenate(cs, axis=0), jnp.concatenate(cp, axis=0),
                                          PEER_TOPK, payload=jnp.concatenate(ce, axis=0))
        ex = jnp.exp(best_s - best_s[0:1, :])
        gate_rows.append(ex / jnp.sum(ex, axis=0, keepdims=True))
        idx_rows.append(best_e)
    idx_ref[...] = (jnp.concatenate(idx_rows, axis=0) * ROW_WORDS_SUBLANES).astype(jnp.int32).T
    gate_ref[...] = jnp.concatenate(gate_rows, axis=0).T


def _peer_route(h, wqt, keys):
    t = h.shape[0]
    rt = ROUTE_TOKENS
    return pl.pallas_call(
        _peer_route_kernel,
        out_shape=(jax.ShapeDtypeStruct((t, PEER_PAIRS), jnp.int32),
                   jax.ShapeDtypeStruct((t, PEER_PAIRS), jnp.float32)),
        grid=(t // rt,),
        in_specs=[
            pl.BlockSpec((rt, D_MODEL), lambda i: (i, 0)),
            pl.BlockSpec((D_MODEL, D_MODEL), lambda i: (0, 0)),
            pl.BlockSpec((2, PEER_NKEYS, PEER_DKEY // 2), lambda i: (0, 0, 0)),
        ],
        out_specs=(pl.BlockSpec((rt, PEER_PAIRS), lambda i: (i, 0)),
                   pl.BlockSpec((rt, PEER_PAIRS), lambda i: (i, 0))),
        compiler_params=_cparams(("parallel",)),
        name="peer_route",
    )(h, wqt, keys)


def _pack_table(u):
    ub = u.astype(jnp.bfloat16)
    half = D_MODEL // 2
    lo = lax.bitcast_convert_type(ub[:, :half], jnp.uint16).astype(jnp.uint32)
    hi = lax.bitcast_convert_type(ub[:, half:], jnp.uint16).astype(jnp.uint32)
    return (lo | (hi << 16)).reshape(u.shape[0] * ROW_WORDS_SUBLANES, LANES)


def _peer_sublayer(h, wq, keys, u, v, ln_g, ln_b):
    t = h.shape[0]
    rows, gate = _peer_route(h, wq.T, keys)
    rows = rows.reshape(t * PEER_PAIRS)
    w = _peer_in(rows, h, gate, _pack_table(u))
    return _peer_out(rows, w, _pack_table(v), h, ln_g, ln_b)


def kernel(x, emb_ln_g, emb_ln_b, w_in, rg_conv_w, rg_conv_b, rg_wa, rg_ba, rg_wx, rg_bx, rg_lambda, ssm_conv_w, ssm_conv_b, ssm_dt_bias, ssm_a_log, ssm_d, ssm_norm_g, attn_sinks, w_out, ln1_g, ln1_b, peer_wq, peer_keys, peer_u, peer_v, ln2_g, ln2_b):
    bsz, seq, d = x.shape
    assert d == D_MODEL and seq % ROW_TILE == 0 and (bsz * seq) % ROUTE_TOKENS == 0
    h = _layer_norm(x.reshape(bsz * seq, d), emb_ln_g, emb_ln_b)
    for l in range(DEPTH):
        rg, z, xbc, dt, q, kv = _in_proj(h, _split_w_in(w_in[l]))
        y_a = _rg_lru(rg, seq, rg_conv_w[l], rg_conv_b[l], _block_diag(rg_wa[l]), rg_ba[l],
                      _block_diag(rg_wx[l]), rg_bx[l], rg_lambda[l])
        y_b = _ssd(z, xbc, dt, seq, ssm_conv_w[l], ssm_conv_b[l], ssm_dt_bias[l], ssm_a_log[l], ssm_d[l],
                   ssm_norm_g[l])
        y_c = _swa(q, kv, seq, attn_sinks[l])
        h = _out_proj(y_a, y_b, y_c, h, w_out[l], ln1_g[l], ln1_b[l])
        h = _peer_sublayer(h, peer_wq[l], peer_keys[l], peer_u[l], peer_v[l], ln2_g[l], ln2_b[l])
    return h.reshape(bsz, seq, d)
```

```python
import math

import jax
import jax.numpy as jnp
from jax import lax
from jax.experimental import pallas as pl
from jax.experimental.pallas import tpu as pltpu

D_MODEL = 1024
DEPTH = 2

W_LRU = D_MODEL // 2
LRU_BLOCKS = 8
LRU_C = 8.0
CONV_K = 4

SSM_HEAD_DIM = 64
SSM_D_INNER = D_MODEL
SSM_HEADS = SSM_D_INNER // SSM_HEAD_DIM
SSM_GROUPS = 2
SSM_STATE = 128
SSM_CHUNK = 128
SSM_CONV_CH = SSM_D_INNER + 2 * SSM_GROUPS * SSM_STATE
SSD_GROUP_W = SSM_D_INNER // SSM_GROUPS

ATTN_HEAD_DIM = 64
ATTN_Q_HEADS = (D_MODEL // 2) // ATTN_HEAD_DIM
ATTN_KV_HEADS = 2
ATTN_REP = ATTN_Q_HEADS // ATTN_KV_HEADS
ATTN_BLOCK = 128
ATTN_Q_W = ATTN_Q_HEADS * ATTN_HEAD_DIM
ATTN_KV_W = 2 * ATTN_KV_HEADS * ATTN_HEAD_DIM

MIX_WIDTH = W_LRU + SSM_D_INNER + ATTN_Q_W

PEER_HEADS = 8
PEER_NKEYS = 128
PEER_DKEY = 128
PEER_TOPK = 16
PEER_PAIRS = PEER_HEADS * PEER_TOPK

DN_ALPHA = (2 * DEPTH) ** 0.25
LN_EPS = 1e-5

SUBLANES = 8
LANES = 128
HALO = SUBLANES
ROW_WORDS_SUBLANES = D_MODEL // 2 // LANES
TOKEN_SUBLANES = D_MODEL // LANES

ROW_TILE = 512
GATHER_TOKENS = 64
GATHER_GROUP = 2
FUSED_TOKENS = 128
VMEM_LIMIT = 48 * 1024 * 1024
PEER_VMEM_LIMIT = 56 * 1024 * 1024

BF = jnp.bfloat16
F32 = jnp.float32
HI = lax.Precision.HIGHEST

PROJ_SEGS = (2 * W_LRU, SSM_D_INNER, SSM_CONV_CH, LANES, ATTN_Q_W, ATTN_KV_W)


def _cparams(sem):
    return pltpu.CompilerParams(dimension_semantics=sem, vmem_limit_bytes=VMEM_LIMIT)


def _softplus(x):
    return jnp.maximum(x, 0.0) + jnp.log1p(jnp.exp(-jnp.abs(x)))


def _sigmoid(x):
    return 1.0 / (1.0 + jnp.exp(-x))


def _silu(x):
    return x * _sigmoid(x)


def _gelu_tanh(x):
    return 0.5 * x * (1.0 + jnp.tanh(math.sqrt(2.0 / math.pi) * (x + 0.044715 * (x * x * x))))


def _ln_math(x, g, b):
    mu = jnp.mean(x, axis=-1, keepdims=True)
    xc = x - mu
    var = jnp.mean(xc * xc, axis=-1, keepdims=True)
    return xc * lax.rsqrt(var + LN_EPS) * g + b


def _ln_kernel(a_ref, g_ref, b_ref, o_ref):
    o_ref[...] = _ln_math(a_ref[...], g_ref[...], b_ref[...])


def _res_ln_kernel(a_ref, r_ref, g_ref, b_ref, o_ref):
    o_ref[...] = _ln_math(DN_ALPHA * a_ref[...] + r_ref[...], g_ref[...], b_ref[...])


def _layer_norm(a, g, b, res=None):
    t, d = a.shape
    row = pl.BlockSpec((ROW_TILE, d), lambda i: (i, 0))
    vec = pl.BlockSpec((1, d), lambda i: (0, 0))
    if res is None:
        kern, args, specs = _ln_kernel, (a,), [row]
    else:
        kern, args, specs = _res_ln_kernel, (a, res), [row, row]
    return pl.pallas_call(
        kern, out_shape=jax.ShapeDtypeStruct((t, d), F32), grid=(t // ROW_TILE,),
        in_specs=specs + [vec, vec], out_specs=row, compiler_params=_cparams(("parallel",)),
        name="layer_norm",
    )(*args, g.reshape(1, d), b.reshape(1, d))


def _inproj_kernel(h_ref, *refs):
    n = len(PROJ_SEGS)
    hb = h_ref[...].astype(BF)
    for w_ref, o_ref in zip(refs[:n], refs[n:]):
        o_ref[...] = jnp.dot(hb, w_ref[...], preferred_element_type=F32)


def _in_proj(h, ws):
    t = h.shape[0]
    return pl.pallas_call(
        _inproj_kernel,
        out_shape=tuple(jax.ShapeDtypeStruct((t, w), F32) for w in PROJ_SEGS),
        grid=(t // ROW_TILE,),
        in_specs=[pl.BlockSpec((ROW_TILE, D_MODEL), lambda i: (i, 0))]
        + [pl.BlockSpec((D_MODEL, w), lambda i: (0, 0)) for w in PROJ_SEGS],
        out_specs=tuple(pl.BlockSpec((ROW_TILE, w), lambda i: (i, 0)) for w in PROJ_SEGS),
        compiler_params=_cparams(("parallel",)),
        name="in_proj",
    )(h, *ws)


def _split_w_in(w_in):
    w = w_in.astype(BF)
    c = [0]
    for width in (2 * W_LRU, SSM_D_INNER, SSM_CONV_CH, SSM_HEADS, ATTN_Q_W, ATTN_KV_W):
        c.append(c[-1] + width)
    segs = [w[:, c[i]:c[i + 1]] for i in range(6)]
    segs[3] = jnp.pad(segs[3], ((0, 0), (0, LANES - SSM_HEADS)))
    return tuple(segs)


def _shift_rows(x, s, fill):
    n = x.shape[0]
    if s % SUBLANES == 0:
        return jnp.concatenate([jnp.full((s,) + x.shape[1:], fill, x.dtype), x[:n - s]], axis=0)
    rolled = pltpu.roll(x, s, axis=0)
    row = lax.broadcasted_iota(jnp.int32, x.shape, 0)
    return jnp.where(row < s, fill, rolled)


def _causal_conv(x, halo, w_ref, b_ref):
    n = x.shape[0]
    xp = jnp.concatenate([halo, x], axis=0)
    out = b_ref[...] + w_ref[CONV_K - 1:CONV_K, :] * x
    for s in range(1, CONV_K):
        out = out + w_ref[CONV_K - 1 - s:CONV_K - s, :] * pltpu.roll(xp, s, axis=0)[HALO:HALO + n]
    return out


def _rglru_kernel(rg_ref, cw_ref, cb_ref, wa_ref, ba_ref, wx_ref, bx_ref, lam_ref, o_ref, halo_ref, carry_ref):
    n = rg_ref.shape[0]

    @pl.when(pl.program_id(1) == 0)
    def _():
        halo_ref[...] = jnp.zeros_like(halo_ref)
        carry_ref[...] = jnp.zeros_like(carry_ref)

    x = rg_ref[:, 0:W_LRU]
    xc = _causal_conv(x, halo_ref[...], cw_ref, cb_ref)
    halo_ref[...] = x[n - HALO:n]
    xb = xc.astype(BF)
    r = _sigmoid(jnp.dot(xb, wa_ref[...], preferred_element_type=F32) + ba_ref[...])
    i = _sigmoid(jnp.dot(xb, wx_ref[...], preferred_element_type=F32) + bx_ref[...])
    log_a = (-LRU_C * r) * _softplus(-lam_ref[...])
    a = jnp.exp(log_a)
    u = jnp.sqrt(-jnp.tanh(log_a) * (a * a + 1.0)) * (i * xc)
    k = 1
    while k < n:
        u = a * _shift_rows(u, k, 0.0) + u
        a = a * _shift_rows(a, k, 1.0)
        k *= 2
    h = u + a * carry_ref[0:1, :]
    carry_ref[...] = jnp.broadcast_to(h[n - 1:n, :], carry_ref.shape)
    o_ref[...] = _gelu_tanh(rg_ref[:, W_LRU:2 * W_LRU]) * h


def _rg_lru(rg, seq, conv_w, conv_b, wa, ba, wx, bx, lam):
    t = rg.shape[0]
    nb = seq // ROW_TILE
    row = lambda b, j: (b * nb + j, 0)
    const = lambda b, j: (0, 0)
    vec = pl.BlockSpec((1, W_LRU), const)
    return pl.pallas_call(
        _rglru_kernel,
        out_shape=jax.ShapeDtypeStruct((t, W_LRU), F32),
        grid=(t // seq, nb),
        in_specs=[pl.BlockSpec((ROW_TILE, 2 * W_LRU), row),
                  pl.BlockSpec((CONV_K, W_LRU), const), vec,
                  pl.BlockSpec((W_LRU, W_LRU), const), vec,
                  pl.BlockSpec((W_LRU, W_LRU), const), vec, vec],
        out_specs=pl.BlockSpec((ROW_TILE, W_LRU), row),
        scratch_shapes=[pltpu.VMEM((HALO, W_LRU), F32), pltpu.VMEM((SUBLANES, W_LRU), F32)],
        compiler_params=_cparams(("parallel", "arbitrary")),
        name="rg_lru",
    )(rg, conv_w, conv_b.reshape(1, -1), wa, ba.reshape(1, -1), wx, bx.reshape(1, -1), lam.reshape(1, -1))


def _block_diag(w):
    nb, c, _ = w.shape
    eye = jnp.eye(nb, dtype=w.dtype)
    return (eye[:, None, :, None] * w[:, :, None, :]).reshape(nb * c, nb * c).astype(BF)


def _ssd_kernel(z_ref, xbc_ref, dt_ref, cw_ref, cb_ref, dtb_ref, alog_ref, dskip_ref, ng_ref, expand_ref,
                o_ref, halo_ref, state_ref):
    L = SSM_CHUNK

    @pl.when(pl.program_id(1) == 0)
    def _():
        halo_ref[...] = jnp.zeros_like(halo_ref)
        state_ref[...] = jnp.zeros_like(state_ref)

    xbc = xbc_ref[...]
    conv = _silu(_causal_conv(xbc, halo_ref[...], cw_ref, cb_ref))
    halo_ref[...] = xbc[L - HALO:L]
    xs = conv[:, 0:SSM_D_INNER]
    bm = conv[:, SSM_D_INNER:SSM_D_INNER + SSM_GROUPS * SSM_STATE]
    cm = conv[:, SSM_D_INNER + SSM_GROUPS * SSM_STATE:]

    dt = _softplus(dt_ref[...] + dtb_ref[...])
    da = dt * (-jnp.exp(alog_ref[...]))
    row = lax.broadcasted_iota(jnp.int32, (L, L), 0)
    col = lax.broadcasted_iota(jnp.int32, (L, L), 1)
    causal = col <= row
    a_cs = jnp.dot(causal.astype(F32), da, precision=HI, preferred_element_type=F32)
    a_cs_t = a_cs.T
    expand = expand_ref[...]
    dt_full = jnp.dot(dt, expand, precision=HI, preferred_element_type=F32)
    acs_full = jnp.dot(a_cs, expand, precision=HI, preferred_element_type=F32)
    alast_full = acs_full[L - 1:L, :]
    xdt = xs * dt_full
    xdt_b = xdt.astype(BF)
    xst_b = (xdt * jnp.exp(alast_full - acs_full)).astype(BF)
    left = lax.broadcasted_iota(jnp.int32, (L, LANES), 1) < SSM_HEAD_DIM

    y_parts = []
    for g in range(SSM_GROUPS):
        cg = cm[:, g * SSM_STATE:(g + 1) * SSM_STATE].astype(BF)
        bg = bm[:, g * SSM_STATE:(g + 1) * SSM_STATE].astype(BF)
        cb = lax.dot_general(cg, bg, (((1,), (1,)), ((), ())), preferred_element_type=F32)
        st = state_ref[:, g * SSD_GROUP_W:(g + 1) * SSD_GROUP_W]
        y_off = jnp.dot(cg, st.astype(BF), preferred_element_type=F32)
        for j in range(SSD_GROUP_W // LANES):
            h0 = g * (SSM_HEADS // SSM_GROUPS) + 2 * j
            ms = []
            for h in (h0, h0 + 1):
                seg = a_cs[:, h:h + 1] - a_cs_t[h:h + 1, :]
                ms.append((cb * jnp.exp(jnp.where(causal, seg, -jnp.inf))).astype(BF))
            c0 = h0 * SSM_HEAD_DIM
            x2 = xdt_b[:, c0:c0 + LANES]
            zero = jnp.zeros_like(x2)
            xblk = jnp.concatenate([jnp.where(left, x2, zero), jnp.where(left, zero, x2)], axis=0)
            y_parts.append(jnp.dot(jnp.concatenate(ms, axis=1), xblk, preferred_element_type=F32)
                           + y_off[:, j * LANES:(j + 1) * LANES] * jnp.exp(acs_full[:, c0:c0 + LANES]))
        new = lax.dot_general(bg, xst_b[:, g * SSD_GROUP_W:(g + 1) * SSD_GROUP_W], (((0,), (0,)), ((), ())),
                              preferred_element_type=F32)
        state_ref[:, g * SSD_GROUP_W:(g + 1) * SSD_GROUP_W] = (
            st * jnp.exp(alast_full[:, g * SSD_GROUP_W:(g + 1) * SSD_GROUP_W]) + new)
    y = jnp.concatenate(y_parts, axis=1) + dskip_ref[...] * xs
    y = y * _silu(z_ref[...])
    outs = []
    for g in range(SSM_GROUPS):
        yg = y[:, g * SSD_GROUP_W:(g + 1) * SSD_GROUP_W]
        outs.append(yg * lax.rsqrt(jnp.mean(yg * yg, axis=-1, keepdims=True) + LN_EPS))
    o_ref[...] = jnp.concatenate(outs, axis=1) * ng_ref[...]


def _ssd(z, xbc, dt, seq, conv_w, conv_b, dt_bias, a_log, d_skip, norm_g):
    t = z.shape[0]
    nc = seq // SSM_CHUNK
    row = lambda b, c: (b * nc + c, 0)
    const = lambda b, c: (0, 0)
    pad = LANES - SSM_HEADS
    expand = (jnp.arange(LANES)[:, None] == (jnp.arange(SSM_D_INNER)[None, :] // SSM_HEAD_DIM)).astype(F32)
    return pl.pallas_call(
        _ssd_kernel,
        out_shape=jax.ShapeDtypeStruct((t, SSM_D_INNER), F32),
        grid=(t // seq, nc),
        in_specs=[pl.BlockSpec((SSM_CHUNK, SSM_D_INNER), row),
                  pl.BlockSpec((SSM_CHUNK, SSM_CONV_CH), row),
                  pl.BlockSpec((SSM_CHUNK, LANES), row),
                  pl.BlockSpec((CONV_K, SSM_CONV_CH), const),
                  pl.BlockSpec((1, SSM_CONV_CH), const),
                  pl.BlockSpec((1, LANES), const),
                  pl.BlockSpec((1, LANES), const),
                  pl.BlockSpec((1, SSM_D_INNER), const),
                  pl.BlockSpec((1, SSM_D_INNER), const),
                  pl.BlockSpec((LANES, SSM_D_INNER), const)],
        out_specs=pl.BlockSpec((SSM_CHUNK, SSM_D_INNER), row),
        scratch_shapes=[pltpu.VMEM((HALO, SSM_CONV_CH), F32), pltpu.VMEM((SSM_STATE, SSM_D_INNER), F32)],
        compiler_params=_cparams(("parallel", "arbitrary")),
        name="ssd",
    )(z, xbc, dt, conv_w, conv_b.reshape(1, -1), jnp.pad(dt_bias, (0, pad)).reshape(1, -1),
      jnp.pad(a_log, (0, pad)).reshape(1, -1), jnp.repeat(d_skip, SSM_HEAD_DIM).reshape(1, -1),
      norm_g.reshape(1, -1), expand)


def _swa_kernel(sink_ref, q_ref, kv_ref, kvp_ref, o_ref):
    L = ATTN_BLOCK
    hd = ATTN_HEAD_DIM
    first = pl.program_id(1) == 0
    kv = jnp.concatenate([kvp_ref[...], kv_ref[...]], axis=0)
    kk = kv[:, 0:LANES]
    vv = kv[:, LANES:2 * LANES].astype(BF)
    lane_k = lax.broadcasted_iota(jnp.int32, (2 * L, LANES), 1)
    qi = lax.broadcasted_iota(jnp.int32, (2 * L, 2 * L), 0) % L
    kj = lax.broadcasted_iota(jnp.int32, (2 * L, 2 * L), 1)
    rel = qi + L - kj
    valid = (rel >= 0) & (rel < L) & (jnp.logical_not(first) | (kj >= L))
    top = lax.broadcasted_iota(jnp.int32, (2 * L, 1), 0) < L
    lane_o = lax.broadcasted_iota(jnp.int32, (L, LANES), 1)
    scale = hd ** -0.5
    for j in range(ATTN_Q_HEADS // 2):
        g = (2 * j) // ATTN_REP
        q2 = q_ref[:, j * LANES:(j + 1) * LANES]
        q2r = pltpu.roll(q2, hd, axis=1)
        kg = jnp.where((lane_k >= g * hd) & (lane_k < (g + 1) * hd), kk, 0.0).astype(BF)
        qa, qb = (q2, q2r) if g == 0 else (q2r, q2)
        qs = jnp.concatenate([qa, qb], axis=0).astype(BF)
        logits = lax.dot_general(qs, kg, (((1,), (1,)), ((), ())), preferred_element_type=F32) * scale
        logits = jnp.where(valid, logits, -jnp.inf)
        sink = jnp.where(top, sink_ref[2 * j], sink_ref[2 * j + 1])
        m = jnp.maximum(jnp.max(logits, axis=-1, keepdims=True), sink)
        p = jnp.exp(logits - m)
        probs = p / (jnp.sum(p, axis=-1, keepdims=True) + jnp.exp(sink - m))
        o = jnp.dot(probs.astype(BF), vv, preferred_element_type=F32)
        oa, ob = o[0:L], o[L:2 * L]
        if g == 0:
            out2 = jnp.where(lane_o < hd, oa, pltpu.roll(ob, hd, axis=1))
        else:
            out2 = jnp.where(lane_o < hd, pltpu.roll(oa, hd, axis=1), ob)
        o_ref[:, j * LANES:(j + 1) * LANES] = out2


def _swa(q, kv, seq, sinks):
    t = q.shape[0]
    nb = seq // ATTN_BLOCK
    return pl.pallas_call(
        _swa_kernel,
        out_shape=jax.ShapeDtypeStruct((t, ATTN_Q_W), F32),
        grid=(t // seq, nb),
        in_specs=[pl.BlockSpec(memory_space=pltpu.SMEM),
                  pl.BlockSpec((ATTN_BLOCK, ATTN_Q_W), lambda b, n: (b * nb + n, 0)),
                  pl.BlockSpec((ATTN_BLOCK, ATTN_KV_W), lambda b, n: (b * nb + n, 0)),
                  pl.BlockSpec((ATTN_BLOCK, ATTN_KV_W), lambda b, n: (b * nb + jnp.maximum(n - 1, 0), 0))],
        out_specs=pl.BlockSpec((ATTN_BLOCK, ATTN_Q_W), lambda b, n: (b * nb + n, 0)),
        compiler_params=_cparams(("parallel", "parallel")),
        name="swa",
    )(sinks, q, kv, kv)


def _outproj_kernel(ya_ref, yb_ref, yc_ref, h_ref, wa_ref, wb_ref, wc_ref, g_ref, b_ref, o_ref):
    mix = (jnp.dot(ya_ref[...].astype(BF), wa_ref[...], preferred_element_type=F32)
           + jnp.dot(yb_ref[...].astype(BF), wb_ref[...], preferred_element_type=F32)
           + jnp.dot(yc_ref[...].astype(BF), wc_ref[...], preferred_element_type=F32))
    o_ref[...] = _ln_math(DN_ALPHA * h_ref[...] + mix, g_ref[...], b_ref[...])


def _out_proj(ya, yb, yc, h, w_out, g, b):
    t = h.shape[0]
    w = w_out.astype(BF)
    row = lambda i: (i, 0)
    const = lambda i: (0, 0)
    c1, c2 = W_LRU, W_LRU + SSM_D_INNER
    return pl.pallas_call(
        _outproj_kernel,
        out_shape=jax.ShapeDtypeStruct((t, D_MODEL), F32),
        grid=(t // ROW_TILE,),
        in_specs=[pl.BlockSpec((ROW_TILE, W_LRU), row), pl.BlockSpec((ROW_TILE, SSM_D_INNER), row),
                  pl.BlockSpec((ROW_TILE, ATTN_Q_W), row), pl.BlockSpec((ROW_TILE, D_MODEL), row),
                  pl.BlockSpec((W_LRU, D_MODEL), const), pl.BlockSpec((SSM_D_INNER, D_MODEL), const),
                  pl.BlockSpec((ATTN_Q_W, D_MODEL), const),
                  pl.BlockSpec((1, D_MODEL), const), pl.BlockSpec((1, D_MODEL), const)],
        out_specs=pl.BlockSpec((ROW_TILE, D_MODEL), row),
        compiler_params=_cparams(("parallel",)),
        name="out_proj",
    )(ya, yb, yc, h, w[0:c1], w[c1:c2], w[c2:MIX_WIDTH], g.reshape(1, -1), b.reshape(1, -1))


def _gather_token(idx_ref, tbl_ref, buf_ref, t):
    rs = ROW_WORDS_SUBLANES
    for p in range(PEER_PAIRS):
        if p % SUBLANES == 0:
            window = idx_ref.at[t, pl.ds(p, SUBLANES)]
        r = pl.multiple_of(window[p % SUBLANES], rs)
        buf_ref[pl.ds(rs * p, rs), :] = tbl_ref[pl.ds(r, rs), :]


def _pair_rows_bf16(buf_ref, s):
    return pltpu.bitcast(buf_ref[pl.ds(s, PEER_PAIRS, stride=ROW_WORDS_SUBLANES), :], BF)


def _split_hi_lo(x):
    hi = x.astype(BF)
    lo = (x - hi.astype(F32)).astype(BF)
    return hi, lo


def _gather_pipeline(idx_ref, tbl_ref, bufs, consume, n_tokens, side=None, side_steps=0):
    u = GATHER_GROUP
    halves = (bufs[:u], bufs[u:])
    n_groups = n_tokens // u

    def gather(buf_ref, t):
        _gather_token(idx_ref, tbl_ref, buf_ref, t)
        for _ in range(side_steps):
            next(side, None)

    for k in range(u):
        gather(halves[0][k], k)
    for g in range(n_groups):
        cur, nxt = halves[g % 2], halves[(g + 1) % 2]
        for k in range(u):
            if g + 1 < n_groups:
                gather(nxt[k], (g + 1) * u + k)
            consume(cur[k], g * u + k)
    if side is not None:
        for _ in side:
            pass


def _peer_route_in_kernel(hn_ref, wqt_ref, keys_ref, idx_ref, x_ref, gate_ref, tbl_ref,
                          idx_o_ref, gate_o_ref, o_ref, m_ref, *bufs):
    rs = ROW_WORDS_SUBLANES
    two_p = 2 * PEER_PAIRS
    row = lax.broadcasted_iota(jnp.int32, (2 * SUBLANES, two_p), 0) % SUBLANES
    even = lax.broadcasted_iota(jnp.int32, (2 * SUBLANES, two_p), 1) % 2 == 0
    masks = [((row == s) & even) | ((row == rs + s) & jnp.logical_not(even)) for s in range(rs)]

    def consume(buf_ref, t):
        x_row = x_ref[pl.ds(t, 1), :]
        x8 = jnp.concatenate([x_row[:, c * LANES:(c + 1) * LANES] for c in range(TOKEN_SUBLANES)], axis=0)
        x_hi, x_lo = _split_hi_lo(x8)
        x16 = jnp.concatenate([x_hi, x_lo], axis=0)
        acc = jnp.zeros((2 * SUBLANES, two_p), F32)
        for s in range(rs):
            r = lax.dot_general(x16, _pair_rows_bf16(buf_ref, s), (((1,), (1,)), ((), ())),
                                preferred_element_type=F32)
            acc = jnp.where(masks[s], r, acc)
        m_ref[pl.ds(t, 1), :] = jnp.sum(acc, axis=0, keepdims=True)

    n = x_ref.shape[0]
    side = _route_steps(hn_ref, wqt_ref, keys_ref, idx_o_ref, gate_o_ref)
    _gather_pipeline(idx_ref, tbl_ref, bufs, consume, n, side, -(-ROUTE_STEPS // n))
    fold = (lax.broadcasted_iota(jnp.int32, (two_p, PEER_PAIRS), 0) // 2
            == lax.broadcasted_iota(jnp.int32, (two_p, PEER_PAIRS), 1)).astype(BF)
    m_hi, m_lo = _split_hi_lo(m_ref[...])
    a = jnp.dot(m_hi, fold, preferred_element_type=F32) + jnp.dot(m_lo, fold, preferred_element_type=F32)
    o_ref[...] = gate_ref[...] * (0.5 * a * (1.0 + lax.erf(a * (2.0 ** -0.5))))


def _gather_scratch():
    return [pltpu.VMEM((ROW_WORDS_SUBLANES * PEER_PAIRS, LANES), jnp.uint32)] * (2 * GATHER_GROUP)


def _peer_route_in(h, wqt, keys, tbl, seq):
    t = h.shape[0]
    ft = FUSED_TOKENS
    nb, nk = t // seq, seq // ft
    padded = (nb + 1) * seq
    route_src = lambda s, k: (jnp.minimum(s, nb - 1) * nk + k, 0)
    route_dst = lambda s, k: (s * nk + k, 0)
    gather_src = lambda s, k: (jnp.maximum(s - 1, 0) * nk + k, 0)
    gather_slot = lambda s, k: (((s + nb) % (nb + 1)) * nk + k, 0)
    const = lambda s, k: (0, 0)
    pairs_block = lambda imap, **kw: pl.BlockSpec((ft, PEER_PAIRS), imap, **kw)
    rows, gate, w = pl.pallas_call(
        _peer_route_in_kernel,
        out_shape=(jax.ShapeDtypeStruct((padded, PEER_PAIRS), jnp.int32),
                   jax.ShapeDtypeStruct((padded, PEER_PAIRS), F32),
                   jax.ShapeDtypeStruct((padded, PEER_PAIRS), F32)),
        grid=(nb + 1, nk),
        in_specs=[
            pl.BlockSpec((ft, D_MODEL), route_src),
            pl.BlockSpec((D_MODEL, D_MODEL), const),
            pl.BlockSpec((2, PEER_NKEYS, PEER_DKEY // 2), lambda s, k: (0, 0, 0)),
            pairs_block(gather_slot, memory_space=pltpu.SMEM),
            pl.BlockSpec((ft, D_MODEL), gather_src),
            pairs_block(gather_slot),
            pl.BlockSpec(memory_space=pltpu.VMEM),
        ],
        out_specs=(pairs_block(route_dst), pairs_block(route_dst), pairs_block(gather_slot)),
        scratch_shapes=[pltpu.VMEM((ft, 2 * PEER_PAIRS), F32)] + _gather_scratch(),
        input_output_aliases={3: 0, 5: 1},
        compiler_params=pltpu.CompilerParams(dimension_semantics=("arbitrary", "arbitrary"),
                                             vmem_limit_bytes=PEER_VMEM_LIMIT),
        name="peer_route_in",
    )(h, wqt, keys, jnp.zeros((padded, PEER_PAIRS), jnp.int32), h, jnp.zeros((padded, PEER_PAIRS), F32), tbl)
    return rows, w


def _peer_out_kernel(idx_ref, w_ref, tbl_ref, h_ref, g_ref, b_ref, o_ref, spread_ref, ffn_ref, *bufs):
    rs = ROW_WORDS_SUBLANES
    two_p = 2 * PEER_PAIRS
    lane = lax.broadcasted_iota(jnp.int32, (PEER_PAIRS, two_p), 1)
    pair = lax.broadcasted_iota(jnp.int32, (PEER_PAIRS, two_p), 0)
    to_even = (lane == 2 * pair).astype(BF)
    to_odd = (lane == 2 * pair + 1).astype(BF)
    w_hi, w_lo = _split_hi_lo(w_ref[...])
    for k, (wv, place) in enumerate(((w_hi, to_even), (w_hi, to_odd), (w_lo, to_even), (w_lo, to_odd))):
        spread_ref[k] = jnp.dot(wv, place, preferred_element_type=F32)
    row4 = lax.broadcasted_iota(jnp.int32, (4 * rs, two_p), 0) % rs

    def consume(buf_ref, t):
        w16 = jnp.concatenate([jnp.broadcast_to(spread_ref[k, pl.ds(t, 1), :], (rs, two_p)) for k in range(4)],
                              axis=0)
        acc = jnp.zeros((4 * rs, LANES), F32)
        for s in range(rs):
            lhs = jnp.where(row4 == s, w16, 0.0).astype(BF)
            acc = acc + jnp.dot(lhs, _pair_rows_bf16(buf_ref, s), preferred_element_type=F32)
        out8 = acc[0:TOKEN_SUBLANES] + acc[TOKEN_SUBLANES:]
        for c in range(TOKEN_SUBLANES):
            ffn_ref[pl.ds(t, 1), c * LANES:(c + 1) * LANES] = out8[c:c + 1, :]

    _gather_pipeline(idx_ref, tbl_ref, bufs, consume, w_ref.shape[0])
    o_ref[...] = _ln_math(DN_ALPHA * h_ref[...] + ffn_ref[...], g_ref[...], b_ref[...])


def _peer_out(rows, w, tbl, h, g, b):
    t = h.shape[0]
    tt = GATHER_TOKENS
    row = pl.BlockSpec((tt, D_MODEL), lambda i: (i, 0))
    vec = pl.BlockSpec((1, D_MODEL), lambda i: (0, 0))
    return pl.pallas_call(
        _peer_out_kernel,
        out_shape=jax.ShapeDtypeStruct((t, D_MODEL), F32),
        grid=(t // tt,),
        in_specs=[
            pl.BlockSpec((tt, PEER_PAIRS), lambda i: (i, 0), memory_space=pltpu.SMEM),
            pl.BlockSpec((tt, PEER_PAIRS), lambda i: (i, 0)),
            pl.BlockSpec(memory_space=pltpu.VMEM),
            row, vec, vec,
        ],
        out_specs=row,
        scratch_shapes=[pltpu.VMEM((4, tt, 2 * PEER_PAIRS), F32), pltpu.VMEM((tt, D_MODEL), F32)]
        + _gather_scratch(),
        compiler_params=pltpu.CompilerParams(vmem_limit_bytes=PEER_VMEM_LIMIT),
        name="peer_out",
    )(rows, w, tbl, h, g.reshape(1, D_MODEL), b.reshape(1, D_MODEL))


def _extract_topk_steps(s, pos, k, payload=None):
    big = jnp.asarray(2 ** 30, pos.dtype)
    vals, poss, pays = [], [], []
    for _ in range(k):
        m = jnp.max(s, axis=0, keepdims=True)
        j = jnp.min(jnp.where(s == m, pos, big), axis=0, keepdims=True)
        sel = pos == j
        vals.append(m)
        poss.append(j)
        if payload is not None:
            pays.append(jnp.sum(jnp.where(sel, payload, jnp.zeros_like(payload)), axis=0, keepdims=True))
        s = jnp.where(sel, -jnp.inf, s)
        yield
    out = [jnp.concatenate(vals, axis=0), jnp.concatenate(poss, axis=0)]
    if payload is not None:
        out.append(jnp.concatenate(pays, axis=0))
    return out


ROUTE_STEPS = PEER_HEADS * (3 * PEER_TOPK + 1)


def _route_steps(h_ref, wqt_ref, keys_ref, idx_ref, gate_ref):
    half_dim = PEER_DKEY // 2
    hb = h_ref[...].astype(BF)
    n = hb.shape[0]
    key_iota = lax.broadcasted_iota(jnp.int32, (PEER_NKEYS, n), 0).astype(F32)
    iota16 = lax.broadcasted_iota(jnp.int32, (PEER_TOPK, n), 0).astype(F32)
    iota8 = lax.broadcasted_iota(jnp.int32, (SUBLANES, n), 0).astype(F32)
    for hd in range(PEER_HEADS):
        qt = lax.dot_general(wqt_ref[hd * PEER_DKEY:(hd + 1) * PEER_DKEY, :], hb,
                             (((1,), (1,)), ((), ())), preferred_element_type=F32)
        tv, ti = [], []
        for half in range(2):
            s = jnp.dot(keys_ref[half], qt[half * half_dim:(half + 1) * half_dim, :], preferred_element_type=F32)
            v, i = yield from _extract_topk_steps(s, key_iota, PEER_TOPK)
            tv.append(v)
            ti.append(i)
        cs = [tv[0][0:1, :] + tv[1]]
        cp = [iota16]
        ce = [ti[0][0:1, :] * PEER_NKEYS + ti[1]]
        for a in range(1, SUBLANES):
            cs.append(tv[0][a:a + 1, :] + tv[1][0:SUBLANES, :])
            cp.append(iota8 + a * PEER_TOPK)
            ce.append(ti[0][a:a + 1, :] * PEER_NKEYS + ti[1][0:SUBLANES, :])
        cs.append(tv[0][SUBLANES:, :] + tv[1][0:1, :])
        cp.append((iota8 + SUBLANES) * PEER_TOPK)
        ce.append(ti[0][SUBLANES:, :] * PEER_NKEYS + ti[1][0:1, :])
        best_s, _, best_e = yield from _extract_topk_steps(
            jnp.concatenate(cs, axis=0), jnp.concatenate(cp, axis=0), PEER_TOPK, payload=jnp.concatenate(ce, axis=0))
        ex = jnp.exp(best_s - best_s[0:1, :])
        gate = ex / jnp.sum(ex, axis=0, keepdims=True)
        cols = slice(hd * PEER_TOPK, (hd + 1) * PEER_TOPK)
        idx_ref[:, cols] = (best_e * ROW_WORDS_SUBLANES).astype(jnp.int32).T
        gate_ref[:, cols] = gate.T
        yield


def _pack_table(u):
    ub = u.astype(jnp.bfloat16)
    half = D_MODEL // 2
    lo = lax.bitcast_convert_type(ub[:, :half], jnp.uint16).astype(jnp.uint32)
    hi = lax.bitcast_convert_type(ub[:, half:], jnp.uint16).astype(jnp.uint32)
    return (lo | (hi << 16)).reshape(u.shape[0] * ROW_WORDS_SUBLANES, LANES)


def _peer_sublayer(h, seq, wq, keys, u, v, ln_g, ln_b):
    rows, w = _peer_route_in(h, wq.T.astype(BF), keys, _pack_table(u), seq)
    return _peer_out(rows, w, _pack_table(v), h, ln_g, ln_b)


def kernel(x, emb_ln_g, emb_ln_b, w_in, rg_conv_w, rg_conv_b, rg_wa, rg_ba, rg_wx, rg_bx, rg_lambda, ssm_conv_w, ssm_conv_b, ssm_dt_bias, ssm_a_log, ssm_d, ssm_norm_g, attn_sinks, w_out, ln1_g, ln1_b, peer_wq, peer_keys, peer_u, peer_v, ln2_g, ln2_b):
    bsz, seq, d = x.shape
    assert d == D_MODEL and seq % ROW_TILE == 0 and seq % FUSED_TOKENS == 0
    h = _layer_norm(x.reshape(bsz * seq, d), emb_ln_g, emb_ln_b)
    for l in range(DEPTH):
        rg, z, xbc, dt, q, kv = _in_proj(h, _split_w_in(w_in[l]))
        y_a = _rg_lru(rg, seq, rg_conv_w[l], rg_conv_b[l], _block_diag(rg_wa[l]), rg_ba[l],
                      _block_diag(rg_wx[l]), rg_bx[l], rg_lambda[l])
        y_b = _ssd(z, xbc, dt, seq, ssm_conv_w[l], ssm_conv_b[l], ssm_dt_bias[l], ssm_a_log[l], ssm_d[l],
                   ssm_norm_g[l])
        y_c = _swa(q, kv, seq, attn_sinks[l])
        h = _out_proj(y_a, y_b, y_c, h, w_out[l], ln1_g[l], ln1_b[l])
        h = _peer_sublayer(h, seq, peer_wq[l], peer_keys[l], peer_u[l], peer_v[l], ln2_g[l], ln2_b[l])
    return h.reshape(bsz, seq, d)
```

```python
import math

import jax
import jax.numpy as jnp
from jax import lax
from jax.experimental import pallas as pl
from jax.experimental.pallas import tpu as pltpu

D_MODEL = 1024
DEPTH = 2

W_LRU = D_MODEL // 2
LRU_BLOCKS = 8
LRU_C = 8.0
CONV_K = 4

SSM_HEAD_DIM = 64
SSM_D_INNER = D_MODEL
SSM_HEADS = SSM_D_INNER // SSM_HEAD_DIM
SSM_GROUPS = 2
SSM_STATE = 128
SSM_CHUNK = 128
SSM_CONV_CH = SSM_D_INNER + 2 * SSM_GROUPS * SSM_STATE
SSD_GROUP_W = SSM_D_INNER // SSM_GROUPS

ATTN_HEAD_DIM = 64
ATTN_Q_HEADS = (D_MODEL // 2) // ATTN_HEAD_DIM
ATTN_KV_HEADS = 2
ATTN_REP = ATTN_Q_HEADS // ATTN_KV_HEADS
ATTN_BLOCK = 128
ATTN_Q_W = ATTN_Q_HEADS * ATTN_HEAD_DIM
ATTN_KV_W = 2 * ATTN_KV_HEADS * ATTN_HEAD_DIM

MIX_WIDTH = W_LRU + SSM_D_INNER + ATTN_Q_W

PEER_HEADS = 8
PEER_NKEYS = 128
PEER_DKEY = 128
PEER_TOPK = 16
PEER_PAIRS = PEER_HEADS * PEER_TOPK

DN_ALPHA = (2 * DEPTH) ** 0.25
LN_EPS = 1e-5

SUBLANES = 8
LANES = 128
HALO = SUBLANES
ROW_WORDS_SUBLANES = D_MODEL // 2 // LANES
TOKEN_SUBLANES = D_MODEL // LANES

ROW_TILE = 512
PAIR_LANES = 2 * ROW_WORDS_SUBLANES
GROUP_PAIRS = 32
GROUP_K = GROUP_PAIRS * PAIR_LANES
GATHER_TOKENS = 64
FUSED_TOKENS = 128
VMEM_LIMIT = 48 * 1024 * 1024
PEER_VMEM_LIMIT = 56 * 1024 * 1024

BF = jnp.bfloat16
F32 = jnp.float32
HI = lax.Precision.HIGHEST

PROJ_SEGS = (2 * W_LRU, SSM_D_INNER, SSM_CONV_CH, LANES, ATTN_Q_W, ATTN_KV_W)


def _cparams(sem):
    return pltpu.CompilerParams(dimension_semantics=sem, vmem_limit_bytes=VMEM_LIMIT)


def _softplus(x):
    return jnp.maximum(x, 0.0) + jnp.log1p(jnp.exp(-jnp.abs(x)))


def _sigmoid(x):
    return 1.0 / (1.0 + jnp.exp(-x))


def _silu(x):
    return x * _sigmoid(x)


def _gelu_tanh(x):
    return 0.5 * x * (1.0 + jnp.tanh(math.sqrt(2.0 / math.pi) * (x + 0.044715 * (x * x * x))))


def _ln_math(x, g, b):
    mu = jnp.mean(x, axis=-1, keepdims=True)
    xc = x - mu
    var = jnp.mean(xc * xc, axis=-1, keepdims=True)
    return xc * lax.rsqrt(var + LN_EPS) * g + b


def _ln_kernel(a_ref, g_ref, b_ref, o_ref):
    o_ref[...] = _ln_math(a_ref[...], g_ref[...], b_ref[...])


def _res_ln_kernel(a_ref, r_ref, g_ref, b_ref, o_ref):
    o_ref[...] = _ln_math(DN_ALPHA * a_ref[...] + r_ref[...], g_ref[...], b_ref[...])


def _layer_norm(a, g, b, res=None):
    t, d = a.shape
    row = pl.BlockSpec((ROW_TILE, d), lambda i: (i, 0))
    vec = pl.BlockSpec((1, d), lambda i: (0, 0))
    if res is None:
        kern, args, specs = _ln_kernel, (a,), [row]
    else:
        kern, args, specs = _res_ln_kernel, (a, res), [row, row]
    return pl.pallas_call(
        kern, out_shape=jax.ShapeDtypeStruct((t, d), F32), grid=(t // ROW_TILE,),
        in_specs=specs + [vec, vec], out_specs=row, compiler_params=_cparams(("parallel",)),
        name="layer_norm",
    )(*args, g.reshape(1, d), b.reshape(1, d))


def _inproj_kernel(h_ref, *refs):
    n = len(PROJ_SEGS)
    hb = h_ref[...].astype(BF)
    for w_ref, o_ref in zip(refs[:n], refs[n:]):
        o_ref[...] = jnp.dot(hb, w_ref[...], preferred_element_type=F32)


def _in_proj(h, ws):
    t = h.shape[0]
    return pl.pallas_call(
        _inproj_kernel,
        out_shape=tuple(jax.ShapeDtypeStruct((t, w), F32) for w in PROJ_SEGS),
        grid=(t // ROW_TILE,),
        in_specs=[pl.BlockSpec((ROW_TILE, D_MODEL), lambda i: (i, 0))]
        + [pl.BlockSpec((D_MODEL, w), lambda i: (0, 0)) for w in PROJ_SEGS],
        out_specs=tuple(pl.BlockSpec((ROW_TILE, w), lambda i: (i, 0)) for w in PROJ_SEGS),
        compiler_params=_cparams(("parallel",)),
        name="in_proj",
    )(h, *ws)


def _split_w_in(w_in):
    w = w_in.astype(BF)
    c = [0]
    for width in (2 * W_LRU, SSM_D_INNER, SSM_CONV_CH, SSM_HEADS, ATTN_Q_W, ATTN_KV_W):
        c.append(c[-1] + width)
    segs = [w[:, c[i]:c[i + 1]] for i in range(6)]
    segs[3] = jnp.pad(segs[3], ((0, 0), (0, LANES - SSM_HEADS)))
    return tuple(segs)


def _shift_rows(x, s, fill):
    n = x.shape[0]
    if s % SUBLANES == 0:
        return jnp.concatenate([jnp.full((s,) + x.shape[1:], fill, x.dtype), x[:n - s]], axis=0)
    rolled = pltpu.roll(x, s, axis=0)
    row = lax.broadcasted_iota(jnp.int32, x.shape, 0)
    return jnp.where(row < s, fill, rolled)


def _causal_conv(x, halo, w_ref, b_ref):
    n = x.shape[0]
    xp = jnp.concatenate([halo, x], axis=0)
    out = b_ref[...] + w_ref[CONV_K - 1:CONV_K, :] * x
    for s in range(1, CONV_K):
        out = out + w_ref[CONV_K - 1 - s:CONV_K - s, :] * pltpu.roll(xp, s, axis=0)[HALO:HALO + n]
    return out


def _rglru_kernel(rg_ref, cw_ref, cb_ref, wa_ref, ba_ref, wx_ref, bx_ref, lam_ref, o_ref, halo_ref, carry_ref):
    n = rg_ref.shape[0]

    @pl.when(pl.program_id(1) == 0)
    def _():
        halo_ref[...] = jnp.zeros_like(halo_ref)
        carry_ref[...] = jnp.zeros_like(carry_ref)

    x = rg_ref[:, 0:W_LRU]
    xc = _causal_conv(x, halo_ref[...], cw_ref, cb_ref)
    halo_ref[...] = x[n - HALO:n]
    xb = xc.astype(BF)
    r = _sigmoid(jnp.dot(xb, wa_ref[...], preferred_element_type=F32) + ba_ref[...])
    i = _sigmoid(jnp.dot(xb, wx_ref[...], preferred_element_type=F32) + bx_ref[...])
    log_a = (-LRU_C * r) * _softplus(-lam_ref[...])
    a = jnp.exp(log_a)
    u = jnp.sqrt(-jnp.tanh(log_a) * (a * a + 1.0)) * (i * xc)
    k = 1
    while k < n:
        u = a * _shift_rows(u, k, 0.0) + u
        a = a * _shift_rows(a, k, 1.0)
        k *= 2
    h = u + a * carry_ref[0:1, :]
    carry_ref[...] = jnp.broadcast_to(h[n - 1:n, :], carry_ref.shape)
    o_ref[...] = _gelu_tanh(rg_ref[:, W_LRU:2 * W_LRU]) * h


def _rg_lru(rg, seq, conv_w, conv_b, wa, ba, wx, bx, lam):
    t = rg.shape[0]
    nb = seq // ROW_TILE
    row = lambda b, j: (b * nb + j, 0)
    const = lambda b, j: (0, 0)
    vec = pl.BlockSpec((1, W_LRU), const)
    return pl.pallas_call(
        _rglru_kernel,
        out_shape=jax.ShapeDtypeStruct((t, W_LRU), F32),
        grid=(t // seq, nb),
        in_specs=[pl.BlockSpec((ROW_TILE, 2 * W_LRU), row),
                  pl.BlockSpec((CONV_K, W_LRU), const), vec,
                  pl.BlockSpec((W_LRU, W_LRU), const), vec,
                  pl.BlockSpec((W_LRU, W_LRU), const), vec, vec],
        out_specs=pl.BlockSpec((ROW_TILE, W_LRU), row),
        scratch_shapes=[pltpu.VMEM((HALO, W_LRU), F32), pltpu.VMEM((SUBLANES, W_LRU), F32)],
        compiler_params=_cparams(("parallel", "arbitrary")),
        name="rg_lru",
    )(rg, conv_w, conv_b.reshape(1, -1), wa, ba.reshape(1, -1), wx, bx.reshape(1, -1), lam.reshape(1, -1))


def _block_diag(w):
    nb, c, _ = w.shape
    eye = jnp.eye(nb, dtype=w.dtype)
    return (eye[:, None, :, None] * w[:, :, None, :]).reshape(nb * c, nb * c).astype(BF)


def _ssd_kernel(z_ref, xbc_ref, dt_ref, cw_ref, cb_ref, dtb_ref, alog_ref, dskip_ref, ng_ref, expand_ref,
                o_ref, halo_ref, state_ref):
    L = SSM_CHUNK

    @pl.when(pl.program_id(1) == 0)
    def _():
        halo_ref[...] = jnp.zeros_like(halo_ref)
        state_ref[...] = jnp.zeros_like(state_ref)

    xbc = xbc_ref[...]
    conv = _silu(_causal_conv(xbc, halo_ref[...], cw_ref, cb_ref))
    halo_ref[...] = xbc[L - HALO:L]
    xs = conv[:, 0:SSM_D_INNER]
    bm = conv[:, SSM_D_INNER:SSM_D_INNER + SSM_GROUPS * SSM_STATE]
    cm = conv[:, SSM_D_INNER + SSM_GROUPS * SSM_STATE:]

    dt = _softplus(dt_ref[...] + dtb_ref[...])
    da = dt * (-jnp.exp(alog_ref[...]))
    row = lax.broadcasted_iota(jnp.int32, (L, L), 0)
    col = lax.broadcasted_iota(jnp.int32, (L, L), 1)
    causal = col <= row
    a_cs = jnp.dot(causal.astype(F32), da, precision=HI, preferred_element_type=F32)
    a_cs_t = a_cs.T
    expand = expand_ref[...]
    dt_full = jnp.dot(dt, expand, precision=HI, preferred_element_type=F32)
    acs_full = jnp.dot(a_cs, expand, precision=HI, preferred_element_type=F32)
    alast_full = acs_full[L - 1:L, :]
    xdt = xs * dt_full
    xdt_b = xdt.astype(BF)
    xst_b = (xdt * jnp.exp(alast_full - acs_full)).astype(BF)
    left = lax.broadcasted_iota(jnp.int32, (L, LANES), 1) < SSM_HEAD_DIM

    y_parts = []
    for g in range(SSM_GROUPS):
        cg = cm[:, g * SSM_STATE:(g + 1) * SSM_STATE].astype(BF)
        bg = bm[:, g * SSM_STATE:(g + 1) * SSM_STATE].astype(BF)
        cb = lax.dot_general(cg, bg, (((1,), (1,)), ((), ())), preferred_element_type=F32)
        st = state_ref[:, g * SSD_GROUP_W:(g + 1) * SSD_GROUP_W]
        y_off = jnp.dot(cg, st.astype(BF), preferred_element_type=F32)
        for j in range(SSD_GROUP_W // LANES):
            h0 = g * (SSM_HEADS // SSM_GROUPS) + 2 * j
            ms = []
            for h in (h0, h0 + 1):
                seg = a_cs[:, h:h + 1] - a_cs_t[h:h + 1, :]
                ms.append((cb * jnp.exp(jnp.where(causal, seg, -jnp.inf))).astype(BF))
            c0 = h0 * SSM_HEAD_DIM
            x2 = xdt_b[:, c0:c0 + LANES]
            zero = jnp.zeros_like(x2)
            xblk = jnp.concatenate([jnp.where(left, x2, zero), jnp.where(left, zero, x2)], axis=0)
            y_parts.append(jnp.dot(jnp.concatenate(ms, axis=1), xblk, preferred_element_type=F32)
                           + y_off[:, j * LANES:(j + 1) * LANES] * jnp.exp(acs_full[:, c0:c0 + LANES]))
        new = lax.dot_general(bg, xst_b[:, g * SSD_GROUP_W:(g + 1) * SSD_GROUP_W], (((0,), (0,)), ((), ())),
                              preferred_element_type=F32)
        state_ref[:, g * SSD_GROUP_W:(g + 1) * SSD_GROUP_W] = (
            st * jnp.exp(alast_full[:, g * SSD_GROUP_W:(g + 1) * SSD_GROUP_W]) + new)
    y = jnp.concatenate(y_parts, axis=1) + dskip_ref[...] * xs
    y = y * _silu(z_ref[...])
    outs = []
    for g in range(SSM_GROUPS):
        yg = y[:, g * SSD_GROUP_W:(g + 1) * SSD_GROUP_W]
        outs.append(yg * lax.rsqrt(jnp.mean(yg * yg, axis=-1, keepdims=True) + LN_EPS))
    o_ref[...] = jnp.concatenate(outs, axis=1) * ng_ref[...]


def _ssd(z, xbc, dt, seq, conv_w, conv_b, dt_bias, a_log, d_skip, norm_g):
    t = z.shape[0]
    nc = seq // SSM_CHUNK
    row = lambda b, c: (b * nc + c, 0)
    const = lambda b, c: (0, 0)
    pad = LANES - SSM_HEADS
    expand = (jnp.arange(LANES)[:, None] == (jnp.arange(SSM_D_INNER)[None, :] // SSM_HEAD_DIM)).astype(F32)
    return pl.pallas_call(
        _ssd_kernel,
        out_shape=jax.ShapeDtypeStruct((t, SSM_D_INNER), F32),
        grid=(t // seq, nc),
        in_specs=[pl.BlockSpec((SSM_CHUNK, SSM_D_INNER), row),
                  pl.BlockSpec((SSM_CHUNK, SSM_CONV_CH), row),
                  pl.BlockSpec((SSM_CHUNK, LANES), row),
                  pl.BlockSpec((CONV_K, SSM_CONV_CH), const),
                  pl.BlockSpec((1, SSM_CONV_CH), const),
                  pl.BlockSpec((1, LANES), const),
                  pl.BlockSpec((1, LANES), const),
                  pl.BlockSpec((1, SSM_D_INNER), const),
                  pl.BlockSpec((1, SSM_D_INNER), const),
                  pl.BlockSpec((LANES, SSM_D_INNER), const)],
        out_specs=pl.BlockSpec((SSM_CHUNK, SSM_D_INNER), row),
        scratch_shapes=[pltpu.VMEM((HALO, SSM_CONV_CH), F32), pltpu.VMEM((SSM_STATE, SSM_D_INNER), F32)],
        compiler_params=_cparams(("parallel", "arbitrary")),
        name="ssd",
    )(z, xbc, dt, conv_w, conv_b.reshape(1, -1), jnp.pad(dt_bias, (0, pad)).reshape(1, -1),
      jnp.pad(a_log, (0, pad)).reshape(1, -1), jnp.repeat(d_skip, SSM_HEAD_DIM).reshape(1, -1),
      norm_g.reshape(1, -1), expand)


def _swa_kernel(sink_ref, q_ref, kv_ref, kvp_ref, o_ref):
    L = ATTN_BLOCK
    hd = ATTN_HEAD_DIM
    first = pl.program_id(1) == 0
    kv = jnp.concatenate([kvp_ref[...], kv_ref[...]], axis=0)
    kk = kv[:, 0:LANES]
    vv = kv[:, LANES:2 * LANES].astype(BF)
    lane_k = lax.broadcasted_iota(jnp.int32, (2 * L, LANES), 1)
    qi = lax.broadcasted_iota(jnp.int32, (2 * L, 2 * L), 0) % L
    kj = lax.broadcasted_iota(jnp.int32, (2 * L, 2 * L), 1)
    rel = qi + L - kj
    valid = (rel >= 0) & (rel < L) & (jnp.logical_not(first) | (kj >= L))
    top = lax.broadcasted_iota(jnp.int32, (2 * L, 1), 0) < L
    lane_o = lax.broadcasted_iota(jnp.int32, (L, LANES), 1)
    scale = hd ** -0.5
    for j in range(ATTN_Q_HEADS // 2):
        g = (2 * j) // ATTN_REP
        q2 = q_ref[:, j * LANES:(j + 1) * LANES]
        q2r = pltpu.roll(q2, hd, axis=1)
        kg = jnp.where((lane_k >= g * hd) & (lane_k < (g + 1) * hd), kk, 0.0).astype(BF)
        qa, qb = (q2, q2r) if g == 0 else (q2r, q2)
        qs = jnp.concatenate([qa, qb], axis=0).astype(BF)
        logits = lax.dot_general(qs, kg, (((1,), (1,)), ((), ())), preferred_element_type=F32) * scale
        logits = jnp.where(valid, logits, -jnp.inf)
        sink = jnp.where(top, sink_ref[2 * j], sink_ref[2 * j + 1])
        m = jnp.maximum(jnp.max(logits, axis=-1, keepdims=True), sink)
        p = jnp.exp(logits - m)
        probs = p / (jnp.sum(p, axis=-1, keepdims=True) + jnp.exp(sink - m))
        o = jnp.dot(probs.astype(BF), vv, preferred_element_type=F32)
        oa, ob = o[0:L], o[L:2 * L]
        if g == 0:
            out2 = jnp.where(lane_o < hd, oa, pltpu.roll(ob, hd, axis=1))
        else:
            out2 = jnp.where(lane_o < hd, pltpu.roll(oa, hd, axis=1), ob)
        o_ref[:, j * LANES:(j + 1) * LANES] = out2


def _swa(q, kv, seq, sinks):
    t = q.shape[0]
    nb = seq // ATTN_BLOCK
    return pl.pallas_call(
        _swa_kernel,
        out_shape=jax.ShapeDtypeStruct((t, ATTN_Q_W), F32),
        grid=(t // seq, nb),
        in_specs=[pl.BlockSpec(memory_space=pltpu.SMEM),
                  pl.BlockSpec((ATTN_BLOCK, ATTN_Q_W), lambda b, n: (b * nb + n, 0)),
                  pl.BlockSpec((ATTN_BLOCK, ATTN_KV_W), lambda b, n: (b * nb + n, 0)),
                  pl.BlockSpec((ATTN_BLOCK, ATTN_KV_W), lambda b, n: (b * nb + jnp.maximum(n - 1, 0), 0))],
        out_specs=pl.BlockSpec((ATTN_BLOCK, ATTN_Q_W), lambda b, n: (b * nb + n, 0)),
        compiler_params=_cparams(("parallel", "parallel")),
        name="swa",
    )(sinks, q, kv, kv)


def _outproj_kernel(ya_ref, yb_ref, yc_ref, h_ref, wa_ref, wb_ref, wc_ref, g_ref, b_ref, o_ref):
    mix = (jnp.dot(ya_ref[...].astype(BF), wa_ref[...], preferred_element_type=F32)
           + jnp.dot(yb_ref[...].astype(BF), wb_ref[...], preferred_element_type=F32)
           + jnp.dot(yc_ref[...].astype(BF), wc_ref[...], preferred_element_type=F32))
    o_ref[...] = _ln_math(DN_ALPHA * h_ref[...] + mix, g_ref[...], b_ref[...])


def _out_proj(ya, yb, yc, h, w_out, g, b):
    t = h.shape[0]
    w = w_out.astype(BF)
    row = lambda i: (i, 0)
    const = lambda i: (0, 0)
    c1, c2 = W_LRU, W_LRU + SSM_D_INNER
    return pl.pallas_call(
        _outproj_kernel,
        out_shape=jax.ShapeDtypeStruct((t, D_MODEL), F32),
        grid=(t // ROW_TILE,),
        in_specs=[pl.BlockSpec((ROW_TILE, W_LRU), row), pl.BlockSpec((ROW_TILE, SSM_D_INNER), row),
                  pl.BlockSpec((ROW_TILE, ATTN_Q_W), row), pl.BlockSpec((ROW_TILE, D_MODEL), row),
                  pl.BlockSpec((W_LRU, D_MODEL), const), pl.BlockSpec((SSM_D_INNER, D_MODEL), const),
                  pl.BlockSpec((ATTN_Q_W, D_MODEL), const),
                  pl.BlockSpec((1, D_MODEL), const), pl.BlockSpec((1, D_MODEL), const)],
        out_specs=pl.BlockSpec((ROW_TILE, D_MODEL), row),
        compiler_params=_cparams(("parallel",)),
        name="out_proj",
    )(ya, yb, yc, h, w[0:c1], w[c1:c2], w[c2:MIX_WIDTH], g.reshape(1, -1), b.reshape(1, -1))


def _gather_group(idx_ref, tbl_ref, t, g):
    rs = ROW_WORDS_SUBLANES
    pieces = []
    for q in range(GROUP_PAIRS // 2):
        p = g * GROUP_PAIRS + 2 * q
        if p % SUBLANES == 0:
            window = idx_ref.at[t, pl.ds(p, SUBLANES)]
        ra = pl.multiple_of(window[p % SUBLANES], rs)
        rb = pl.multiple_of(window[p % SUBLANES + 1], rs)
        pieces.append(jnp.concatenate([tbl_ref[pl.ds(ra, rs), :], tbl_ref[pl.ds(rb, rs), :]], axis=0))
    return pltpu.bitcast(jnp.concatenate(pieces, axis=0), BF)


def _chunk_of_column(col):
    j = col % PAIR_LANES
    return j // 2 + ROW_WORDS_SUBLANES * (j % 2)


def _split_hi_lo(x):
    hi = x.astype(BF)
    lo = (x - hi.astype(F32)).astype(BF)
    return hi, lo


def _peer_route_in_kernel(hn_ref, wqt_ref, keys_ref, idx_ref, x_ref, gate_ref, tbl_ref,
                          idx_o_ref, gate_o_ref, o_ref, m_ref):
    n = x_ref.shape[0]
    row = lax.broadcasted_iota(jnp.int32, (2 * SUBLANES, GROUP_K), 0) % SUBLANES
    keep = row == _chunk_of_column(lax.broadcasted_iota(jnp.int32, (2 * SUBLANES, GROUP_K), 1))
    side = _route_steps(hn_ref, wqt_ref, keys_ref, idx_o_ref, gate_o_ref)
    for t in range(n):
        x_row = x_ref[pl.ds(t, 1), :]
        x8 = jnp.concatenate([x_row[:, c * LANES:(c + 1) * LANES] for c in range(TOKEN_SUBLANES)], axis=0)
        x_hi, x_lo = _split_hi_lo(x8)
        x16 = jnp.concatenate([x_hi, x_lo], axis=0)
        for g in range(PEER_PAIRS // GROUP_PAIRS):
            r = lax.dot_general(x16, _gather_group(idx_ref, tbl_ref, t, g), (((1,), (1,)), ((), ())),
                                preferred_element_type=F32)
            m_ref[pl.ds(t, 1), g * GROUP_K:(g + 1) * GROUP_K] = jnp.sum(jnp.where(keep, r, 0.0), axis=0,
                                                                        keepdims=True)
        for _ in range(-(-ROUTE_STEPS // n)):
            next(side, None)
    for _ in side:
        pass
    wide = PEER_PAIRS * PAIR_LANES
    fold = (lax.broadcasted_iota(jnp.int32, (wide, PEER_PAIRS), 0) // PAIR_LANES
            == lax.broadcasted_iota(jnp.int32, (wide, PEER_PAIRS), 1)).astype(BF)
    m_hi, m_lo = _split_hi_lo(m_ref[...])
    a = jnp.dot(m_hi, fold, preferred_element_type=F32) + jnp.dot(m_lo, fold, preferred_element_type=F32)
    o_ref[...] = gate_ref[...] * (0.5 * a * (1.0 + lax.erf(a * (2.0 ** -0.5))))


def _peer_route_in(h, wqt, keys, tbl, seq):
    t = h.shape[0]
    ft = FUSED_TOKENS
    nb, nk = t // seq, seq // ft
    padded = (nb + 1) * seq
    route_src = lambda s, k: (jnp.minimum(s, nb - 1) * nk + k, 0)
    route_dst = lambda s, k: (s * nk + k, 0)
    gather_src = lambda s, k: (jnp.maximum(s - 1, 0) * nk + k, 0)
    gather_slot = lambda s, k: (((s + nb) % (nb + 1)) * nk + k, 0)
    const = lambda s, k: (0, 0)
    pairs_block = lambda imap, **kw: pl.BlockSpec((ft, PEER_PAIRS), imap, **kw)
    rows, gate, w = pl.pallas_call(
        _peer_route_in_kernel,
        out_shape=(jax.ShapeDtypeStruct((padded, PEER_PAIRS), jnp.int32),
                   jax.ShapeDtypeStruct((padded, PEER_PAIRS), F32),
                   jax.ShapeDtypeStruct((padded, PEER_PAIRS), F32)),
        grid=(nb + 1, nk),
        in_specs=[
            pl.BlockSpec((ft, D_MODEL), route_src),
            pl.BlockSpec((D_MODEL, D_MODEL), const),
            pl.BlockSpec((2, PEER_NKEYS, PEER_DKEY // 2), lambda s, k: (0, 0, 0)),
            pairs_block(gather_slot, memory_space=pltpu.SMEM),
            pl.BlockSpec((ft, D_MODEL), gather_src),
            pairs_block(gather_slot),
            pl.BlockSpec(memory_space=pltpu.VMEM),
        ],
        out_specs=(pairs_block(route_dst), pairs_block(route_dst), pairs_block(gather_slot)),
        scratch_shapes=[pltpu.VMEM((ft, PEER_PAIRS * PAIR_LANES), F32)],
        input_output_aliases={3: 0, 5: 1},
        compiler_params=pltpu.CompilerParams(dimension_semantics=("arbitrary", "arbitrary"),
                                             vmem_limit_bytes=PEER_VMEM_LIMIT),
        name="peer_route_in",
    )(h, wqt, keys, jnp.zeros((padded, PEER_PAIRS), jnp.int32), h, jnp.zeros((padded, PEER_PAIRS), F32), tbl)
    return rows, w


def _peer_out_kernel(idx_ref, w_ref, tbl_ref, h_ref, g_ref, b_ref, o_ref, wide_ref, ffn_ref):
    n = w_ref.shape[0]
    wide = PEER_PAIRS * PAIR_LANES
    rep = (lax.broadcasted_iota(jnp.int32, (PEER_PAIRS, wide), 1) // PAIR_LANES
           == lax.broadcasted_iota(jnp.int32, (PEER_PAIRS, wide), 0)).astype(BF)
    for k, wv in enumerate(_split_hi_lo(w_ref[...])):
        wide_ref[k] = jnp.dot(wv, rep, preferred_element_type=F32)
    keep = (lax.broadcasted_iota(jnp.int32, (SUBLANES, GROUP_K), 0)
            == _chunk_of_column(lax.broadcasted_iota(jnp.int32, (SUBLANES, GROUP_K), 1)))
    for t in range(n):
        acc = jnp.zeros((2 * SUBLANES, LANES), F32)
        for g in range(PEER_PAIRS // GROUP_PAIRS):
            cols = slice(g * GROUP_K, (g + 1) * GROUP_K)
            lhs = jnp.concatenate(
                [jnp.where(keep, jnp.broadcast_to(wide_ref[k, pl.ds(t, 1), cols], (SUBLANES, GROUP_K)), 0.0)
                 for k in range(2)], axis=0).astype(BF)
            acc = acc + jnp.dot(lhs, _gather_group(idx_ref, tbl_ref, t, g), preferred_element_type=F32)
        out8 = acc[0:SUBLANES] + acc[SUBLANES:]
        for c in range(TOKEN_SUBLANES):
            ffn_ref[pl.ds(t, 1), c * LANES:(c + 1) * LANES] = out8[c:c + 1, :]
    o_ref[...] = _ln_math(DN_ALPHA * h_ref[...] + ffn_ref[...], g_ref[...], b_ref[...])


def _peer_out(rows, w, tbl, h, g, b):
    t = h.shape[0]
    tt = GATHER_TOKENS
    row = pl.BlockSpec((tt, D_MODEL), lambda i: (i, 0))
    vec = pl.BlockSpec((1, D_MODEL), lambda i: (0, 0))
    return pl.pallas_call(
        _peer_out_kernel,
        out_shape=jax.ShapeDtypeStruct((t, D_MODEL), F32),
        grid=(t // tt,),
        in_specs=[
            pl.BlockSpec((tt, PEER_PAIRS), lambda i: (i, 0), memory_space=pltpu.SMEM),
            pl.BlockSpec((tt, PEER_PAIRS), lambda i: (i, 0)),
            pl.BlockSpec(memory_space=pltpu.VMEM),
            row, vec, vec,
        ],
        out_specs=row,
        scratch_shapes=[pltpu.VMEM((2, tt, PEER_PAIRS * PAIR_LANES), F32), pltpu.VMEM((tt, D_MODEL), F32)],
        compiler_params=pltpu.CompilerParams(vmem_limit_bytes=PEER_VMEM_LIMIT),
        name="peer_out",
    )(rows, w, tbl, h, g.reshape(1, D_MODEL), b.reshape(1, D_MODEL))


def _extract_topk_steps(s, pos, k, payload=None):
    big = jnp.asarray(2 ** 30, pos.dtype)
    vals, poss, pays = [], [], []
    for _ in range(k):
        m = jnp.max(s, axis=0, keepdims=True)
        j = jnp.min(jnp.where(s == m, pos, big), axis=0, keepdims=True)
        sel = pos == j
        vals.append(m)
        poss.append(j)
        if payload is not None:
            pays.append(jnp.sum(jnp.where(sel, payload, jnp.zeros_like(payload)), axis=0, keepdims=True))
        s = jnp.where(sel, -jnp.inf, s)
        yield
    out = [jnp.concatenate(vals, axis=0), jnp.concatenate(poss, axis=0)]
    if payload is not None:
        out.append(jnp.concatenate(pays, axis=0))
    return out


ROUTE_STEPS = PEER_HEADS * (3 * PEER_TOPK + 1)


def _route_steps(h_ref, wqt_ref, keys_ref, idx_ref, gate_ref):
    half_dim = PEER_DKEY // 2
    hb = h_ref[...].astype(BF)
    n = hb.shape[0]
    key_iota = lax.broadcasted_iota(jnp.int32, (PEER_NKEYS, n), 0).astype(F32)
    iota16 = lax.broadcasted_iota(jnp.int32, (PEER_TOPK, n), 0).astype(F32)
    iota8 = lax.broadcasted_iota(jnp.int32, (SUBLANES, n), 0).astype(F32)
    for hd in range(PEER_HEADS):
        qt = lax.dot_general(wqt_ref[hd * PEER_DKEY:(hd + 1) * PEER_DKEY, :], hb,
                             (((1,), (1,)), ((), ())), preferred_element_type=F32)
        tv, ti = [], []
        for half in range(2):
            s = jnp.dot(keys_ref[half], qt[half * half_dim:(half + 1) * half_dim, :], preferred_element_type=F32)
            v, i = yield from _extract_topk_steps(s, key_iota, PEER_TOPK)
            tv.append(v)
            ti.append(i)
        cs = [tv[0][0:1, :] + tv[1]]
        cp = [iota16]
        ce = [ti[0][0:1, :] * PEER_NKEYS + ti[1]]
        for a in range(1, SUBLANES):
            cs.append(tv[0][a:a + 1, :] + tv[1][0:SUBLANES, :])
            cp.append(iota8 + a * PEER_TOPK)
            ce.append(ti[0][a:a + 1, :] * PEER_NKEYS + ti[1][0:SUBLANES, :])
        cs.append(tv[0][SUBLANES:, :] + tv[1][0:1, :])
        cp.append((iota8 + SUBLANES) * PEER_TOPK)
        ce.append(ti[0][SUBLANES:, :] * PEER_NKEYS + ti[1][0:1, :])
        best_s, _, best_e = yield from _extract_topk_steps(
            jnp.concatenate(cs, axis=0), jnp.concatenate(cp, axis=0), PEER_TOPK, payload=jnp.concatenate(ce, axis=0))
        ex = jnp.exp(best_s - best_s[0:1, :])
        gate = ex / jnp.sum(ex, axis=0, keepdims=True)
        cols = slice(hd * PEER_TOPK, (hd + 1) * PEER_TOPK)
        idx_ref[:, cols] = (best_e * ROW_WORDS_SUBLANES).astype(jnp.int32).T
        gate_ref[:, cols] = gate.T
        yield


def _pack_table(u):
    ub = u.astype(jnp.bfloat16)
    half = D_MODEL // 2
    lo = lax.bitcast_convert_type(ub[:, :half], jnp.uint16).astype(jnp.uint32)
    hi = lax.bitcast_convert_type(ub[:, half:], jnp.uint16).astype(jnp.uint32)
    return (lo | (hi << 16)).reshape(u.shape[0] * ROW_WORDS_SUBLANES, LANES)


def _peer_sublayer(h, seq, wq, keys, u, v, ln_g, ln_b):
    rows, w = _peer_route_in(h, wq.T.astype(BF), keys, _pack_table(u), seq)
    return _peer_out(rows, w, _pack_table(v), h, ln_g, ln_b)


def kernel(x, emb_ln_g, emb_ln_b, w_in, rg_conv_w, rg_conv_b, rg_wa, rg_ba, rg_wx, rg_bx, rg_lambda, ssm_conv_w, ssm_conv_b, ssm_dt_bias, ssm_a_log, ssm_d, ssm_norm_g, attn_sinks, w_out, ln1_g, ln1_b, peer_wq, peer_keys, peer_u, peer_v, ln2_g, ln2_b):
    bsz, seq, d = x.shape
    assert d == D_MODEL and seq % ROW_TILE == 0 and seq % FUSED_TOKENS == 0
    h = _layer_norm(x.reshape(bsz * seq, d), emb_ln_g, emb_ln_b)
    for l in range(DEPTH):
        rg, z, xbc, dt, q, kv = _in_proj(h, _split_w_in(w_in[l]))
        y_a = _rg_lru(rg, seq, rg_conv_w[l], rg_conv_b[l], _block_diag(rg_wa[l]), rg_ba[l],
                      _block_diag(rg_wx[l]), rg_bx[l], rg_lambda[l])
        y_b = _ssd(z, xbc, dt, seq, ssm_conv_w[l], ssm_conv_b[l], ssm_dt_bias[l], ssm_a_log[l], ssm_d[l],
                   ssm_norm_g[l])
        y_c = _swa(q, kv, seq, attn_sinks[l])
        h = _out_proj(y_a, y_b, y_c, h, w_out[l], ln1_g[l], ln1_b[l])
        h = _peer_sublayer(h, seq, peer_wq[l], peer_keys[l], peer_u[l], peer_v[l], ln2_g[l], ln2_b[l])
    return h.reshape(bsz, seq, d)
```

```python
import math

import jax
import jax.numpy as jnp
from jax import lax
from jax.experimental import pallas as pl
from jax.experimental.pallas import tpu as pltpu

D_MODEL = 1024
DEPTH = 2

W_LRU = D_MODEL // 2
LRU_BLOCKS = 8
LRU_C = 8.0
CONV_K = 4

SSM_HEAD_DIM = 64
SSM_D_INNER = D_MODEL
SSM_HEADS = SSM_D_INNER // SSM_HEAD_DIM
SSM_GROUPS = 2
SSM_STATE = 128
SSM_CHUNK = 128
SSM_CONV_CH = SSM_D_INNER + 2 * SSM_GROUPS * SSM_STATE
SSD_GROUP_W = SSM_D_INNER // SSM_GROUPS

ATTN_HEAD_DIM = 64
ATTN_Q_HEADS = (D_MODEL // 2) // ATTN_HEAD_DIM
ATTN_KV_HEADS = 2
ATTN_REP = ATTN_Q_HEADS // ATTN_KV_HEADS
ATTN_BLOCK = 128
ATTN_Q_W = ATTN_Q_HEADS * ATTN_HEAD_DIM
ATTN_KV_W = 2 * ATTN_KV_HEADS * ATTN_HEAD_DIM

MIX_WIDTH = W_LRU + SSM_D_INNER + ATTN_Q_W

PEER_HEADS = 8
PEER_NKEYS = 128
PEER_DKEY = 128
PEER_TOPK = 16
PEER_PAIRS = PEER_HEADS * PEER_TOPK

DN_ALPHA = (2 * DEPTH) ** 0.25
LN_EPS = 1e-5

SUBLANES = 8
LANES = 128
HALO = SUBLANES
ROW_WORDS_SUBLANES = D_MODEL // 2 // LANES
TOKEN_SUBLANES = D_MODEL // LANES

ROW_TILE = 512
PAIR_LANES = 2 * ROW_WORDS_SUBLANES
GROUP_PAIRS = 32
GROUP_K = GROUP_PAIRS * PAIR_LANES
GATHER_TOKENS = 128
ROUTE_TOKENS = 256
VMEM_LIMIT = 48 * 1024 * 1024
PEER_VMEM_LIMIT = 56 * 1024 * 1024

BF = jnp.bfloat16
F32 = jnp.float32
HI = lax.Precision.HIGHEST

PROJ_SEGS = (2 * W_LRU, SSM_D_INNER, SSM_CONV_CH, LANES, ATTN_Q_W, ATTN_KV_W)


def _cparams(sem):
    return pltpu.CompilerParams(dimension_semantics=sem, vmem_limit_bytes=VMEM_LIMIT)


def _softplus(x):
    return jnp.maximum(x, 0.0) + jnp.log1p(jnp.exp(-jnp.abs(x)))


def _sigmoid(x):
    return 1.0 / (1.0 + jnp.exp(-x))


def _silu(x):
    return x * _sigmoid(x)


def _gelu_tanh(x):
    return 0.5 * x * (1.0 + jnp.tanh(math.sqrt(2.0 / math.pi) * (x + 0.044715 * (x * x * x))))


def _ln_math(x, g, b):
    mu = jnp.mean(x, axis=-1, keepdims=True)
    xc = x - mu
    var = jnp.mean(xc * xc, axis=-1, keepdims=True)
    return xc * lax.rsqrt(var + LN_EPS) * g + b


def _ln_kernel(a_ref, g_ref, b_ref, o_ref):
    o_ref[...] = _ln_math(a_ref[...], g_ref[...], b_ref[...])


def _res_ln_kernel(a_ref, r_ref, g_ref, b_ref, o_ref):
    o_ref[...] = _ln_math(DN_ALPHA * a_ref[...] + r_ref[...], g_ref[...], b_ref[...])


def _layer_norm(a, g, b, res=None):
    t, d = a.shape
    row = pl.BlockSpec((ROW_TILE, d), lambda i: (i, 0))
    vec = pl.BlockSpec((1, d), lambda i: (0, 0))
    if res is None:
        kern, args, specs = _ln_kernel, (a,), [row]
    else:
        kern, args, specs = _res_ln_kernel, (a, res), [row, row]
    return pl.pallas_call(
        kern, out_shape=jax.ShapeDtypeStruct((t, d), F32), grid=(t // ROW_TILE,),
        in_specs=specs + [vec, vec], out_specs=row, compiler_params=_cparams(("parallel",)),
        name="layer_norm",
    )(*args, g.reshape(1, d), b.reshape(1, d))


def _inproj_kernel(h_ref, *refs):
    n = len(PROJ_SEGS)
    hb = h_ref[...].astype(BF)
    for w_ref, o_ref in zip(refs[:n], refs[n:]):
        o_ref[...] = jnp.dot(hb, w_ref[...], preferred_element_type=F32)


def _in_proj(h, ws):
    t = h.shape[0]
    return pl.pallas_call(
        _inproj_kernel,
        out_shape=tuple(jax.ShapeDtypeStruct((t, w), F32) for w in PROJ_SEGS),
        grid=(t // ROW_TILE,),
        in_specs=[pl.BlockSpec((ROW_TILE, D_MODEL), lambda i: (i, 0))]
        + [pl.BlockSpec((D_MODEL, w), lambda i: (0, 0)) for w in PROJ_SEGS],
        out_specs=tuple(pl.BlockSpec((ROW_TILE, w), lambda i: (i, 0)) for w in PROJ_SEGS),
        compiler_params=_cparams(("parallel",)),
        name="in_proj",
    )(h, *ws)


def _split_w_in(w_in):
    w = w_in.astype(BF)
    c = [0]
    for width in (2 * W_LRU, SSM_D_INNER, SSM_CONV_CH, SSM_HEADS, ATTN_Q_W, ATTN_KV_W):
        c.append(c[-1] + width)
    segs = [w[:, c[i]:c[i + 1]] for i in range(6)]
    segs[3] = jnp.pad(segs[3], ((0, 0), (0, LANES - SSM_HEADS)))
    return tuple(segs)


def _shift_rows(x, s, fill):
    n = x.shape[0]
    if s % SUBLANES == 0:
        return jnp.concatenate([jnp.full((s,) + x.shape[1:], fill, x.dtype), x[:n - s]], axis=0)
    rolled = pltpu.roll(x, s, axis=0)
    row = lax.broadcasted_iota(jnp.int32, x.shape, 0)
    return jnp.where(row < s, fill, rolled)


def _causal_conv(x, halo, w_ref, b_ref):
    n = x.shape[0]
    xp = jnp.concatenate([halo, x], axis=0)
    out = b_ref[...] + w_ref[CONV_K - 1:CONV_K, :] * x
    for s in range(1, CONV_K):
        out = out + w_ref[CONV_K - 1 - s:CONV_K - s, :] * pltpu.roll(xp, s, axis=0)[HALO:HALO + n]
    return out


def _rglru_kernel(rg_ref, cw_ref, cb_ref, wa_ref, ba_ref, wx_ref, bx_ref, lam_ref, o_ref, halo_ref, carry_ref):
    n = rg_ref.shape[0]

    @pl.when(pl.program_id(1) == 0)
    def _():
        halo_ref[...] = jnp.zeros_like(halo_ref)
        carry_ref[...] = jnp.zeros_like(carry_ref)

    x = rg_ref[:, 0:W_LRU]
    xc = _causal_conv(x, halo_ref[...], cw_ref, cb_ref)
    halo_ref[...] = x[n - HALO:n]
    xb = xc.astype(BF)
    r = _sigmoid(jnp.dot(xb, wa_ref[...], preferred_element_type=F32) + ba_ref[...])
    i = _sigmoid(jnp.dot(xb, wx_ref[...], preferred_element_type=F32) + bx_ref[...])
    log_a = (-LRU_C * r) * _softplus(-lam_ref[...])
    a = jnp.exp(log_a)
    u = jnp.sqrt(-jnp.tanh(log_a) * (a * a + 1.0)) * (i * xc)
    k = 1
    while k < n:
        u = a * _shift_rows(u, k, 0.0) + u
        a = a * _shift_rows(a, k, 1.0)
        k *= 2
    h = u + a * carry_ref[0:1, :]
    carry_ref[...] = jnp.broadcast_to(h[n - 1:n, :], carry_ref.shape)
    o_ref[...] = _gelu_tanh(rg_ref[:, W_LRU:2 * W_LRU]) * h


def _rg_lru(rg, seq, conv_w, conv_b, wa, ba, wx, bx, lam):
    t = rg.shape[0]
    nb = seq // ROW_TILE
    row = lambda b, j: (b * nb + j, 0)
    const = lambda b, j: (0, 0)
    vec = pl.BlockSpec((1, W_LRU), const)
    return pl.pallas_call(
        _rglru_kernel,
        out_shape=jax.ShapeDtypeStruct((t, W_LRU), F32),
        grid=(t // seq, nb),
        in_specs=[pl.BlockSpec((ROW_TILE, 2 * W_LRU), row),
                  pl.BlockSpec((CONV_K, W_LRU), const), vec,
                  pl.BlockSpec((W_LRU, W_LRU), const), vec,
                  pl.BlockSpec((W_LRU, W_LRU), const), vec, vec],
        out_specs=pl.BlockSpec((ROW_TILE, W_LRU), row),
        scratch_shapes=[pltpu.VMEM((HALO, W_LRU), F32), pltpu.VMEM((SUBLANES, W_LRU), F32)],
        compiler_params=_cparams(("parallel", "arbitrary")),
        name="rg_lru",
    )(rg, conv_w, conv_b.reshape(1, -1), wa, ba.reshape(1, -1), wx, bx.reshape(1, -1), lam.reshape(1, -1))


def _block_diag(w):
    nb, c, _ = w.shape
    eye = jnp.eye(nb, dtype=w.dtype)
    return (eye[:, None, :, None] * w[:, :, None, :]).reshape(nb * c, nb * c).astype(BF)


def _ssd_kernel(z_ref, xbc_ref, dt_ref, cw_ref, cb_ref, dtb_ref, alog_ref, dskip_ref, ng_ref, expand_ref,
                o_ref, halo_ref, state_ref):
    L = SSM_CHUNK

    @pl.when(pl.program_id(1) == 0)
    def _():
        halo_ref[...] = jnp.zeros_like(halo_ref)
        state_ref[...] = jnp.zeros_like(state_ref)

    xbc = xbc_ref[...]
    conv = _silu(_causal_conv(xbc, halo_ref[...], cw_ref, cb_ref))
    halo_ref[...] = xbc[L - HALO:L]
    xs = conv[:, 0:SSM_D_INNER]
    bm = conv[:, SSM_D_INNER:SSM_D_INNER + SSM_GROUPS * SSM_STATE]
    cm = conv[:, SSM_D_INNER + SSM_GROUPS * SSM_STATE:]

    dt = _softplus(dt_ref[...] + dtb_ref[...])
    da = dt * (-jnp.exp(alog_ref[...]))
    row = lax.broadcasted_iota(jnp.int32, (L, L), 0)
    col = lax.broadcasted_iota(jnp.int32, (L, L), 1)
    causal = col <= row
    a_cs = jnp.dot(causal.astype(F32), da, precision=HI, preferred_element_type=F32)
    a_cs_t = a_cs.T
    expand = expand_ref[...]
    dt_full = jnp.dot(dt, expand, precision=HI, preferred_element_type=F32)
    acs_full = jnp.dot(a_cs, expand, precision=HI, preferred_element_type=F32)
    alast_full = acs_full[L - 1:L, :]
    xdt = xs * dt_full
    xdt_b = xdt.astype(BF)
    xst_b = (xdt * jnp.exp(alast_full - acs_full)).astype(BF)
    left = lax.broadcasted_iota(jnp.int32, (L, LANES), 1) < SSM_HEAD_DIM

    y_parts = []
    for g in range(SSM_GROUPS):
        cg = cm[:, g * SSM_STATE:(g + 1) * SSM_STATE].astype(BF)
        bg = bm[:, g * SSM_STATE:(g + 1) * SSM_STATE].astype(BF)
        cb = lax.dot_general(cg, bg, (((1,), (1,)), ((), ())), preferred_element_type=F32)
        st = state_ref[:, g * SSD_GROUP_W:(g + 1) * SSD_GROUP_W]
        y_off = jnp.dot(cg, st.astype(BF), preferred_element_type=F32)
        for j in range(SSD_GROUP_W // LANES):
            h0 = g * (SSM_HEADS // SSM_GROUPS) + 2 * j
            ms = []
            for h in (h0, h0 + 1):
                seg = a_cs[:, h:h + 1] - a_cs_t[h:h + 1, :]
                ms.append((cb * jnp.exp(jnp.where(causal, seg, -jnp.inf))).astype(BF))
            c0 = h0 * SSM_HEAD_DIM
            x2 = xdt_b[:, c0:c0 + LANES]
            zero = jnp.zeros_like(x2)
            xblk = jnp.concatenate([jnp.where(left, x2, zero), jnp.where(left, zero, x2)], axis=0)
            y_parts.append(jnp.dot(jnp.concatenate(ms, axis=1), xblk, preferred_element_type=F32)
                           + y_off[:, j * LANES:(j + 1) * LANES] * jnp.exp(acs_full[:, c0:c0 + LANES]))
        new = lax.dot_general(bg, xst_b[:, g * SSD_GROUP_W:(g + 1) * SSD_GROUP_W], (((0,), (0,)), ((), ())),
                              preferred_element_type=F32)
        state_ref[:, g * SSD_GROUP_W:(g + 1) * SSD_GROUP_W] = (
            st * jnp.exp(alast_full[:, g * SSD_GROUP_W:(g + 1) * SSD_GROUP_W]) + new)
    y = jnp.concatenate(y_parts, axis=1) + dskip_ref[...] * xs
    y = y * _silu(z_ref[...])
    outs = []
    for g in range(SSM_GROUPS):
        yg = y[:, g * SSD_GROUP_W:(g + 1) * SSD_GROUP_W]
        outs.append(yg * lax.rsqrt(jnp.mean(yg * yg, axis=-1, keepdims=True) + LN_EPS))
    o_ref[...] = jnp.concatenate(outs, axis=1) * ng_ref[...]


def _ssd(z, xbc, dt, seq, conv_w, conv_b, dt_bias, a_log, d_skip, norm_g):
    t = z.shape[0]
    nc = seq // SSM_CHUNK
    row = lambda b, c: (b * nc + c, 0)
    const = lambda b, c: (0, 0)
    pad = LANES - SSM_HEADS
    expand = (jnp.arange(LANES)[:, None] == (jnp.arange(SSM_D_INNER)[None, :] // SSM_HEAD_DIM)).astype(F32)
    return pl.pallas_call(
        _ssd_kernel,
        out_shape=jax.ShapeDtypeStruct((t, SSM_D_INNER), F32),
        grid=(t // seq, nc),
        in_specs=[pl.BlockSpec((SSM_CHUNK, SSM_D_INNER), row),
                  pl.BlockSpec((SSM_CHUNK, SSM_CONV_CH), row),
                  pl.BlockSpec((SSM_CHUNK, LANES), row),
                  pl.BlockSpec((CONV_K, SSM_CONV_CH), const),
                  pl.BlockSpec((1, SSM_CONV_CH), const),
                  pl.BlockSpec((1, LANES), const),
                  pl.BlockSpec((1, LANES), const),
                  pl.BlockSpec((1, SSM_D_INNER), const),
                  pl.BlockSpec((1, SSM_D_INNER), const),
                  pl.BlockSpec((LANES, SSM_D_INNER), const)],
        out_specs=pl.BlockSpec((SSM_CHUNK, SSM_D_INNER), row),
        scratch_shapes=[pltpu.VMEM((HALO, SSM_CONV_CH), F32), pltpu.VMEM((SSM_STATE, SSM_D_INNER), F32)],
        compiler_params=_cparams(("parallel", "arbitrary")),
        name="ssd",
    )(z, xbc, dt, conv_w, conv_b.reshape(1, -1), jnp.pad(dt_bias, (0, pad)).reshape(1, -1),
      jnp.pad(a_log, (0, pad)).reshape(1, -1), jnp.repeat(d_skip, SSM_HEAD_DIM).reshape(1, -1),
      norm_g.reshape(1, -1), expand)


def _swa_kernel(sink_ref, q_ref, kv_ref, kvp_ref, o_ref):
    L = ATTN_BLOCK
    hd = ATTN_HEAD_DIM
    first = pl.program_id(1) == 0
    kv = jnp.concatenate([kvp_ref[...], kv_ref[...]], axis=0)
    kk = kv[:, 0:LANES]
    vv = kv[:, LANES:2 * LANES].astype(BF)
    lane_k = lax.broadcasted_iota(jnp.int32, (2 * L, LANES), 1)
    qi = lax.broadcasted_iota(jnp.int32, (2 * L, 2 * L), 0) % L
    kj = lax.broadcasted_iota(jnp.int32, (2 * L, 2 * L), 1)
    rel = qi + L - kj
    valid = (rel >= 0) & (rel < L) & (jnp.logical_not(first) | (kj >= L))
    top = lax.broadcasted_iota(jnp.int32, (2 * L, 1), 0) < L
    lane_o = lax.broadcasted_iota(jnp.int32, (L, LANES), 1)
    scale = hd ** -0.5
    for j in range(ATTN_Q_HEADS // 2):
        g = (2 * j) // ATTN_REP
        q2 = q_ref[:, j * LANES:(j + 1) * LANES]
        q2r = pltpu.roll(q2, hd, axis=1)
        kg = jnp.where((lane_k >= g * hd) & (lane_k < (g + 1) * hd), kk, 0.0).astype(BF)
        qa, qb = (q2, q2r) if g == 0 else (q2r, q2)
        qs = jnp.concatenate([qa, qb], axis=0).astype(BF)
        logits = lax.dot_general(qs, kg, (((1,), (1,)), ((), ())), preferred_element_type=F32) * scale
        logits = jnp.where(valid, logits, -jnp.inf)
        sink = jnp.where(top, sink_ref[2 * j], sink_ref[2 * j + 1])
        m = jnp.maximum(jnp.max(logits, axis=-1, keepdims=True), sink)
        p = jnp.exp(logits - m)
        probs = p / (jnp.sum(p, axis=-1, keepdims=True) + jnp.exp(sink - m))
        o = jnp.dot(probs.astype(BF), vv, preferred_element_type=F32)
        oa, ob = o[0:L], o[L:2 * L]
        if g == 0:
            out2 = jnp.where(lane_o < hd, oa, pltpu.roll(ob, hd, axis=1))
        else:
            out2 = jnp.where(lane_o < hd, pltpu.roll(oa, hd, axis=1), ob)
        o_ref[:, j * LANES:(j + 1) * LANES] = out2


def _swa(q, kv, seq, sinks):
    t = q.shape[0]
    nb = seq // ATTN_BLOCK
    return pl.pallas_call(
        _swa_kernel,
        out_shape=jax.ShapeDtypeStruct((t, ATTN_Q_W), F32),
        grid=(t // seq, nb),
        in_specs=[pl.BlockSpec(memory_space=pltpu.SMEM),
                  pl.BlockSpec((ATTN_BLOCK, ATTN_Q_W), lambda b, n: (b * nb + n, 0)),
                  pl.BlockSpec((ATTN_BLOCK, ATTN_KV_W), lambda b, n: (b * nb + n, 0)),
                  pl.BlockSpec((ATTN_BLOCK, ATTN_KV_W), lambda b, n: (b * nb + jnp.maximum(n - 1, 0), 0))],
        out_specs=pl.BlockSpec((ATTN_BLOCK, ATTN_Q_W), lambda b, n: (b * nb + n, 0)),
        compiler_params=_cparams(("parallel", "parallel")),
        name="swa",
    )(sinks, q, kv, kv)


def _outproj_kernel(ya_ref, yb_ref, yc_ref, h_ref, wa_ref, wb_ref, wc_ref, g_ref, b_ref, o_ref):
    mix = (jnp.dot(ya_ref[...].astype(BF), wa_ref[...], preferred_element_type=F32)
           + jnp.dot(yb_ref[...].astype(BF), wb_ref[...], preferred_element_type=F32)
           + jnp.dot(yc_ref[...].astype(BF), wc_ref[...], preferred_element_type=F32))
    o_ref[...] = _ln_math(DN_ALPHA * h_ref[...] + mix, g_ref[...], b_ref[...])


def _out_proj(ya, yb, yc, h, w_out, g, b):
    t = h.shape[0]
    w = w_out.astype(BF)
    row = lambda i: (i, 0)
    const = lambda i: (0, 0)
    c1, c2 = W_LRU, W_LRU + SSM_D_INNER
    return pl.pallas_call(
        _outproj_kernel,
        out_shape=jax.ShapeDtypeStruct((t, D_MODEL), F32),
        grid=(t // ROW_TILE,),
        in_specs=[pl.BlockSpec((ROW_TILE, W_LRU), row), pl.BlockSpec((ROW_TILE, SSM_D_INNER), row),
                  pl.BlockSpec((ROW_TILE, ATTN_Q_W), row), pl.BlockSpec((ROW_TILE, D_MODEL), row),
                  pl.BlockSpec((W_LRU, D_MODEL), const), pl.BlockSpec((SSM_D_INNER, D_MODEL), const),
                  pl.BlockSpec((ATTN_Q_W, D_MODEL), const),
                  pl.BlockSpec((1, D_MODEL), const), pl.BlockSpec((1, D_MODEL), const)],
        out_specs=pl.BlockSpec((ROW_TILE, D_MODEL), row),
        compiler_params=_cparams(("parallel",)),
        name="out_proj",
    )(ya, yb, yc, h, w[0:c1], w[c1:c2], w[c2:MIX_WIDTH], g.reshape(1, -1), b.reshape(1, -1))


def _gather_group(idx_ref, tbl_ref, t, g):
    rs = ROW_WORDS_SUBLANES
    pieces = []
    for q in range(GROUP_PAIRS // 2):
        p = g * GROUP_PAIRS + 2 * q
        if p % SUBLANES == 0:
            window = idx_ref.at[t, pl.ds(p, SUBLANES)]
        ra = pl.multiple_of(window[p % SUBLANES], rs)
        rb = pl.multiple_of(window[p % SUBLANES + 1], rs)
        pieces.append(jnp.concatenate([tbl_ref[pl.ds(ra, rs), :], tbl_ref[pl.ds(rb, rs), :]], axis=0))
    return pltpu.bitcast(jnp.concatenate(pieces, axis=0), BF)


def _chunk_of_column(col):
    j = col % PAIR_LANES
    return j // 2 + ROW_WORDS_SUBLANES * (j % 2)


def _split_hi_lo(x):
    hi = x.astype(BF)
    lo = (x - hi.astype(F32)).astype(BF)
    return hi, lo


def _peer_in_kernel(idx_ref, x_ref, gate_ref, tbl_ref, o_ref, m_ref):
    n = x_ref.shape[0]
    row = lax.broadcasted_iota(jnp.int32, (2 * SUBLANES, GROUP_K), 0) % SUBLANES
    keep = row == _chunk_of_column(lax.broadcasted_iota(jnp.int32, (2 * SUBLANES, GROUP_K), 1))
    for t in range(n):
        x_row = x_ref[pl.ds(t, 1), :]
        x8 = jnp.concatenate([x_row[:, c * LANES:(c + 1) * LANES] for c in range(TOKEN_SUBLANES)], axis=0)
        x_hi, x_lo = _split_hi_lo(x8)
        x16 = jnp.concatenate([x_hi, x_lo], axis=0)
        for g in range(PEER_PAIRS // GROUP_PAIRS):
            r = lax.dot_general(x16, _gather_group(idx_ref, tbl_ref, t, g), (((1,), (1,)), ((), ())),
                                preferred_element_type=F32)
            m_ref[pl.ds(t, 1), g * GROUP_K:(g + 1) * GROUP_K] = jnp.sum(jnp.where(keep, r, 0.0), axis=0,
                                                                        keepdims=True)
    wide = PEER_PAIRS * PAIR_LANES
    fold = (lax.broadcasted_iota(jnp.int32, (wide, PEER_PAIRS), 0) // PAIR_LANES
            == lax.broadcasted_iota(jnp.int32, (wide, PEER_PAIRS), 1)).astype(BF)
    m_hi, m_lo = _split_hi_lo(m_ref[...])
    a = jnp.dot(m_hi, fold, preferred_element_type=F32) + jnp.dot(m_lo, fold, preferred_element_type=F32)
    o_ref[...] = gate_ref[...] * (0.5 * a * (1.0 + lax.erf(a * (2.0 ** -0.5))))


def _peer_in(rows, x, gate, tbl):
    t = x.shape[0]
    tt = GATHER_TOKENS
    pairs = pl.BlockSpec((tt, PEER_PAIRS), lambda i: (i, 0))
    return pl.pallas_call(
        _peer_in_kernel,
        out_shape=jax.ShapeDtypeStruct((t, PEER_PAIRS), F32),
        grid=(t // tt,),
        in_specs=[
            pl.BlockSpec((tt, PEER_PAIRS), lambda i: (i, 0), memory_space=pltpu.SMEM),
            pl.BlockSpec((tt, D_MODEL), lambda i: (i, 0)),
            pairs,
            pl.BlockSpec(memory_space=pltpu.VMEM),
        ],
        out_specs=pairs,
        scratch_shapes=[pltpu.VMEM((tt, PEER_PAIRS * PAIR_LANES), F32)],
        compiler_params=pltpu.CompilerParams(dimension_semantics=("parallel",), vmem_limit_bytes=PEER_VMEM_LIMIT),
        name="peer_in",
    )(rows, x, gate, tbl)


def _peer_out_kernel(idx_ref, w_ref, tbl_ref, h_ref, g_ref, b_ref, o_ref, wide_ref, ffn_ref):
    n = w_ref.shape[0]
    wide = PEER_PAIRS * PAIR_LANES
    rep = (lax.broadcasted_iota(jnp.int32, (PEER_PAIRS, wide), 1) // PAIR_LANES
           == lax.broadcasted_iota(jnp.int32, (PEER_PAIRS, wide), 0)).astype(BF)
    for k, wv in enumerate(_split_hi_lo(w_ref[...])):
        wide_ref[k] = jnp.dot(wv, rep, preferred_element_type=F32)
    keep = (lax.broadcasted_iota(jnp.int32, (SUBLANES, GROUP_K), 0)
            == _chunk_of_column(lax.broadcasted_iota(jnp.int32, (SUBLANES, GROUP_K), 1)))
    for t in range(n):
        acc = jnp.zeros((2 * SUBLANES, LANES), F32)
        for g in range(PEER_PAIRS // GROUP_PAIRS):
            cols = slice(g * GROUP_K, (g + 1) * GROUP_K)
            lhs = jnp.concatenate(
                [jnp.where(keep, jnp.broadcast_to(wide_ref[k, pl.ds(t, 1), cols], (SUBLANES, GROUP_K)), 0.0)
                 for k in range(2)], axis=0).astype(BF)
            acc = acc + jnp.dot(lhs, _gather_group(idx_ref, tbl_ref, t, g), preferred_element_type=F32)
        out8 = acc[0:SUBLANES] + acc[SUBLANES:]
        for c in range(TOKEN_SUBLANES):
            ffn_ref[pl.ds(t, 1), c * LANES:(c + 1) * LANES] = out8[c:c + 1, :]
    o_ref[...] = _ln_math(DN_ALPHA * h_ref[...] + ffn_ref[...], g_ref[...], b_ref[...])


def _peer_out(rows, w, tbl, h, g, b):
    t = h.shape[0]
    tt = GATHER_TOKENS
    row = pl.BlockSpec((tt, D_MODEL), lambda i: (i, 0))
    vec = pl.BlockSpec((1, D_MODEL), lambda i: (0, 0))
    return pl.pallas_call(
        _peer_out_kernel,
        out_shape=jax.ShapeDtypeStruct((t, D_MODEL), F32),
        grid=(t // tt,),
        in_specs=[
            pl.BlockSpec((tt, PEER_PAIRS), lambda i: (i, 0), memory_space=pltpu.SMEM),
            pl.BlockSpec((tt, PEER_PAIRS), lambda i: (i, 0)),
            pl.BlockSpec(memory_space=pltpu.VMEM),
            row, vec, vec,
        ],
        out_specs=row,
        scratch_shapes=[pltpu.VMEM((2, tt, PEER_PAIRS * PAIR_LANES), F32), pltpu.VMEM((tt, D_MODEL), F32)],
        compiler_params=pltpu.CompilerParams(vmem_limit_bytes=PEER_VMEM_LIMIT),
        name="peer_out",
    )(rows, w, tbl, h, g.reshape(1, D_MODEL), b.reshape(1, D_MODEL))


def _extract_topk(s, pos, k, payload=None):
    big = jnp.asarray(2 ** 30, pos.dtype)
    vals, poss, pays = [], [], []
    for _ in range(k):
        m = jnp.max(s, axis=0, keepdims=True)
        j = jnp.min(jnp.where(s == m, pos, big), axis=0, keepdims=True)
        sel = pos == j
        vals.append(m)
        poss.append(j)
        if payload is not None:
            pays.append(jnp.sum(jnp.where(sel, payload, jnp.zeros_like(payload)), axis=0, keepdims=True))
        s = jnp.where(sel, -jnp.inf, s)
    out = [jnp.concatenate(vals, axis=0), jnp.concatenate(poss, axis=0)]
    if payload is not None:
        out.append(jnp.concatenate(pays, axis=0))
    return out


def _peer_route_kernel(h_ref, wqt_ref, keys_ref, idx_ref, gate_ref):
    half_dim = PEER_DKEY // 2
    hb = h_ref[...].astype(BF)
    n = hb.shape[0]
    key_iota = lax.broadcasted_iota(jnp.int32, (PEER_NKEYS, n), 0).astype(F32)
    iota16 = lax.broadcasted_iota(jnp.int32, (PEER_TOPK, n), 0).astype(F32)
    iota8 = lax.broadcasted_iota(jnp.int32, (SUBLANES, n), 0).astype(F32)
    for hd in range(PEER_HEADS):
        qt = lax.dot_general(wqt_ref[hd * PEER_DKEY:(hd + 1) * PEER_DKEY, :], hb,
                             (((1,), (1,)), ((), ())), preferred_element_type=F32)
        tv, ti = [], []
        for half in range(2):
            s = jnp.dot(keys_ref[half], qt[half * half_dim:(half + 1) * half_dim, :], preferred_element_type=F32)
            v, i = _extract_topk(s, key_iota, PEER_TOPK)
            tv.append(v)
            ti.append(i)
        cs = [tv[0][0:1, :] + tv[1]]
        cp = [iota16]
        ce = [ti[0][0:1, :] * PEER_NKEYS + ti[1]]
        for a in range(1, SUBLANES):
            cs.append(tv[0][a:a + 1, :] + tv[1][0:SUBLANES, :])
            cp.append(iota8 + a * PEER_TOPK)
            ce.append(ti[0][a:a + 1, :] * PEER_NKEYS + ti[1][0:SUBLANES, :])
        cs.append(tv[0][SUBLANES:, :] + tv[1][0:1, :])
        cp.append((iota8 + SUBLANES) * PEER_TOPK)
        ce.append(ti[0][SUBLANES:, :] * PEER_NKEYS + ti[1][0:1, :])
        best_s, _, best_e = _extract_topk(jnp.concatenate(cs, axis=0), jnp.concatenate(cp, axis=0), PEER_TOPK,
                                          payload=jnp.concatenate(ce, axis=0))
        ex = jnp.exp(best_s - best_s[0:1, :])
        gate = ex / jnp.sum(ex, axis=0, keepdims=True)
        cols = slice(hd * PEER_TOPK, (hd + 1) * PEER_TOPK)
        idx_ref[:, cols] = (best_e * ROW_WORDS_SUBLANES).astype(jnp.int32).T
        gate_ref[:, cols] = gate.T


def _peer_route(h, wqt, keys):
    t = h.shape[0]
    rt = ROUTE_TOKENS
    pairs = pl.BlockSpec((rt, PEER_PAIRS), lambda i: (i, 0))
    return pl.pallas_call(
        _peer_route_kernel,
        out_shape=(jax.ShapeDtypeStruct((t, PEER_PAIRS), jnp.int32), jax.ShapeDtypeStruct((t, PEER_PAIRS), F32)),
        grid=(t // rt,),
        in_specs=[
            pl.BlockSpec((rt, D_MODEL), lambda i: (i, 0)),
            pl.BlockSpec((D_MODEL, D_MODEL), lambda i: (0, 0)),
            pl.BlockSpec((2, PEER_NKEYS, PEER_DKEY // 2), lambda i: (0, 0, 0)),
        ],
        out_specs=(pairs, pairs),
        compiler_params=_cparams(("parallel",)),
        name="peer_route",
    )(h, wqt, keys)


def _pack_table(u):
    ub = u.astype(jnp.bfloat16)
    half = D_MODEL // 2
    lo = lax.bitcast_convert_type(ub[:, :half], jnp.uint16).astype(jnp.uint32)
    hi = lax.bitcast_convert_type(ub[:, half:], jnp.uint16).astype(jnp.uint32)
    return (lo | (hi << 16)).reshape(u.shape[0] * ROW_WORDS_SUBLANES, LANES)


def _peer_sublayer(h, wq, keys, u, v, ln_g, ln_b):
    rows, gate = _peer_route(h, wq.T.astype(BF), keys)
    w = _peer_in(rows, h, gate, _pack_table(u))
    return _peer_out(rows, w, _pack_table(v), h, ln_g, ln_b)


def kernel(x, emb_ln_g, emb_ln_b, w_in, rg_conv_w, rg_conv_b, rg_wa, rg_ba, rg_wx, rg_bx, rg_lambda, ssm_conv_w, ssm_conv_b, ssm_dt_bias, ssm_a_log, ssm_d, ssm_norm_g, attn_sinks, w_out, ln1_g, ln1_b, peer_wq, peer_keys, peer_u, peer_v, ln2_g, ln2_b):
    bsz, seq, d = x.shape
    assert d == D_MODEL and seq % ROW_TILE == 0 and (bsz * seq) % ROUTE_TOKENS == 0
    h = _layer_norm(x.reshape(bsz * seq, d), emb_ln_g, emb_ln_b)
    for l in range(DEPTH):
        rg, z, xbc, dt, q, kv = _in_proj(h, _split_w_in(w_in[l]))
        y_a = _rg_lru(rg, seq, rg_conv_w[l], rg_conv_b[l], _block_diag(rg_wa[l]), rg_ba[l],
                      _block_diag(rg_wx[l]), rg_bx[l], rg_lambda[l])
        y_b = _ssd(z, xbc, dt, seq, ssm_conv_w[l], ssm_conv_b[l], ssm_dt_bias[l], ssm_a_log[l], ssm_d[l],
                   ssm_norm_g[l])
        y_c = _swa(q, kv, seq, attn_sinks[l])
        h = _out_proj(y_a, y_b, y_c, h, w_out[l], ln1_g[l], ln1_b[l])
        h = _peer_sublayer(h, peer_wq[l], peer_keys[l], peer_u[l], peer_v[l], ln2_g[l], ln2_b[l])
    return h.reshape(bsz, seq, d)
```

```python
import math

import jax
import jax.numpy as jnp
from jax import lax
from jax.experimental import pallas as pl
from jax.experimental.pallas import tpu as pltpu

D_MODEL = 1024
DEPTH = 2

W_LRU = D_MODEL // 2
LRU_BLOCKS = 8
LRU_C = 8.0
CONV_K = 4

SSM_HEAD_DIM = 64
SSM_D_INNER = D_MODEL
SSM_HEADS = SSM_D_INNER // SSM_HEAD_DIM
SSM_GROUPS = 2
SSM_STATE = 128
SSM_CHUNK = 128
SSM_CONV_CH = SSM_D_INNER + 2 * SSM_GROUPS * SSM_STATE
SSD_GROUP_W = SSM_D_INNER // SSM_GROUPS

ATTN_HEAD_DIM = 64
ATTN_Q_HEADS = (D_MODEL // 2) // ATTN_HEAD_DIM
ATTN_KV_HEADS = 2
ATTN_REP = ATTN_Q_HEADS // ATTN_KV_HEADS
ATTN_BLOCK = 128
ATTN_Q_W = ATTN_Q_HEADS * ATTN_HEAD_DIM
ATTN_KV_W = 2 * ATTN_KV_HEADS * ATTN_HEAD_DIM

MIX_WIDTH = W_LRU + SSM_D_INNER + ATTN_Q_W

PEER_HEADS = 8
PEER_NKEYS = 128
PEER_DKEY = 128
PEER_TOPK = 16
PEER_PAIRS = PEER_HEADS * PEER_TOPK

DN_ALPHA = (2 * DEPTH) ** 0.25
LN_EPS = 1e-5

SUBLANES = 8
LANES = 128
HALO = SUBLANES
ROW_WORDS_SUBLANES = D_MODEL // 2 // LANES
TOKEN_SUBLANES = D_MODEL // LANES

ROW_TILE = 512
PAIR_LANES = 2 * ROW_WORDS_SUBLANES
GROUP_PAIRS = 32
GROUP_K = GROUP_PAIRS * PAIR_LANES
GATHER_TOKENS = 128
ROUTE_TOKENS = 256
VMEM_LIMIT = 48 * 1024 * 1024
PEER_VMEM_LIMIT = 56 * 1024 * 1024

BF = jnp.bfloat16
F32 = jnp.float32
HI = lax.Precision.HIGHEST

PROJ_SEGS = (2 * W_LRU, SSM_D_INNER, SSM_CONV_CH, LANES, ATTN_Q_W, ATTN_KV_W)


def _cparams(sem):
    return pltpu.CompilerParams(dimension_semantics=sem, vmem_limit_bytes=VMEM_LIMIT)


def _softplus(x):
    return jnp.maximum(x, 0.0) + jnp.log1p(jnp.exp(-jnp.abs(x)))


def _sigmoid(x):
    return 1.0 / (1.0 + jnp.exp(-x))


def _silu(x):
    return x * _sigmoid(x)


def _gelu_tanh(x):
    return 0.5 * x * (1.0 + jnp.tanh(math.sqrt(2.0 / math.pi) * (x + 0.044715 * (x * x * x))))


def _ln_math(x, g, b):
    mu = jnp.mean(x, axis=-1, keepdims=True)
    xc = x - mu
    var = jnp.mean(xc * xc, axis=-1, keepdims=True)
    return xc * lax.rsqrt(var + LN_EPS) * g + b


def _ln_kernel(a_ref, g_ref, b_ref, o_ref):
    o_ref[...] = _ln_math(a_ref[...], g_ref[...], b_ref[...])


def _res_ln_kernel(a_ref, r_ref, g_ref, b_ref, o_ref):
    o_ref[...] = _ln_math(DN_ALPHA * a_ref[...] + r_ref[...], g_ref[...], b_ref[...])


def _layer_norm(a, g, b, res=None):
    t, d = a.shape
    row = pl.BlockSpec((ROW_TILE, d), lambda i: (i, 0))
    vec = pl.BlockSpec((1, d), lambda i: (0, 0))
    if res is None:
        kern, args, specs = _ln_kernel, (a,), [row]
    else:
        kern, args, specs = _res_ln_kernel, (a, res), [row, row]
    return pl.pallas_call(
        kern, out_shape=jax.ShapeDtypeStruct((t, d), F32), grid=(t // ROW_TILE,),
        in_specs=specs + [vec, vec], out_specs=row, compiler_params=_cparams(("parallel",)),
        name="layer_norm",
    )(*args, g.reshape(1, d), b.reshape(1, d))


def _inproj_kernel(h_ref, *refs):
    n = len(PROJ_SEGS)
    hb = h_ref[...].astype(BF)
    for w_ref, o_ref in zip(refs[:n], refs[n:]):
        o_ref[...] = jnp.dot(hb, w_ref[...], preferred_element_type=F32)


def _in_proj(h, ws):
    t = h.shape[0]
    return pl.pallas_call(
        _inproj_kernel,
        out_shape=tuple(jax.ShapeDtypeStruct((t, w), F32) for w in PROJ_SEGS),
        grid=(t // ROW_TILE,),
        in_specs=[pl.BlockSpec((ROW_TILE, D_MODEL), lambda i: (i, 0))]
        + [pl.BlockSpec((D_MODEL, w), lambda i: (0, 0)) for w in PROJ_SEGS],
        out_specs=tuple(pl.BlockSpec((ROW_TILE, w), lambda i: (i, 0)) for w in PROJ_SEGS),
        compiler_params=_cparams(("parallel",)),
        name="in_proj",
    )(h, *ws)


def _split_w_in(w_in):
    w = w_in.astype(BF)
    c = [0]
    for width in (2 * W_LRU, SSM_D_INNER, SSM_CONV_CH, SSM_HEADS, ATTN_Q_W, ATTN_KV_W):
        c.append(c[-1] + width)
    segs = [w[:, c[i]:c[i + 1]] for i in range(6)]
    segs[3] = jnp.pad(segs[3], ((0, 0), (0, LANES - SSM_HEADS)))
    return tuple(segs)


def _shift_rows(x, s, fill):
    n = x.shape[0]
    if s % SUBLANES == 0:
        return jnp.concatenate([jnp.full((s,) + x.shape[1:], fill, x.dtype), x[:n - s]], axis=0)
    rolled = pltpu.roll(x, s, axis=0)
    row = lax.broadcasted_iota(jnp.int32, x.shape, 0)
    return jnp.where(row < s, fill, rolled)


def _causal_conv(x, halo, w_ref, b_ref):
    n = x.shape[0]
    xp = jnp.concatenate([halo, x], axis=0)
    out = b_ref[...] + w_ref[CONV_K - 1:CONV_K, :] * x
    for s in range(1, CONV_K):
        out = out + w_ref[CONV_K - 1 - s:CONV_K - s, :] * pltpu.roll(xp, s, axis=0)[HALO:HALO + n]
    return out


def _rglru_kernel(rg_ref, cw_ref, cb_ref, wa_ref, ba_ref, wx_ref, bx_ref, lam_ref, o_ref, halo_ref, carry_ref):
    n = rg_ref.shape[0]

    @pl.when(pl.program_id(1) == 0)
    def _():
        halo_ref[...] = jnp.zeros_like(halo_ref)
        carry_ref[...] = jnp.zeros_like(carry_ref)

    x = rg_ref[:, 0:W_LRU]
    xc = _causal_conv(x, halo_ref[...], cw_ref, cb_ref)
    halo_ref[...] = x[n - HALO:n]
    xb = xc.astype(BF)
    r = _sigmoid(jnp.dot(xb, wa_ref[...], preferred_element_type=F32) + ba_ref[...])
    i = _sigmoid(jnp.dot(xb, wx_ref[...], preferred_element_type=F32) + bx_ref[...])
    log_a = (-LRU_C * r) * _softplus(-lam_ref[...])
    a = jnp.exp(log_a)
    u = jnp.sqrt(-jnp.tanh(log_a) * (a * a + 1.0)) * (i * xc)
    k = 1
    while k < n:
        u = a * _shift_rows(u, k, 0.0) + u
        a = a * _shift_rows(a, k, 1.0)
        k *= 2
    h = u + a * carry_ref[0:1, :]
    carry_ref[...] = jnp.broadcast_to(h[n - 1:n, :], carry_ref.shape)
    o_ref[...] = _gelu_tanh(rg_ref[:, W_LRU:2 * W_LRU]) * h


def _rg_lru(rg, seq, conv_w, conv_b, wa, ba, wx, bx, lam):
    t = rg.shape[0]
    nb = seq // ROW_TILE
    row = lambda b, j: (b * nb + j, 0)
    const = lambda b, j: (0, 0)
    vec = pl.BlockSpec((1, W_LRU), const)
    return pl.pallas_call(
        _rglru_kernel,
        out_shape=jax.ShapeDtypeStruct((t, W_LRU), F32),
        grid=(t // seq, nb),
        in_specs=[pl.BlockSpec((ROW_TILE, 2 * W_LRU), row),
                  pl.BlockSpec((CONV_K, W_LRU), const), vec,
                  pl.BlockSpec((W_LRU, W_LRU), const), vec,
                  pl.BlockSpec((W_LRU, W_LRU), const), vec, vec],
        out_specs=pl.BlockSpec((ROW_TILE, W_LRU), row),
        scratch_shapes=[pltpu.VMEM((HALO, W_LRU), F32), pltpu.VMEM((SUBLANES, W_LRU), F32)],
        compiler_params=_cparams(("parallel", "arbitrary")),
        name="rg_lru",
    )(rg, conv_w, conv_b.reshape(1, -1), wa, ba.reshape(1, -1), wx, bx.reshape(1, -1), lam.reshape(1, -1))


def _block_diag(w):
    nb, c, _ = w.shape
    eye = jnp.eye(nb, dtype=w.dtype)
    return (eye[:, None, :, None] * w[:, :, None, :]).reshape(nb * c, nb * c).astype(BF)


def _ssd_kernel(z_ref, xbc_ref, dt_ref, cw_ref, cb_ref, dtb_ref, alog_ref, dskip_ref, ng_ref, expand_ref,
                o_ref, halo_ref, state_ref):
    L = SSM_CHUNK

    @pl.when(pl.program_id(1) == 0)
    def _():
        halo_ref[...] = jnp.zeros_like(halo_ref)
        state_ref[...] = jnp.zeros_like(state_ref)

    xbc = xbc_ref[...]
    conv = _silu(_causal_conv(xbc, halo_ref[...], cw_ref, cb_ref))
    halo_ref[...] = xbc[L - HALO:L]
    xs = conv[:, 0:SSM_D_INNER]
    bm = conv[:, SSM_D_INNER:SSM_D_INNER + SSM_GROUPS * SSM_STATE]
    cm = conv[:, SSM_D_INNER + SSM_GROUPS * SSM_STATE:]

    dt = _softplus(dt_ref[...] + dtb_ref[...])
    da = dt * (-jnp.exp(alog_ref[...]))
    row = lax.broadcasted_iota(jnp.int32, (L, L), 0)
    col = lax.broadcasted_iota(jnp.int32, (L, L), 1)
    causal = col <= row
    a_cs = jnp.dot(causal.astype(F32), da, precision=HI, preferred_element_type=F32)
    a_cs_t = a_cs.T
    expand = expand_ref[...]
    dt_full = jnp.dot(dt, expand, precision=HI, preferred_element_type=F32)
    acs_full = jnp.dot(a_cs, expand, precision=HI, preferred_element_type=F32)
    alast_full = acs_full[L - 1:L, :]
    xdt = xs * dt_full
    xdt_b = xdt.astype(BF)
    xst_b = (xdt * jnp.exp(alast_full - acs_full)).astype(BF)
    left = lax.broadcasted_iota(jnp.int32, (L, LANES), 1) < SSM_HEAD_DIM

    y_parts = []
    for g in range(SSM_GROUPS):
        cg = cm[:, g * SSM_STATE:(g + 1) * SSM_STATE].astype(BF)
        bg = bm[:, g * SSM_STATE:(g + 1) * SSM_STATE].astype(BF)
        cb = lax.dot_general(cg, bg, (((1,), (1,)), ((), ())), preferred_element_type=F32)
        st = state_ref[:, g * SSD_GROUP_W:(g + 1) * SSD_GROUP_W]
        y_off = jnp.dot(cg, st.astype(BF), preferred_element_type=F32)
        for j in range(SSD_GROUP_W // LANES):
            h0 = g * (SSM_HEADS // SSM_GROUPS) + 2 * j
            ms = []
            for h in (h0, h0 + 1):
                seg = a_cs[:, h:h + 1] - a_cs_t[h:h + 1, :]
                ms.append((cb * jnp.exp(jnp.where(causal, seg, -jnp.inf))).astype(BF))
            c0 = h0 * SSM_HEAD_DIM
            x2 = xdt_b[:, c0:c0 + LANES]
            zero = jnp.zeros_like(x2)
            xblk = jnp.concatenate([jnp.where(left, x2, zero), jnp.where(left, zero, x2)], axis=0)
            y_parts.append(jnp.dot(jnp.concatenate(ms, axis=1), xblk, preferred_element_type=F32)
                           + y_off[:, j * LANES:(j + 1) * LANES] * jnp.exp(acs_full[:, c0:c0 + LANES]))
        new = lax.dot_general(bg, xst_b[:, g * SSD_GROUP_W:(g + 1) * SSD_GROUP_W], (((0,), (0,)), ((), ())),
                              preferred_element_type=F32)
        state_ref[:, g * SSD_GROUP_W:(g + 1) * SSD_GROUP_W] = (
            st * jnp.exp(alast_full[:, g * SSD_GROUP_W:(g + 1) * SSD_GROUP_W]) + new)
    y = jnp.concatenate(y_parts, axis=1) + dskip_ref[...] * xs
    y = y * _silu(z_ref[...])
    outs = []
    for g in range(SSM_GROUPS):
        yg = y[:, g * SSD_GROUP_W:(g + 1) * SSD_GROUP_W]
        outs.append(yg * lax.rsqrt(jnp.mean(yg * yg, axis=-1, keepdims=True) + LN_EPS))
    o_ref[...] = jnp.concatenate(outs, axis=1) * ng_ref[...]


def _ssd(z, xbc, dt, seq, conv_w, conv_b, dt_bias, a_log, d_skip, norm_g):
    t = z.shape[0]
    nc = seq // SSM_CHUNK
    row = lambda b, c: (b * nc + c, 0)
    const = lambda b, c: (0, 0)
    pad = LANES - SSM_HEADS
    expand = (jnp.arange(LANES)[:, None] == (jnp.arange(SSM_D_INNER)[None, :] // SSM_HEAD_DIM)).astype(F32)
    return pl.pallas_call(
        _ssd_kernel,
        out_shape=jax.ShapeDtypeStruct((t, SSM_D_INNER), F32),
        grid=(t // seq, nc),
        in_specs=[pl.BlockSpec((SSM_CHUNK, SSM_D_INNER), row),
                  pl.BlockSpec((SSM_CHUNK, SSM_CONV_CH), row),
                  pl.BlockSpec((SSM_CHUNK, LANES), row),
                  pl.BlockSpec((CONV_K, SSM_CONV_CH), const),
                  pl.BlockSpec((1, SSM_CONV_CH), const),
                  pl.BlockSpec((1, LANES), const),
                  pl.BlockSpec((1, LANES), const),
                  pl.BlockSpec((1, SSM_D_INNER), const),
                  pl.BlockSpec((1, SSM_D_INNER), const),
                  pl.BlockSpec((LANES, SSM_D_INNER), const)],
        out_specs=pl.BlockSpec((SSM_CHUNK, SSM_D_INNER), row),
        scratch_shapes=[pltpu.VMEM((HALO, SSM_CONV_CH), F32), pltpu.VMEM((SSM_STATE, SSM_D_INNER), F32)],
        compiler_params=_cparams(("parallel", "arbitrary")),
        name="ssd",
    )(z, xbc, dt, conv_w, conv_b.reshape(1, -1), jnp.pad(dt_bias, (0, pad)).reshape(1, -1),
      jnp.pad(a_log, (0, pad)).reshape(1, -1), jnp.repeat(d_skip, SSM_HEAD_DIM).reshape(1, -1),
      norm_g.reshape(1, -1), expand)


def _swa_kernel(sink_ref, q_ref, kv_ref, kvp_ref, o_ref):
    L = ATTN_BLOCK
    hd = ATTN_HEAD_DIM
    first = pl.program_id(1) == 0
    kv = jnp.concatenate([kvp_ref[...], kv_ref[...]], axis=0)
    kk = kv[:, 0:LANES]
    vv = kv[:, LANES:2 * LANES].astype(BF)
    lane_k = lax.broadcasted_iota(jnp.int32, (2 * L, LANES), 1)
    qi = lax.broadcasted_iota(jnp.int32, (2 * L, 2 * L), 0) % L
    kj = lax.broadcasted_iota(jnp.int32, (2 * L, 2 * L), 1)
    rel = qi + L - kj
    valid = (rel >= 0) & (rel < L) & (jnp.logical_not(first) | (kj >= L))
    top = lax.broadcasted_iota(jnp.int32, (2 * L, 1), 0) < L
    lane_o = lax.broadcasted_iota(jnp.int32, (L, LANES), 1)
    scale = hd ** -0.5
    for j in range(ATTN_Q_HEADS // 2):
        g = (2 * j) // ATTN_REP
        q2 = q_ref[:, j * LANES:(j + 1) * LANES]
        q2r = pltpu.roll(q2, hd, axis=1)
        kg = jnp.where((lane_k >= g * hd) & (lane_k < (g + 1) * hd), kk, 0.0).astype(BF)
        qa, qb = (q2, q2r) if g == 0 else (q2r, q2)
        qs = jnp.concatenate([qa, qb], axis=0).astype(BF)
        logits = lax.dot_general(qs, kg, (((1,), (1,)), ((), ())), preferred_element_type=F32) * scale
        logits = jnp.where(valid, logits, -jnp.inf)
        sink = jnp.where(top, sink_ref[2 * j], sink_ref[2 * j + 1])
        m = jnp.maximum(jnp.max(logits, axis=-1, keepdims=True), sink)
        p = jnp.exp(logits - m)
        probs = p / (jnp.sum(p, axis=-1, keepdims=True) + jnp.exp(sink - m))
        o = jnp.dot(probs.astype(BF), vv, preferred_element_type=F32)
        oa, ob = o[0:L], o[L:2 * L]
        if g == 0:
            out2 = jnp.where(lane_o < hd, oa, pltpu.roll(ob, hd, axis=1))
        else:
            out2 = jnp.where(lane_o < hd, pltpu.roll(oa, hd, axis=1), ob)
        o_ref[:, j * LANES:(j + 1) * LANES] = out2


def _swa(q, kv, seq, sinks):
    t = q.shape[0]
    nb = seq // ATTN_BLOCK
    return pl.pallas_call(
        _swa_kernel,
        out_shape=jax.ShapeDtypeStruct((t, ATTN_Q_W), F32),
        grid=(t // seq, nb),
        in_specs=[pl.BlockSpec(memory_space=pltpu.SMEM),
                  pl.BlockSpec((ATTN_BLOCK, ATTN_Q_W), lambda b, n: (b * nb + n, 0)),
                  pl.BlockSpec((ATTN_BLOCK, ATTN_KV_W), lambda b, n: (b * nb + n, 0)),
                  pl.BlockSpec((ATTN_BLOCK, ATTN_KV_W), lambda b, n: (b * nb + jnp.maximum(n - 1, 0), 0))],
        out_specs=pl.BlockSpec((ATTN_BLOCK, ATTN_Q_W), lambda b, n: (b * nb + n, 0)),
        compiler_params=_cparams(("parallel", "parallel")),
        name="swa",
    )(sinks, q, kv, kv)


def _outproj_kernel(ya_ref, yb_ref, yc_ref, h_ref, wa_ref, wb_ref, wc_ref, g_ref, b_ref, o_ref):
    mix = (jnp.dot(ya_ref[...].astype(BF), wa_ref[...], preferred_element_type=F32)
           + jnp.dot(yb_ref[...].astype(BF), wb_ref[...], preferred_element_type=F32)
           + jnp.dot(yc_ref[...].astype(BF), wc_ref[...], preferred_element_type=F32))
    o_ref[...] = _ln_math(DN_ALPHA * h_ref[...] + mix, g_ref[...], b_ref[...])


def _out_proj(ya, yb, yc, h, w_out, g, b):
    t = h.shape[0]
    w = w_out.astype(BF)
    row = lambda i: (i, 0)
    const = lambda i: (0, 0)
    c1, c2 = W_LRU, W_LRU + SSM_D_INNER
    return pl.pallas_call(
        _outproj_kernel,
        out_shape=jax.ShapeDtypeStruct((t, D_MODEL), F32),
        grid=(t // ROW_TILE,),
        in_specs=[pl.BlockSpec((ROW_TILE, W_LRU), row), pl.BlockSpec((ROW_TILE, SSM_D_INNER), row),
                  pl.BlockSpec((ROW_TILE, ATTN_Q_W), row), pl.BlockSpec((ROW_TILE, D_MODEL), row),
                  pl.BlockSpec((W_LRU, D_MODEL), const), pl.BlockSpec((SSM_D_INNER, D_MODEL), const),
                  pl.BlockSpec((ATTN_Q_W, D_MODEL), const),
                  pl.BlockSpec((1, D_MODEL), const), pl.BlockSpec((1, D_MODEL), const)],
        out_specs=pl.BlockSpec((ROW_TILE, D_MODEL), row),
        compiler_params=_cparams(("parallel",)),
        name="out_proj",
    )(ya, yb, yc, h, w[0:c1], w[c1:c2], w[c2:MIX_WIDTH], g.reshape(1, -1), b.reshape(1, -1))


def _gather_group(idx_ref, tbl_ref, t, g):
    rs = ROW_WORDS_SUBLANES
    pieces = []
    for q in range(GROUP_PAIRS // 2):
        p = g * GROUP_PAIRS + 2 * q
        if p % SUBLANES == 0:
            window = idx_ref.at[t, pl.ds(p, SUBLANES)]
        ra = pl.multiple_of(window[p % SUBLANES], rs)
        rb = pl.multiple_of(window[p % SUBLANES + 1], rs)
        pieces.append(jnp.concatenate([tbl_ref[pl.ds(ra, rs), :], tbl_ref[pl.ds(rb, rs), :]], axis=0))
    return pltpu.bitcast(jnp.concatenate(pieces, axis=0), BF)


def _chunk_of_column(col):
    j = col % PAIR_LANES
    return j // 2 + ROW_WORDS_SUBLANES * (j % 2)


def _split_hi_lo(x):
    hi = x.astype(BF)
    lo = (x - hi.astype(F32)).astype(BF)
    return hi, lo


def _peer_in_kernel(idx_ref, x_ref, gate_ref, tbl_ref, o_ref, m_ref):
    n = x_ref.shape[0]
    row = lax.broadcasted_iota(jnp.int32, (2 * SUBLANES, GROUP_K), 0) % SUBLANES
    keep = row == _chunk_of_column(lax.broadcasted_iota(jnp.int32, (2 * SUBLANES, GROUP_K), 1))
    for t in range(n):
        x_row = x_ref[pl.ds(t, 1), :]
        x8 = jnp.concatenate([x_row[:, c * LANES:(c + 1) * LANES] for c in range(TOKEN_SUBLANES)], axis=0)
        x_hi, x_lo = _split_hi_lo(x8)
        x16 = jnp.concatenate([x_hi, x_lo], axis=0)
        for g in range(PEER_PAIRS // GROUP_PAIRS):
            r = lax.dot_general(x16, _gather_group(idx_ref, tbl_ref, t, g), (((1,), (1,)), ((), ())),
                                preferred_element_type=F32)
            m_ref[pl.ds(t, 1), g * GROUP_K:(g + 1) * GROUP_K] = jnp.sum(jnp.where(keep, r, 0.0), axis=0,
                                                                        keepdims=True)
    wide = PEER_PAIRS * PAIR_LANES
    fold = (lax.broadcasted_iota(jnp.int32, (wide, PEER_PAIRS), 0) // PAIR_LANES
            == lax.broadcasted_iota(jnp.int32, (wide, PEER_PAIRS), 1)).astype(BF)
    m_hi, m_lo = _split_hi_lo(m_ref[...])
    a = jnp.dot(m_hi, fold, preferred_element_type=F32) + jnp.dot(m_lo, fold, preferred_element_type=F32)
    o_ref[...] = gate_ref[...] * (0.5 * a * (1.0 + lax.erf(a * (2.0 ** -0.5))))


def _peer_in(rows, x, gate, tbl):
    t = x.shape[0]
    tt = GATHER_TOKENS
    pairs = pl.BlockSpec((tt, PEER_PAIRS), lambda i: (i, 0))
    return pl.pallas_call(
        _peer_in_kernel,
        out_shape=jax.ShapeDtypeStruct((t, PEER_PAIRS), F32),
        grid=(t // tt,),
        in_specs=[
            pl.BlockSpec((tt, PEER_PAIRS), lambda i: (i, 0), memory_space=pltpu.SMEM),
            pl.BlockSpec((tt, D_MODEL), lambda i: (i, 0)),
            pairs,
            pl.BlockSpec(memory_space=pltpu.VMEM),
        ],
        out_specs=pairs,
        scratch_shapes=[pltpu.VMEM((tt, PEER_PAIRS * PAIR_LANES), F32)],
        compiler_params=pltpu.CompilerParams(dimension_semantics=("parallel",), vmem_limit_bytes=PEER_VMEM_LIMIT),
        name="peer_in",
    )(rows, x, gate, tbl)


def _peer_out_kernel(idx_ref, w_ref, tbl_ref, h_ref, g_ref, b_ref, o_ref, wide_ref, ffn_ref):
    n = w_ref.shape[0]
    wide = PEER_PAIRS * PAIR_LANES
    rep = (lax.broadcasted_iota(jnp.int32, (PEER_PAIRS, wide), 1) // PAIR_LANES
           == lax.broadcasted_iota(jnp.int32, (PEER_PAIRS, wide), 0)).astype(BF)
    for k, wv in enumerate(_split_hi_lo(w_ref[...])):
        wide_ref[k] = jnp.dot(wv, rep, preferred_element_type=F32)
    keep = (lax.broadcasted_iota(jnp.int32, (SUBLANES, GROUP_K), 0)
            == _chunk_of_column(lax.broadcasted_iota(jnp.int32, (SUBLANES, GROUP_K), 1)))
    for t in range(n):
        acc = jnp.zeros((2 * SUBLANES, LANES), F32)
        for g in range(PEER_PAIRS // GROUP_PAIRS):
            cols = slice(g * GROUP_K, (g + 1) * GROUP_K)
            lhs = jnp.concatenate(
                [jnp.where(keep, jnp.broadcast_to(wide_ref[k, pl.ds(t, 1), cols], (SUBLANES, GROUP_K)), 0.0)
                 for k in range(2)], axis=0).astype(BF)
            acc = acc + jnp.dot(lhs, _gather_group(idx_ref, tbl_ref, t, g), preferred_element_type=F32)
        out8 = acc[0:SUBLANES] + acc[SUBLANES:]
        for c in range(TOKEN_SUBLANES):
            ffn_ref[pl.ds(t, 1), c * LANES:(c + 1) * LANES] = out8[c:c + 1, :]
    o_ref[...] = _ln_math(DN_ALPHA * h_ref[...] + ffn_ref[...], g_ref[...], b_ref[...])


def _peer_out(rows, w, tbl, h, g, b):
    t = h.shape[0]
    tt = GATHER_TOKENS
    row = pl.BlockSpec((tt, D_MODEL), lambda i: (i, 0))
    vec = pl.BlockSpec((1, D_MODEL), lambda i: (0, 0))
    return pl.pallas_call(
        _peer_out_kernel,
        out_shape=jax.ShapeDtypeStruct((t, D_MODEL), F32),
        grid=(t // tt,),
        in_specs=[
            pl.BlockSpec((tt, PEER_PAIRS), lambda i: (i, 0), memory_space=pltpu.SMEM),
            pl.BlockSpec((tt, PEER_PAIRS), lambda i: (i, 0)),
            pl.BlockSpec(memory_space=pltpu.VMEM),
            row, vec, vec,
        ],
        out_specs=row,
        scratch_shapes=[pltpu.VMEM((2, tt, PEER_PAIRS * PAIR_LANES), F32), pltpu.VMEM((tt, D_MODEL), F32)],
        compiler_params=pltpu.CompilerParams(vmem_limit_bytes=PEER_VMEM_LIMIT),
        name="peer_out",
    )(rows, w, tbl, h, g.reshape(1, D_MODEL), b.reshape(1, D_MODEL))


def _extract_topk(s, pos, k, payload=None):
    big = jnp.asarray(2 ** 30, pos.dtype)
    vals, poss, pays = [], [], []
    for _ in range(k):
        m = jnp.max(s, axis=0, keepdims=True)
        j = jnp.min(jnp.where(s == m, pos, big), axis=0, keepdims=True)
        sel = pos == j
        vals.append(m)
        poss.append(j)
        if payload is not None:
            pays.append(jnp.sum(jnp.where(sel, payload, jnp.zeros_like(payload)), axis=0, keepdims=True))
        s = jnp.where(sel, -jnp.inf, s)
    out = [jnp.concatenate(vals, axis=0), jnp.concatenate(poss, axis=0)]
    if payload is not None:
        out.append(jnp.concatenate(pays, axis=0))
    return out


def _topk_two_per_slot(s, k):
    m, n = s.shape[0] // 2, s.shape[1]
    big = jnp.asarray(2 ** 30, F32)
    pos_a = lax.broadcasted_iota(jnp.int32, (m, n), 0).astype(F32)
    a, b = s[:m], s[m:]
    b_wins = b > a
    win, wait = jnp.where(b_wins, b, a), jnp.where(b_wins, a, b)
    win_pos, wait_pos = jnp.where(b_wins, pos_a + m, pos_a), jnp.where(b_wins, pos_a, pos_a + m)
    vals, poss = [], []
    for _ in range(k):
        top = jnp.max(win, axis=0, keepdims=True)
        j = jnp.min(jnp.where(win == top, win_pos, big), axis=0, keepdims=True)
        sel = win_pos == j
        vals.append(top)
        poss.append(j)
        win = jnp.where(sel, wait, win)
        win_pos = jnp.where(sel, wait_pos, win_pos)
        wait = jnp.where(sel, -jnp.inf, wait)
    return jnp.concatenate(vals, axis=0), jnp.concatenate(poss, axis=0)


def _peer_route_kernel(h_ref, wqt_ref, keys_ref, idx_ref, gate_ref):
    half_dim = PEER_DKEY // 2
    hb = h_ref[...].astype(BF)
    n = hb.shape[0]
    iota16 = lax.broadcasted_iota(jnp.int32, (PEER_TOPK, n), 0).astype(F32)
    iota8 = lax.broadcasted_iota(jnp.int32, (SUBLANES, n), 0).astype(F32)
    qt = lax.dot_general(wqt_ref[...], hb, (((1,), (1,)), ((), ())), preferred_element_type=F32)
    idx_rows, gate_rows = [], []
    for hd in range(PEER_HEADS):
        tv, ti = [], []
        for half in range(2):
            r0 = hd * PEER_DKEY + half * half_dim
            s = jnp.dot(keys_ref[half], qt[r0:r0 + half_dim, :], preferred_element_type=F32)
            v, i = _topk_two_per_slot(s, PEER_TOPK)
            tv.append(v)
            ti.append(i)
        cs = [tv[0][0:1, :] + tv[1]]
        cp = [iota16]
        ce = [ti[0][0:1, :] * PEER_NKEYS + ti[1]]
        for a in range(1, SUBLANES):
            cs.append(tv[0][a:a + 1, :] + tv[1][0:SUBLANES, :])
            cp.append(iota8 + a * PEER_TOPK)
            ce.append(ti[0][a:a + 1, :] * PEER_NKEYS + ti[1][0:SUBLANES, :])
        cs.append(tv[0][SUBLANES:, :] + tv[1][0:1, :])
        cp.append((iota8 + SUBLANES) * PEER_TOPK)
        ce.append(ti[0][SUBLANES:, :] * PEER_NKEYS + ti[1][0:1, :])
        best_s, _, best_e = _extract_topk(jnp.concatenate(cs, axis=0), jnp.concatenate(cp, axis=0), PEER_TOPK,
                                          payload=jnp.concatenate(ce, axis=0))
        ex = jnp.exp(best_s - best_s[0:1, :])
        gate_rows.append(ex / jnp.sum(ex, axis=0, keepdims=True))
        idx_rows.append(best_e)
    idx_ref[...] = (jnp.concatenate(idx_rows, axis=0) * ROW_WORDS_SUBLANES).astype(jnp.int32).T
    gate_ref[...] = jnp.concatenate(gate_rows, axis=0).T


def _peer_route(h, wqt, keys):
    t = h.shape[0]
    rt = ROUTE_TOKENS
    pairs = pl.BlockSpec((rt, PEER_PAIRS), lambda i: (i, 0))
    return pl.pallas_call(
        _peer_route_kernel,
        out_shape=(jax.ShapeDtypeStruct((t, PEER_PAIRS), jnp.int32), jax.ShapeDtypeStruct((t, PEER_PAIRS), F32)),
        grid=(t // rt,),
        in_specs=[
            pl.BlockSpec((rt, D_MODEL), lambda i: (i, 0)),
            pl.BlockSpec((D_MODEL, D_MODEL), lambda i: (0, 0)),
            pl.BlockSpec((2, PEER_NKEYS, PEER_DKEY // 2), lambda i: (0, 0, 0)),
        ],
        out_specs=(pairs, pairs),
        compiler_params=_cparams(("parallel",)),
        name="peer_route",
    )(h, wqt, keys)


def _pack_table(u):
    ub = u.astype(jnp.bfloat16)
    half = D_MODEL // 2
    lo = lax.bitcast_convert_type(ub[:, :half], jnp.uint16).astype(jnp.uint32)
    hi = lax.bitcast_convert_type(ub[:, half:], jnp.uint16).astype(jnp.uint32)
    return (lo | (hi << 16)).reshape(u.shape[0] * ROW_WORDS_SUBLANES, LANES)


def _peer_sublayer(h, wq, keys, u, v, ln_g, ln_b):
    rows, gate = _peer_route(h, wq.T.astype(BF), keys)
    w = _peer_in(rows, h, gate, _pack_table(u))
    return _peer_out(rows, w, _pack_table(v), h, ln_g, ln_b)


def kernel(x, emb_ln_g, emb_ln_b, w_in, rg_conv_w, rg_conv_b, rg_wa, rg_ba, rg_wx, rg_bx, rg_lambda, ssm_conv_w, ssm_conv_b, ssm_dt_bias, ssm_a_log, ssm_d, ssm_norm_g, attn_sinks, w_out, ln1_g, ln1_b, peer_wq, peer_keys, peer_u, peer_v, ln2_g, ln2_b):
    bsz, seq, d = x.shape
    assert d == D_MODEL and seq % ROW_TILE == 0 and (bsz * seq) % ROUTE_TOKENS == 0
    h = _layer_norm(x.reshape(bsz * seq, d), emb_ln_g, emb_ln_b)
    for l in range(DEPTH):
        rg, z, xbc, dt, q, kv = _in_proj(h, _split_w_in(w_in[l]))
        y_a = _rg_lru(rg, seq, rg_conv_w[l], rg_conv_b[l], _block_diag(rg_wa[l]), rg_ba[l],
                      _block_diag(rg_wx[l]), rg_bx[l], rg_lambda[l])
        y_b = _ssd(z, xbc, dt, seq, ssm_conv_w[l], ssm_conv_b[l], ssm_dt_bias[l], ssm_a_log[l], ssm_d[l],
                   ssm_norm_g[l])
        y_c = _swa(q, kv, seq, attn_sinks[l])
        h = _out_proj(y_a, y_b, y_c, h, w_out[l], ln1_g[l], ln1_b[l])
        h = _peer_sublayer(h, peer_wq[l], peer_keys[l], peer_u[l], peer_v[l], ln2_g[l], ln2_b[l])
    return h.reshape(bsz, seq, d)
```

```python
import math

import jax
import jax.numpy as jnp
from jax import lax
from jax.experimental import pallas as pl
from jax.experimental.pallas import tpu as pltpu

D_MODEL = 1024
DEPTH = 2

W_LRU = D_MODEL // 2
LRU_BLOCKS = 8
LRU_C = 8.0
CONV_K = 4

SSM_HEAD_DIM = 64
SSM_D_INNER = D_MODEL
SSM_HEADS = SSM_D_INNER // SSM_HEAD_DIM
SSM_GROUPS = 2
SSM_STATE = 128
SSM_CHUNK = 128
SSM_CONV_CH = SSM_D_INNER + 2 * SSM_GROUPS * SSM_STATE
SSD_GROUP_W = SSM_D_INNER // SSM_GROUPS

ATTN_HEAD_DIM = 64
ATTN_Q_HEADS = (D_MODEL // 2) // ATTN_HEAD_DIM
ATTN_KV_HEADS = 2
ATTN_REP = ATTN_Q_HEADS // ATTN_KV_HEADS
ATTN_BLOCK = 128
ATTN_Q_W = ATTN_Q_HEADS * ATTN_HEAD_DIM
ATTN_KV_W = 2 * ATTN_KV_HEADS * ATTN_HEAD_DIM

MIX_WIDTH = W_LRU + SSM_D_INNER + ATTN_Q_W

PEER_HEADS = 8
PEER_NKEYS = 128
PEER_DKEY = 128
PEER_TOPK = 16
PEER_PAIRS = PEER_HEADS * PEER_TOPK

DN_ALPHA = (2 * DEPTH) ** 0.25
LN_EPS = 1e-5

SUBLANES = 8
LANES = 128
HALO = SUBLANES
ROW_WORDS_SUBLANES = D_MODEL // 2 // LANES
TOKEN_SUBLANES = D_MODEL // LANES

ROW_TILE = 512
PAIR_LANES = 2 * ROW_WORDS_SUBLANES
GROUP_PAIRS = 32
GROUP_K = GROUP_PAIRS * PAIR_LANES
GATHER_TOKENS = 128
ROUTE_TOKENS = 256
VMEM_LIMIT = 48 * 1024 * 1024
PEER_VMEM_LIMIT = 56 * 1024 * 1024

BF = jnp.bfloat16
F32 = jnp.float32

PROJ_SEGS = (2 * W_LRU, SSM_D_INNER, SSM_CONV_CH, LANES, ATTN_Q_W, ATTN_KV_W)


def _cparams(sem):
    return pltpu.CompilerParams(dimension_semantics=sem, vmem_limit_bytes=VMEM_LIMIT)


def _softplus(x):
    return jnp.maximum(x, 0.0) + jnp.log1p(jnp.exp(-jnp.abs(x)))


def _sigmoid(x):
    return 1.0 / (1.0 + jnp.exp(-x))


def _silu(x):
    return x * _sigmoid(x)


def _split3(x):
    hi = x.astype(BF)
    rest = x - hi.astype(F32)
    mid = rest.astype(BF)
    return hi, mid, (rest - mid.astype(F32)).astype(BF)


def _gelu_tanh(x):
    return 0.5 * x * (1.0 + jnp.tanh(math.sqrt(2.0 / math.pi) * (x + 0.044715 * (x * x * x))))


def _ln_math(x, g, b):
    mu = jnp.mean(x, axis=-1, keepdims=True)
    xc = x - mu
    var = jnp.mean(xc * xc, axis=-1, keepdims=True)
    return xc * lax.rsqrt(var + LN_EPS) * g + b


def _ln_kernel(a_ref, g_ref, b_ref, o_ref):
    o_ref[...] = _ln_math(a_ref[...], g_ref[...], b_ref[...])


def _res_ln_kernel(a_ref, r_ref, g_ref, b_ref, o_ref):
    o_ref[...] = _ln_math(DN_ALPHA * a_ref[...] + r_ref[...], g_ref[...], b_ref[...])


def _layer_norm(a, g, b, res=None):
    t, d = a.shape
    row = pl.BlockSpec((ROW_TILE, d), lambda i: (i, 0))
    vec = pl.BlockSpec((1, d), lambda i: (0, 0))
    if res is None:
        kern, args, specs = _ln_kernel, (a,), [row]
    else:
        kern, args, specs = _res_ln_kernel, (a, res), [row, row]
    return pl.pallas_call(
        kern, out_shape=jax.ShapeDtypeStruct((t, d), F32), grid=(t // ROW_TILE,),
        in_specs=specs + [vec, vec], out_specs=row, compiler_params=_cparams(("parallel",)),
        name="layer_norm",
    )(*args, g.reshape(1, d), b.reshape(1, d))


def _inproj_kernel(h_ref, *refs):
    n = len(PROJ_SEGS)
    hb = h_ref[...].astype(BF)
    for w_ref, o_ref in zip(refs[:n], refs[n:]):
        o_ref[...] = jnp.dot(hb, w_ref[...], preferred_element_type=F32)


def _in_proj(h, ws):
    t = h.shape[0]
    return pl.pallas_call(
        _inproj_kernel,
        out_shape=tuple(jax.ShapeDtypeStruct((t, w), F32) for w in PROJ_SEGS),
        grid=(t // ROW_TILE,),
        in_specs=[pl.BlockSpec((ROW_TILE, D_MODEL), lambda i: (i, 0))]
        + [pl.BlockSpec((D_MODEL, w), lambda i: (0, 0)) for w in PROJ_SEGS],
        out_specs=tuple(pl.BlockSpec((ROW_TILE, w), lambda i: (i, 0)) for w in PROJ_SEGS),
        compiler_params=_cparams(("parallel",)),
        name="in_proj",
    )(h, *ws)


def _split_w_in(w_in):
    w = w_in.astype(BF)
    c = [0]
    for width in (2 * W_LRU, SSM_D_INNER, SSM_CONV_CH, SSM_HEADS, ATTN_Q_W, ATTN_KV_W):
        c.append(c[-1] + width)
    segs = [w[:, c[i]:c[i + 1]] for i in range(6)]
    segs[3] = jnp.pad(segs[3], ((0, 0), (0, LANES - SSM_HEADS)))
    return tuple(segs)


def _shift_rows(x, s, fill):
    n = x.shape[0]
    if s % SUBLANES == 0:
        return jnp.concatenate([jnp.full((s,) + x.shape[1:], fill, x.dtype), x[:n - s]], axis=0)
    rolled = pltpu.roll(x, s, axis=0)
    row = lax.broadcasted_iota(jnp.int32, x.shape, 0)
    return jnp.where(row < s, fill, rolled)


def _causal_conv(x, halo_ref, w_ref, b_ref):
    n = x.shape[0]
    halo_ref[pl.ds(HALO, n), :] = x
    out = b_ref[...] + w_ref[CONV_K - 1:CONV_K, :] * x
    for s in range(1, CONV_K):
        out = out + w_ref[CONV_K - 1 - s:CONV_K - s, :] * halo_ref[pl.ds(HALO - s, n), :]
    halo_ref[pl.ds(0, HALO), :] = x[n - HALO:n]
    return out


def _rglru_kernel(rg_ref, cw_ref, cb_ref, wa_ref, ba_ref, wx_ref, bx_ref, lam_ref, o_ref, halo_ref, carry_ref):
    n = rg_ref.shape[0]

    @pl.when(pl.program_id(1) == 0)
    def _():
        halo_ref[pl.ds(0, HALO), :] = jnp.zeros((HALO, halo_ref.shape[1]), F32)
        carry_ref[...] = jnp.zeros_like(carry_ref)

    x = rg_ref[:, 0:W_LRU]
    xc = _causal_conv(x, halo_ref, cw_ref, cb_ref)
    xb = xc.astype(BF)
    r = _sigmoid(jnp.dot(xb, wa_ref[...], preferred_element_type=F32) + ba_ref[...])
    i = _sigmoid(jnp.dot(xb, wx_ref[...], preferred_element_type=F32) + bx_ref[...])
    log_a = (-LRU_C * r) * _softplus(-lam_ref[...])
    a = jnp.exp(log_a)
    u = jnp.sqrt(-jnp.tanh(log_a) * (a * a + 1.0)) * (i * xc)
    k = 1
    while k < n:
        u = a * _shift_rows(u, k, 0.0) + u
        a = a * _shift_rows(a, k, 1.0)
        k *= 2
    h = u + a * carry_ref[0:1, :]
    carry_ref[...] = jnp.broadcast_to(h[n - 1:n, :], carry_ref.shape)
    o_ref[...] = _gelu_tanh(rg_ref[:, W_LRU:2 * W_LRU]) * h


def _rg_lru(rg, seq, conv_w, conv_b, wa, ba, wx, bx, lam):
    t = rg.shape[0]
    nb = seq // ROW_TILE
    row = lambda b, j: (b * nb + j, 0)
    const = lambda b, j: (0, 0)
    vec = pl.BlockSpec((1, W_LRU), const)
    return pl.pallas_call(
        _rglru_kernel,
        out_shape=jax.ShapeDtypeStruct((t, W_LRU), F32),
        grid=(t // seq, nb),
        in_specs=[pl.BlockSpec((ROW_TILE, 2 * W_LRU), row),
                  pl.BlockSpec((CONV_K, W_LRU), const), vec,
                  pl.BlockSpec((W_LRU, W_LRU), const), vec,
                  pl.BlockSpec((W_LRU, W_LRU), const), vec, vec],
        out_specs=pl.BlockSpec((ROW_TILE, W_LRU), row),
        scratch_shapes=[pltpu.VMEM((HALO + ROW_TILE, W_LRU), F32), pltpu.VMEM((SUBLANES, W_LRU), F32)],
        compiler_params=_cparams(("parallel", "arbitrary")),
        name="rg_lru",
    )(rg, conv_w, conv_b.reshape(1, -1), wa, ba.reshape(1, -1), wx, bx.reshape(1, -1), lam.reshape(1, -1))


def _block_diag(w):
    nb, c, _ = w.shape
    eye = jnp.eye(nb, dtype=w.dtype)
    return (eye[:, None, :, None] * w[:, :, None, :]).reshape(nb * c, nb * c).astype(BF)


def _ssd_kernel(z_ref, xbc_ref, dt_ref, cw_ref, cb_ref, dtb_ref, alog_ref, dskip_ref, ng_ref, expand_ref,
                o_ref, halo_ref, state_ref):
    L = SSM_CHUNK

    @pl.when(pl.program_id(1) == 0)
    def _():
        halo_ref[pl.ds(0, HALO), :] = jnp.zeros((HALO, halo_ref.shape[1]), F32)
        state_ref[...] = jnp.zeros_like(state_ref)

    xbc = xbc_ref[...]
    conv = _silu(_causal_conv(xbc, halo_ref, cw_ref, cb_ref))
    xs = conv[:, 0:SSM_D_INNER]
    bm = conv[:, SSM_D_INNER:SSM_D_INNER + SSM_GROUPS * SSM_STATE]
    cm = conv[:, SSM_D_INNER + SSM_GROUPS * SSM_STATE:]

    dt = _softplus(dt_ref[...] + dtb_ref[...])
    da = dt * (-jnp.exp(alog_ref[...]))
    row = lax.broadcasted_iota(jnp.int32, (L, L), 0)
    col = lax.broadcasted_iota(jnp.int32, (L, L), 1)
    causal = col <= row
    a_cs = sum(jnp.dot(causal.astype(BF), part, preferred_element_type=F32) for part in _split3(da))
    a_cs_t = a_cs.T
    expand = expand_ref[...]
    dt_full = sum(jnp.dot(part, expand, preferred_element_type=F32) for part in _split3(dt))
    acs_full = sum(jnp.dot(part, expand, preferred_element_type=F32) for part in _split3(a_cs))
    alast_full = acs_full[L - 1:L, :]
    xdt = xs * dt_full
    xdt_b = xdt.astype(BF)
    xst_b = (xdt * jnp.exp(alast_full - acs_full)).astype(BF)
    left = lax.broadcasted_iota(jnp.int32, (L, LANES), 1) < SSM_HEAD_DIM

    y_parts = []
    for g in range(SSM_GROUPS):
        cg = cm[:, g * SSM_STATE:(g + 1) * SSM_STATE].astype(BF)
        bg = bm[:, g * SSM_STATE:(g + 1) * SSM_STATE].astype(BF)
        cb = lax.dot_general(cg, bg, (((1,), (1,)), ((), ())), preferred_element_type=F32)
        st = state_ref[:, g * SSD_GROUP_W:(g + 1) * SSD_GROUP_W]
        y_off = jnp.dot(cg, st.astype(BF), preferred_element_type=F32)
        for j in range(SSD_GROUP_W // LANES):
            h0 = g * (SSM_HEADS // SSM_GROUPS) + 2 * j
            ms = []
            for h in (h0, h0 + 1):
                seg = a_cs[:, h:h + 1] - a_cs_t[h:h + 1, :]
                ms.append((cb * jnp.exp(jnp.where(causal, seg, -jnp.inf))).astype(BF))
            c0 = h0 * SSM_HEAD_DIM
            x2 = xdt_b[:, c0:c0 + LANES]
            zero = jnp.zeros_like(x2)
            xblk = jnp.concatenate([jnp.where(left, x2, zero), jnp.where(left, zero, x2)], axis=0)
            y_parts.append(jnp.dot(jnp.concatenate(ms, axis=1), xblk, preferred_element_type=F32)
                           + y_off[:, j * LANES:(j + 1) * LANES] * jnp.exp(acs_full[:, c0:c0 + LANES]))
        new = lax.dot_general(bg, xst_b[:, g * SSD_GROUP_W:(g + 1) * SSD_GROUP_W], (((0,), (0,)), ((), ())),
                              preferred_element_type=F32)
        state_ref[:, g * SSD_GROUP_W:(g + 1) * SSD_GROUP_W] = (
            st * jnp.exp(alast_full[:, g * SSD_GROUP_W:(g + 1) * SSD_GROUP_W]) + new)
    y = jnp.concatenate(y_parts, axis=1) + dskip_ref[...] * xs
    y = y * _silu(z_ref[...])
    outs = []
    for g in range(SSM_GROUPS):
        yg = y[:, g * SSD_GROUP_W:(g + 1) * SSD_GROUP_W]
        outs.append(yg * lax.rsqrt(jnp.mean(yg * yg, axis=-1, keepdims=True) + LN_EPS))
    o_ref[...] = jnp.concatenate(outs, axis=1) * ng_ref[...]


def _ssd(z, xbc, dt, seq, conv_w, conv_b, dt_bias, a_log, d_skip, norm_g):
    t = z.shape[0]
    nc = seq // SSM_CHUNK
    row = lambda b, c: (b * nc + c, 0)
    const = lambda b, c: (0, 0)
    pad = LANES - SSM_HEADS
    expand = (jnp.arange(LANES)[:, None] == (jnp.arange(SSM_D_INNER)[None, :] // SSM_HEAD_DIM)).astype(BF)
    return pl.pallas_call(
        _ssd_kernel,
        out_shape=jax.ShapeDtypeStruct((t, SSM_D_INNER), F32),
        grid=(t // seq, nc),
        in_specs=[pl.BlockSpec((SSM_CHUNK, SSM_D_INNER), row),
                  pl.BlockSpec((SSM_CHUNK, SSM_CONV_CH), row),
                  pl.BlockSpec((SSM_CHUNK, LANES), row),
                  pl.BlockSpec((CONV_K, SSM_CONV_CH), const),
                  pl.BlockSpec((1, SSM_CONV_CH), const),
                  pl.BlockSpec((1, LANES), const),
                  pl.BlockSpec((1, LANES), const),
                  pl.BlockSpec((1, SSM_D_INNER), const),
                  pl.BlockSpec((1, SSM_D_INNER), const),
                  pl.BlockSpec((LANES, SSM_D_INNER), const)],
        out_specs=pl.BlockSpec((SSM_CHUNK, SSM_D_INNER), row),
        scratch_shapes=[pltpu.VMEM((HALO + SSM_CHUNK, SSM_CONV_CH), F32), pltpu.VMEM((SSM_STATE, SSM_D_INNER), F32)],
        compiler_params=_cparams(("parallel", "arbitrary")),
        name="ssd",
    )(z, xbc, dt, conv_w, conv_b.reshape(1, -1), jnp.pad(dt_bias, (0, pad)).reshape(1, -1),
      jnp.pad(a_log, (0, pad)).reshape(1, -1), jnp.repeat(d_skip, SSM_HEAD_DIM).reshape(1, -1),
      norm_g.reshape(1, -1), expand)


def _swa_kernel(sink_ref, q_ref, kv_ref, kvp_ref, o_ref):
    L = ATTN_BLOCK
    hd = ATTN_HEAD_DIM
    first = pl.program_id(1) == 0
    kv = jnp.concatenate([kvp_ref[...], kv_ref[...]], axis=0)
    kk = kv[:, 0:LANES]
    vv = kv[:, LANES:2 * LANES].astype(BF)
    lane_k = lax.broadcasted_iota(jnp.int32, (2 * L, LANES), 1)
    qi = lax.broadcasted_iota(jnp.int32, (2 * L, 2 * L), 0) % L
    kj = lax.broadcasted_iota(jnp.int32, (2 * L, 2 * L), 1)
    rel = qi + L - kj
    valid = (rel >= 0) & (rel < L) & (jnp.logical_not(first) | (kj >= L))
    top = lax.broadcasted_iota(jnp.int32, (2 * L, 1), 0) < L
    lane_o = lax.broadcasted_iota(jnp.int32, (L, LANES), 1)
    scale = hd ** -0.5
    for j in range(ATTN_Q_HEADS // 2):
        g = (2 * j) // ATTN_REP
        q2 = q_ref[:, j * LANES:(j + 1) * LANES]
        q2r = pltpu.roll(q2, hd, axis=1)
        kg = jnp.where((lane_k >= g * hd) & (lane_k < (g + 1) * hd), kk, 0.0).astype(BF)
        qa, qb = (q2, q2r) if g == 0 else (q2r, q2)
        qs = jnp.concatenate([qa, qb], axis=0).astype(BF)
        logits = lax.dot_general(qs, kg, (((1,), (1,)), ((), ())), preferred_element_type=F32) * scale
        logits = jnp.where(valid, logits, -jnp.inf)
        sink = jnp.where(top, sink_ref[2 * j], sink_ref[2 * j + 1])
        m = jnp.maximum(jnp.max(logits, axis=-1, keepdims=True), sink)
        p = jnp.exp(logits - m)
        probs = p / (jnp.sum(p, axis=-1, keepdims=True) + jnp.exp(sink - m))
        o = jnp.dot(probs.astype(BF), vv, preferred_element_type=F32)
        oa, ob = o[0:L], o[L:2 * L]
        if g == 0:
            out2 = jnp.where(lane_o < hd, oa, pltpu.roll(ob, hd, axis=1))
        else:
            out2 = jnp.where(lane_o < hd, pltpu.roll(oa, hd, axis=1), ob)
        o_ref[:, j * LANES:(j + 1) * LANES] = out2


def _swa(q, kv, seq, sinks):
    t = q.shape[0]
    nb = seq // ATTN_BLOCK
    return pl.pallas_call(
        _swa_kernel,
        out_shape=jax.ShapeDtypeStruct((t, ATTN_Q_W), F32),
        grid=(t // seq, nb),
        in_specs=[pl.BlockSpec(memory_space=pltpu.SMEM),
                  pl.BlockSpec((ATTN_BLOCK, ATTN_Q_W), lambda b, n: (b * nb + n, 0)),
                  pl.BlockSpec((ATTN_BLOCK, ATTN_KV_W), lambda b, n: (b * nb + n, 0)),
                  pl.BlockSpec((ATTN_BLOCK, ATTN_KV_W), lambda b, n: (b * nb + jnp.maximum(n - 1, 0), 0))],
        out_specs=pl.BlockSpec((ATTN_BLOCK, ATTN_Q_W), lambda b, n: (b * nb + n, 0)),
        compiler_params=_cparams(("parallel", "parallel")),
        name="swa",
    )(sinks, q, kv, kv)


def _outproj_kernel(ya_ref, yb_ref, yc_ref, h_ref, wa_ref, wb_ref, wc_ref, g_ref, b_ref, o_ref):
    mix = (jnp.dot(ya_ref[...].astype(BF), wa_ref[...], preferred_element_type=F32)
           + jnp.dot(yb_ref[...].astype(BF), wb_ref[...], preferred_element_type=F32)
           + jnp.dot(yc_ref[...].astype(BF), wc_ref[...], preferred_element_type=F32))
    o_ref[...] = _ln_math(DN_ALPHA * h_ref[...] + mix, g_ref[...], b_ref[...])


def _out_proj(ya, yb, yc, h, w_out, g, b):
    t = h.shape[0]
    w = w_out.astype(BF)
    row = lambda i: (i, 0)
    const = lambda i: (0, 0)
    c1, c2 = W_LRU, W_LRU + SSM_D_INNER
    return pl.pallas_call(
        _outproj_kernel,
        out_shape=jax.ShapeDtypeStruct((t, D_MODEL), F32),
        grid=(t // ROW_TILE,),
        in_specs=[pl.BlockSpec((ROW_TILE, W_LRU), row), pl.BlockSpec((ROW_TILE, SSM_D_INNER), row),
                  pl.BlockSpec((ROW_TILE, ATTN_Q_W), row), pl.BlockSpec((ROW_TILE, D_MODEL), row),
                  pl.BlockSpec((W_LRU, D_MODEL), const), pl.BlockSpec((SSM_D_INNER, D_MODEL), const),
                  pl.BlockSpec((ATTN_Q_W, D_MODEL), const),
                  pl.BlockSpec((1, D_MODEL), const), pl.BlockSpec((1, D_MODEL), const)],
        out_specs=pl.BlockSpec((ROW_TILE, D_MODEL), row),
        compiler_params=_cparams(("parallel",)),
        name="out_proj",
    )(ya, yb, yc, h, w[0:c1], w[c1:c2], w[c2:MIX_WIDTH], g.reshape(1, -1), b.reshape(1, -1))


def _gather_group(idx_ref, tbl_ref, t, g):
    rs = ROW_WORDS_SUBLANES
    pieces = []
    for q in range(GROUP_PAIRS // 2):
        p = g * GROUP_PAIRS + 2 * q
        if p % SUBLANES == 0:
            window = idx_ref.at[t, pl.ds(p, SUBLANES)]
        ra = pl.multiple_of(window[p % SUBLANES], rs)
        rb = pl.multiple_of(window[p % SUBLANES + 1], rs)
        pieces.append(jnp.concatenate([tbl_ref[pl.ds(ra, rs), :], tbl_ref[pl.ds(rb, rs), :]], axis=0))
    return pltpu.bitcast(jnp.concatenate(pieces, axis=0), BF)


def _chunk_of_column(col):
    j = col % PAIR_LANES
    return j // 2 + ROW_WORDS_SUBLANES * (j % 2)


def _split_hi_lo(x):
    hi = x.astype(BF)
    lo = (x - hi.astype(F32)).astype(BF)
    return hi, lo


def _peer_in_kernel(idx_ref, x_ref, gate_ref, tbl_ref, o_ref, m_ref):
    n = x_ref.shape[0]
    row = lax.broadcasted_iota(jnp.int32, (2 * SUBLANES, GROUP_K), 0) % SUBLANES
    keep = row == _chunk_of_column(lax.broadcasted_iota(jnp.int32, (2 * SUBLANES, GROUP_K), 1))
    for t in range(n):
        x_row = x_ref[pl.ds(t, 1), :]
        x8 = jnp.concatenate([x_row[:, c * LANES:(c + 1) * LANES] for c in range(TOKEN_SUBLANES)], axis=0)
        x_hi, x_lo = _split_hi_lo(x8)
        x16 = jnp.concatenate([x_hi, x_lo], axis=0)
        for g in range(PEER_PAIRS // GROUP_PAIRS):
            r = lax.dot_general(x16, _gather_group(idx_ref, tbl_ref, t, g), (((1,), (1,)), ((), ())),
                                preferred_element_type=F32)
            m_ref[pl.ds(t, 1), g * GROUP_K:(g + 1) * GROUP_K] = jnp.sum(jnp.where(keep, r, 0.0), axis=0,
                                                                        keepdims=True)
    wide = PEER_PAIRS * PAIR_LANES
    fold = (lax.broadcasted_iota(jnp.int32, (wide, PEER_PAIRS), 0) // PAIR_LANES
            == lax.broadcasted_iota(jnp.int32, (wide, PEER_PAIRS), 1)).astype(BF)
    m_hi, m_lo = _split_hi_lo(m_ref[...])
    a = jnp.dot(m_hi, fold, preferred_element_type=F32) + jnp.dot(m_lo, fold, preferred_element_type=F32)
    o_ref[...] = gate_ref[...] * (0.5 * a * (1.0 + lax.erf(a * (2.0 ** -0.5))))


def _peer_in(rows, x, gate, tbl):
    t = x.shape[0]
    tt = GATHER_TOKENS
    pairs = pl.BlockSpec((tt, PEER_PAIRS), lambda i: (i, 0))
    return pl.pallas_call(
        _peer_in_kernel,
        out_shape=jax.ShapeDtypeStruct((t, PEER_PAIRS), F32),
        grid=(t // tt,),
        in_specs=[
            pl.BlockSpec((tt, PEER_PAIRS), lambda i: (i, 0), memory_space=pltpu.SMEM),
            pl.BlockSpec((tt, D_MODEL), lambda i: (i, 0)),
            pairs,
            pl.BlockSpec(memory_space=pltpu.VMEM),
        ],
        out_specs=pairs,
        scratch_shapes=[pltpu.VMEM((tt, PEER_PAIRS * PAIR_LANES), F32)],
        compiler_params=pltpu.CompilerParams(dimension_semantics=("parallel",), vmem_limit_bytes=PEER_VMEM_LIMIT),
        name="peer_in",
    )(rows, x, gate, tbl)


def _peer_out_kernel(idx_ref, w_ref, tbl_ref, h_ref, g_ref, b_ref, o_ref, wide_ref, ffn_ref):
    n = w_ref.shape[0]
    wide = PEER_PAIRS * PAIR_LANES
    rep = (lax.broadcasted_iota(jnp.int32, (PEER_PAIRS, wide), 1) // PAIR_LANES
           == lax.broadcasted_iota(jnp.int32, (PEER_PAIRS, wide), 0)).astype(BF)
    for k, wv in enumerate(_split_hi_lo(w_ref[...])):
        wide_ref[k] = jnp.dot(wv, rep, preferred_element_type=F32)
    keep = (lax.broadcasted_iota(jnp.int32, (SUBLANES, GROUP_K), 0)
            == _chunk_of_column(lax.broadcasted_iota(jnp.int32, (SUBLANES, GROUP_K), 1)))
    for t in range(n):
        acc = jnp.zeros((2 * SUBLANES, LANES), F32)
        for g in range(PEER_PAIRS // GROUP_PAIRS):
            cols = slice(g * GROUP_K, (g + 1) * GROUP_K)
            lhs = jnp.concatenate(
                [jnp.where(keep, jnp.broadcast_to(wide_ref[k, pl.ds(t, 1), cols], (SUBLANES, GROUP_K)), 0.0)
                 for k in range(2)], axis=0).astype(BF)
            acc = acc + jnp.dot(lhs, _gather_group(idx_ref, tbl_ref, t, g), preferred_element_type=F32)
        out8 = acc[0:SUBLANES] + acc[SUBLANES:]
        for c in range(TOKEN_SUBLANES):
            ffn_ref[pl.ds(t, 1), c * LANES:(c + 1) * LANES] = out8[c:c + 1, :]
    o_ref[...] = _ln_math(DN_ALPHA * h_ref[...] + ffn_ref[...], g_ref[...], b_ref[...])


def _peer_out(rows, w, tbl, h, g, b):
    t = h.shape[0]
    tt = GATHER_TOKENS
    row = pl.BlockSpec((tt, D_MODEL), lambda i: (i, 0))
    vec = pl.BlockSpec((1, D_MODEL), lambda i: (0, 0))
    return pl.pallas_call(
        _peer_out_kernel,
        out_shape=jax.ShapeDtypeStruct((t, D_MODEL), F32),
        grid=(t // tt,),
        in_specs=[
            pl.BlockSpec((tt, PEER_PAIRS), lambda i: (i, 0), memory_space=pltpu.SMEM),
            pl.BlockSpec((tt, PEER_PAIRS), lambda i: (i, 0)),
            pl.BlockSpec(memory_space=pltpu.VMEM),
            row, vec, vec,
        ],
        out_specs=row,
        scratch_shapes=[pltpu.VMEM((2, tt, PEER_PAIRS * PAIR_LANES), F32), pltpu.VMEM((tt, D_MODEL), F32)],
        compiler_params=pltpu.CompilerParams(vmem_limit_bytes=PEER_VMEM_LIMIT),
        name="peer_out",
    )(rows, w, tbl, h, g.reshape(1, D_MODEL), b.reshape(1, D_MODEL))


def _extract_topk(s, pos, k, payload=None):
    big = jnp.asarray(2 ** 30, pos.dtype)
    vals, poss, pays = [], [], []
    for _ in range(k):
        m = jnp.max(s, axis=0, keepdims=True)
        j = jnp.min(jnp.where(s == m, pos, big), axis=0, keepdims=True)
        sel = pos == j
        vals.append(m)
        poss.append(j)
        if payload is not None:
            pays.append(jnp.sum(jnp.where(sel, payload, jnp.zeros_like(payload)), axis=0, keepdims=True))
        s = jnp.where(sel, -jnp.inf, s)
    out = [jnp.concatenate(vals, axis=0), jnp.concatenate(poss, axis=0)]
    if payload is not None:
        out.append(jnp.concatenate(pays, axis=0))
    return out


def _topk_two_per_slot(s, k):
    m, n = s.shape[0] // 2, s.shape[1]
    big = jnp.asarray(2 ** 30, F32)
    pos_a = lax.broadcasted_iota(jnp.int32, (m, n), 0).astype(F32)
    a, b = s[:m], s[m:]
    b_wins = b > a
    win, wait = jnp.where(b_wins, b, a), jnp.where(b_wins, a, b)
    win_pos, wait_pos = jnp.where(b_wins, pos_a + m, pos_a), jnp.where(b_wins, pos_a, pos_a + m)
    vals, poss = [], []
    for _ in range(k):
        top = jnp.max(win, axis=0, keepdims=True)
        j = jnp.min(jnp.where(win == top, win_pos, big), axis=0, keepdims=True)
        sel = win_pos == j
        vals.append(top)
        poss.append(j)
        win = jnp.where(sel, wait, win)
        win_pos = jnp.where(sel, wait_pos, win_pos)
        wait = jnp.where(sel, -jnp.inf, wait)
    return jnp.concatenate(vals, axis=0), jnp.concatenate(poss, axis=0)


def _peer_route_kernel(h_ref, wqt_ref, keys_ref, idx_ref, gate_ref):
    half_dim = PEER_DKEY // 2
    hb = h_ref[...].astype(BF)
    n = hb.shape[0]
    iota16 = lax.broadcasted_iota(jnp.int32, (PEER_TOPK, n), 0).astype(F32)
    iota8 = lax.broadcasted_iota(jnp.int32, (SUBLANES, n), 0).astype(F32)
    qt = lax.dot_general(wqt_ref[...], hb, (((1,), (1,)), ((), ())), preferred_element_type=F32)
    idx_rows, gate_rows = [], []
    for hd in range(PEER_HEADS):
        tv, ti = [], []
        for half in range(2):
            r0 = hd * PEER_DKEY + half * half_dim
            s = jnp.dot(keys_ref[half], qt[r0:r0 + half_dim, :], preferred_element_type=F32)
            v, i = _topk_two_per_slot(s, PEER_TOPK)
            tv.append(v)
            ti.append(i)
        v1_lo, i1_lo = tv[1][0:SUBLANES, :], ti[1][0:SUBLANES, :]

        def block(a, first_row):
            shift = (lambda x: x) if first_row == 0 else (lambda x: pltpu.roll(x, first_row, axis=0))
            return (tv[0][a:a + 1, :] + shift(v1_lo), iota8 + (a * PEER_TOPK - first_row),
                    ti[0][a:a + 1, :] * PEER_NKEYS + shift(i1_lo))

        def pick(cond, x, y):
            return tuple(jnp.where(cond, xi, yi) for xi, yi in zip(x, y))

        padding = (jnp.full_like(v1_lo, -jnp.inf), iota8 + 2 ** 20, jnp.zeros_like(i1_lo))
        tiles = [block(1, 0),
                 pick(iota8 < 5, block(2, 0), block(4, 5)),
                 pick(iota8 < 4, block(3, 0), pick(iota8 < 6, block(5, 4), block(6, 6))),
                 pick(iota8 < 2, block(7, 0), padding),
                 (tv[0][SUBLANES:, :] + tv[1][0:1, :], (iota8 + SUBLANES) * PEER_TOPK,
                  ti[0][SUBLANES:, :] * PEER_NKEYS + ti[1][0:1, :])]
        cs = [tv[0][0:1, :] + tv[1]] + [tile[0] for tile in tiles]
        cp = [iota16] + [tile[1] for tile in tiles]
        ce = [ti[0][0:1, :] * PEER_NKEYS + ti[1]] + [tile[2] for tile in tiles]
        best_s, _, best_e = _extract_topk(jnp.concatenate(cs, axis=0), jnp.concatenate(cp, axis=0), PEER_TOPK,
                                          payload=jnp.concatenate(ce, axis=0))
        ex = jnp.exp(best_s - best_s[0:1, :])
        gate_rows.append(ex / jnp.sum(ex, axis=0, keepdims=True))
        idx_rows.append(best_e)
    idx_ref[...] = (jnp.concatenate(idx_rows, axis=0) * ROW_WORDS_SUBLANES).astype(jnp.int32).T
    gate_ref[...] = jnp.concatenate(gate_rows, axis=0).T


def _peer_route(h, wqt, keys):
    t = h.shape[0]
    rt = ROUTE_TOKENS
    pairs = pl.BlockSpec((rt, PEER_PAIRS), lambda i: (i, 0))
    return pl.pallas_call(
        _peer_route_kernel,
        out_shape=(jax.ShapeDtypeStruct((t, PEER_PAIRS), jnp.int32), jax.ShapeDtypeStruct((t, PEER_PAIRS), F32)),
        grid=(t // rt,),
        in_specs=[
            pl.BlockSpec((rt, D_MODEL), lambda i: (i, 0)),
            pl.BlockSpec((D_MODEL, D_MODEL), lambda i: (0, 0)),
            pl.BlockSpec((2, PEER_NKEYS, PEER_DKEY // 2), lambda i: (0, 0, 0)),
        ],
        out_specs=(pairs, pairs),
        compiler_params=_cparams(("parallel",)),
        name="peer_route",
    )(h, wqt, keys)


def _pack_table(u):
    ub = u.astype(jnp.bfloat16)
    half = D_MODEL // 2
    lo = lax.bitcast_convert_type(ub[:, :half], jnp.uint16).astype(jnp.uint32)
    hi = lax.bitcast_convert_type(ub[:, half:], jnp.uint16).astype(jnp.uint32)
    return (lo | (hi << 16)).reshape(u.shape[0] * ROW_WORDS_SUBLANES, LANES)


def _peer_sublayer(h, wq, keys, u, v, ln_g, ln_b):
    rows, gate = _peer_route(h, wq.T.astype(BF), keys)
    w = _peer_in(rows, h, gate, _pack_table(u))
    return _peer_out(rows, w, _pack_table(v), h, ln_g, ln_b)


def kernel(x, emb_ln_g, emb_ln_b, w_in, rg_conv_w, rg_conv_b, rg_wa, rg_ba, rg_wx, rg_bx, rg_lambda, ssm_conv_w, ssm_conv_b, ssm_dt_bias, ssm_a_log, ssm_d, ssm_norm_g, attn_sinks, w_out, ln1_g, ln1_b, peer_wq, peer_keys, peer_u, peer_v, ln2_g, ln2_b):
    bsz, seq, d = x.shape
    assert d == D_MODEL and seq % ROW_TILE == 0 and (bsz * seq) % ROUTE_TOKENS == 0
    h = _layer_norm(x.reshape(bsz * seq, d), emb_ln_g, emb_ln_b)
    for l in range(DEPTH):
        rg, z, xbc, dt, q, kv = _in_proj(h, _split_w_in(w_in[l]))
        y_a = _rg_lru(rg, seq, rg_conv_w[l], rg_conv_b[l], _block_diag(rg_wa[l]), rg_ba[l],
                      _block_diag(rg_wx[l]), rg_bx[l], rg_lambda[l])
        y_b = _ssd(z, xbc, dt, seq, ssm_conv_w[l], ssm_conv_b[l], ssm_dt_bias[l], ssm_a_log[l], ssm_d[l],
                   ssm_norm_g[l])
        y_c = _swa(q, kv, seq, attn_sinks[l])
        h = _out_proj(y_a, y_b, y_c, h, w_out[l], ln1_g[l], ln1_b[l])
        h = _peer_sublayer(h, peer_wq[l], peer_keys[l], peer_u[l], peer_v[l], ln2_g[l], ln2_b[l])
    return h.reshape(bsz, seq, d)
```

```python
import math

import jax
import jax.numpy as jnp
from jax import lax
from jax.experimental import pallas as pl
from jax.experimental.pallas import tpu as pltpu

D_MODEL = 1024
DEPTH = 2

W_LRU = D_MODEL // 2
LRU_BLOCKS = 8
LRU_C = 8.0
CONV_K = 4

SSM_HEAD_DIM = 64
SSM_D_INNER = D_MODEL
SSM_HEADS = SSM_D_INNER // SSM_HEAD_DIM
SSM_GROUPS = 2
SSM_STATE = 128
SSM_CHUNK = 128
SSM_CONV_CH = SSM_D_INNER + 2 * SSM_GROUPS * SSM_STATE
SSD_GROUP_W = SSM_D_INNER // SSM_GROUPS

ATTN_HEAD_DIM = 64
ATTN_Q_HEADS = (D_MODEL // 2) // ATTN_HEAD_DIM
ATTN_KV_HEADS = 2
ATTN_REP = ATTN_Q_HEADS // ATTN_KV_HEADS
ATTN_BLOCK = 128
ATTN_STEP_BLOCKS = 2
ATTN_Q_W = ATTN_Q_HEADS * ATTN_HEAD_DIM
ATTN_KV_W = 2 * ATTN_KV_HEADS * ATTN_HEAD_DIM

MIX_WIDTH = W_LRU + SSM_D_INNER + ATTN_Q_W

PEER_HEADS = 8
PEER_NKEYS = 128
PEER_DKEY = 128
PEER_TOPK = 16
PEER_PAIRS = PEER_HEADS * PEER_TOPK

DN_ALPHA = (2 * DEPTH) ** 0.25
LN_EPS = 1e-5

SUBLANES = 8
LANES = 128
HALO = SUBLANES
ROW_WORDS_SUBLANES = D_MODEL // 2 // LANES
TOKEN_SUBLANES = D_MODEL // LANES

ROW_TILE = 512
PAIR_LANES = 2 * ROW_WORDS_SUBLANES
GROUP_PAIRS = 32
GROUP_K = GROUP_PAIRS * PAIR_LANES
GATHER_TOKENS = 128
ROUTE_TOKENS = 256
VMEM_LIMIT = 48 * 1024 * 1024
PEER_VMEM_LIMIT = 56 * 1024 * 1024

BF = jnp.bfloat16
F32 = jnp.float32

PROJ_SEGS = (2 * W_LRU, SSM_D_INNER, SSM_CONV_CH, LANES, ATTN_Q_W, ATTN_KV_W)


def _cparams(sem):
    return pltpu.CompilerParams(dimension_semantics=sem, vmem_limit_bytes=VMEM_LIMIT)


def _softplus(x):
    return jnp.maximum(x, 0.0) + jnp.log1p(jnp.exp(-jnp.abs(x)))


def _sigmoid(x):
    return 1.0 / (1.0 + jnp.exp(-x))


def _silu(x):
    return x * _sigmoid(x)


def _split3(x):
    hi = x.astype(BF)
    rest = x - hi.astype(F32)
    mid = rest.astype(BF)
    return hi, mid, (rest - mid.astype(F32)).astype(BF)


def _gelu_tanh(x):
    return 0.5 * x * (1.0 + jnp.tanh(math.sqrt(2.0 / math.pi) * (x + 0.044715 * (x * x * x))))


def _ln_math(x, g, b):
    mu = jnp.mean(x, axis=-1, keepdims=True)
    xc = x - mu
    var = jnp.mean(xc * xc, axis=-1, keepdims=True)
    return xc * lax.rsqrt(var + LN_EPS) * g + b


def _ln_kernel(a_ref, g_ref, b_ref, o_ref):
    o_ref[...] = _ln_math(a_ref[...], g_ref[...], b_ref[...])


def _res_ln_kernel(a_ref, r_ref, g_ref, b_ref, o_ref):
    o_ref[...] = _ln_math(DN_ALPHA * a_ref[...] + r_ref[...], g_ref[...], b_ref[...])


def _layer_norm(a, g, b, res=None):
    t, d = a.shape
    row = pl.BlockSpec((ROW_TILE, d), lambda i: (i, 0))
    vec = pl.BlockSpec((1, d), lambda i: (0, 0))
    if res is None:
        kern, args, specs = _ln_kernel, (a,), [row]
    else:
        kern, args, specs = _res_ln_kernel, (a, res), [row, row]
    return pl.pallas_call(
        kern, out_shape=jax.ShapeDtypeStruct((t, d), F32), grid=(t // ROW_TILE,),
        in_specs=specs + [vec, vec], out_specs=row, compiler_params=_cparams(("parallel",)),
        name="layer_norm",
    )(*args, g.reshape(1, d), b.reshape(1, d))


def _inproj_kernel(h_ref, *refs):
    n = len(PROJ_SEGS)
    hb = h_ref[...].astype(BF)
    for w_ref, o_ref in zip(refs[:n], refs[n:]):
        o_ref[...] = jnp.dot(hb, w_ref[...], preferred_element_type=F32)


def _in_proj(h, ws):
    t = h.shape[0]
    return pl.pallas_call(
        _inproj_kernel,
        out_shape=tuple(jax.ShapeDtypeStruct((t, w), F32) for w in PROJ_SEGS),
        grid=(t // ROW_TILE,),
        in_specs=[pl.BlockSpec((ROW_TILE, D_MODEL), lambda i: (i, 0))]
        + [pl.BlockSpec((D_MODEL, w), lambda i: (0, 0)) for w in PROJ_SEGS],
        out_specs=tuple(pl.BlockSpec((ROW_TILE, w), lambda i: (i, 0)) for w in PROJ_SEGS),
        compiler_params=_cparams(("parallel",)),
        name="in_proj",
    )(h, *ws)


def _split_w_in(w_in):
    w = w_in.astype(BF)
    c = [0]
    for width in (2 * W_LRU, SSM_D_INNER, SSM_CONV_CH, SSM_HEADS, ATTN_Q_W, ATTN_KV_W):
        c.append(c[-1] + width)
    segs = [w[:, c[i]:c[i + 1]] for i in range(6)]
    segs[3] = jnp.pad(segs[3], ((0, 0), (0, LANES - SSM_HEADS)))
    return tuple(segs)


def _shift_rows(x, s, fill):
    n = x.shape[0]
    if s % SUBLANES == 0:
        return jnp.concatenate([jnp.full((s,) + x.shape[1:], fill, x.dtype), x[:n - s]], axis=0)
    rolled = pltpu.roll(x, s, axis=0)
    row = lax.broadcasted_iota(jnp.int32, x.shape, 0)
    return jnp.where(row < s, fill, rolled)


def _causal_conv(x, halo_ref, w_ref, b_ref):
    n = x.shape[0]
    halo_ref[pl.ds(HALO, n), :] = x
    out = b_ref[...] + w_ref[CONV_K - 1:CONV_K, :] * x
    for s in range(1, CONV_K):
        out = out + w_ref[CONV_K - 1 - s:CONV_K - s, :] * halo_ref[pl.ds(HALO - s, n), :]
    halo_ref[pl.ds(0, HALO), :] = x[n - HALO:n]
    return out


def _rglru_kernel(rg_ref, cw_ref, cb_ref, wa_ref, ba_ref, wx_ref, bx_ref, lam_ref, o_ref, halo_ref, carry_ref):
    n = rg_ref.shape[0]

    @pl.when(pl.program_id(1) == 0)
    def _():
        halo_ref[pl.ds(0, HALO), :] = jnp.zeros((HALO, halo_ref.shape[1]), F32)
        carry_ref[...] = jnp.zeros_like(carry_ref)

    x = rg_ref[:, 0:W_LRU]
    xc = _causal_conv(x, halo_ref, cw_ref, cb_ref)
    xb = xc.astype(BF)
    r = _sigmoid(jnp.dot(xb, wa_ref[...], preferred_element_type=F32) + ba_ref[...])
    i = _sigmoid(jnp.dot(xb, wx_ref[...], preferred_element_type=F32) + bx_ref[...])
    log_a = (-LRU_C * r) * _softplus(-lam_ref[...])
    a = jnp.exp(log_a)
    u = jnp.sqrt(-jnp.tanh(log_a) * (a * a + 1.0)) * (i * xc)
    k = 1
    while k < n:
        u = a * _shift_rows(u, k, 0.0) + u
        a = a * _shift_rows(a, k, 1.0)
        k *= 2
    h = u + a * carry_ref[0:1, :]
    carry_ref[...] = jnp.broadcast_to(h[n - 1:n, :], carry_ref.shape)
    o_ref[...] = _gelu_tanh(rg_ref[:, W_LRU:2 * W_LRU]) * h


def _rg_lru(rg, seq, conv_w, conv_b, wa, ba, wx, bx, lam):
    t = rg.shape[0]
    nb = seq // ROW_TILE
    row = lambda b, j: (b * nb + j, 0)
    const = lambda b, j: (0, 0)
    vec = pl.BlockSpec((1, W_LRU), const)
    return pl.pallas_call(
        _rglru_kernel,
        out_shape=jax.ShapeDtypeStruct((t, W_LRU), F32),
        grid=(t // seq, nb),
        in_specs=[pl.BlockSpec((ROW_TILE, 2 * W_LRU), row),
                  pl.BlockSpec((CONV_K, W_LRU), const), vec,
                  pl.BlockSpec((W_LRU, W_LRU), const), vec,
                  pl.BlockSpec((W_LRU, W_LRU), const), vec, vec],
        out_specs=pl.BlockSpec((ROW_TILE, W_LRU), row),
        scratch_shapes=[pltpu.VMEM((HALO + ROW_TILE, W_LRU), F32), pltpu.VMEM((SUBLANES, W_LRU), F32)],
        compiler_params=_cparams(("parallel", "arbitrary")),
        name="rg_lru",
    )(rg, conv_w, conv_b.reshape(1, -1), wa, ba.reshape(1, -1), wx, bx.reshape(1, -1), lam.reshape(1, -1))


def _block_diag(w):
    nb, c, _ = w.shape
    eye = jnp.eye(nb, dtype=w.dtype)
    return (eye[:, None, :, None] * w[:, :, None, :]).reshape(nb * c, nb * c).astype(BF)


def _ssd_kernel(z_ref, xbc_ref, dt_ref, cw_ref, cb_ref, dtb_ref, alog_ref, dskip_ref, ng_ref, expand_ref,
                o_ref, halo_ref, state_ref):
    L = SSM_CHUNK

    @pl.when(pl.program_id(1) == 0)
    def _():
        halo_ref[pl.ds(0, HALO), :] = jnp.zeros((HALO, halo_ref.shape[1]), F32)
        state_ref[...] = jnp.zeros_like(state_ref)

    xbc = xbc_ref[...]
    conv = _silu(_causal_conv(xbc, halo_ref, cw_ref, cb_ref))
    xs = conv[:, 0:SSM_D_INNER]
    bm = conv[:, SSM_D_INNER:SSM_D_INNER + SSM_GROUPS * SSM_STATE]
    cm = conv[:, SSM_D_INNER + SSM_GROUPS * SSM_STATE:]

    dt = _softplus(dt_ref[...] + dtb_ref[...])
    da = dt * (-jnp.exp(alog_ref[...]))
    row = lax.broadcasted_iota(jnp.int32, (L, L), 0)
    col = lax.broadcasted_iota(jnp.int32, (L, L), 1)
    causal = col <= row
    a_cs = sum(jnp.dot(causal.astype(BF), part, preferred_element_type=F32) for part in _split3(da))
    a_cs_t = a_cs.T
    expand = expand_ref[...]
    dt_full = sum(jnp.dot(part, expand, preferred_element_type=F32) for part in _split3(dt))
    acs_full = sum(jnp.dot(part, expand, preferred_element_type=F32) for part in _split3(a_cs))
    alast_full = acs_full[L - 1:L, :]
    xdt = xs * dt_full
    xdt_b = xdt.astype(BF)
    xst_b = (xdt * jnp.exp(alast_full - acs_full)).astype(BF)
    left = lax.broadcasted_iota(jnp.int32, (L, LANES), 1) < SSM_HEAD_DIM

    y_parts = []
    for g in range(SSM_GROUPS):
        cg = cm[:, g * SSM_STATE:(g + 1) * SSM_STATE].astype(BF)
        bg = bm[:, g * SSM_STATE:(g + 1) * SSM_STATE].astype(BF)
        cb = lax.dot_general(cg, bg, (((1,), (1,)), ((), ())), preferred_element_type=F32)
        st = state_ref[:, g * SSD_GROUP_W:(g + 1) * SSD_GROUP_W]
        y_off = jnp.dot(cg, st.astype(BF), preferred_element_type=F32)
        for j in range(SSD_GROUP_W // LANES):
            h0 = g * (SSM_HEADS // SSM_GROUPS) + 2 * j
            ms = []
            for h in (h0, h0 + 1):
                seg = a_cs[:, h:h + 1] - a_cs_t[h:h + 1, :]
                ms.append((cb * jnp.exp(jnp.where(causal, seg, -jnp.inf))).astype(BF))
            c0 = h0 * SSM_HEAD_DIM
            x2 = xdt_b[:, c0:c0 + LANES]
            zero = jnp.zeros_like(x2)
            xblk = jnp.concatenate([jnp.where(left, x2, zero), jnp.where(left, zero, x2)], axis=0)
            y_parts.append(jnp.dot(jnp.concatenate(ms, axis=1), xblk, preferred_element_type=F32)
                           + y_off[:, j * LANES:(j + 1) * LANES] * jnp.exp(acs_full[:, c0:c0 + LANES]))
        new = lax.dot_general(bg, xst_b[:, g * SSD_GROUP_W:(g + 1) * SSD_GROUP_W], (((0,), (0,)), ((), ())),
                              preferred_element_type=F32)
        state_ref[:, g * SSD_GROUP_W:(g + 1) * SSD_GROUP_W] = (
            st * jnp.exp(alast_full[:, g * SSD_GROUP_W:(g + 1) * SSD_GROUP_W]) + new)
    y = jnp.concatenate(y_parts, axis=1) + dskip_ref[...] * xs
    y = y * _silu(z_ref[...])
    outs = []
    for g in range(SSM_GROUPS):
        yg = y[:, g * SSD_GROUP_W:(g + 1) * SSD_GROUP_W]
        outs.append(yg * lax.rsqrt(jnp.mean(yg * yg, axis=-1, keepdims=True) + LN_EPS))
    o_ref[...] = jnp.concatenate(outs, axis=1) * ng_ref[...]


def _ssd(z, xbc, dt, seq, conv_w, conv_b, dt_bias, a_log, d_skip, norm_g):
    t = z.shape[0]
    nc = seq // SSM_CHUNK
    row = lambda b, c: (b * nc + c, 0)
    const = lambda b, c: (0, 0)
    pad = LANES - SSM_HEADS
    expand = (jnp.arange(LANES)[:, None] == (jnp.arange(SSM_D_INNER)[None, :] // SSM_HEAD_DIM)).astype(BF)
    return pl.pallas_call(
        _ssd_kernel,
        out_shape=jax.ShapeDtypeStruct((t, SSM_D_INNER), F32),
        grid=(t // seq, nc),
        in_specs=[pl.BlockSpec((SSM_CHUNK, SSM_D_INNER), row),
                  pl.BlockSpec((SSM_CHUNK, SSM_CONV_CH), row),
                  pl.BlockSpec((SSM_CHUNK, LANES), row),
                  pl.BlockSpec((CONV_K, SSM_CONV_CH), const),
                  pl.BlockSpec((1, SSM_CONV_CH), const),
                  pl.BlockSpec((1, LANES), const),
                  pl.BlockSpec((1, LANES), const),
                  pl.BlockSpec((1, SSM_D_INNER), const),
                  pl.BlockSpec((1, SSM_D_INNER), const),
                  pl.BlockSpec((LANES, SSM_D_INNER), const)],
        out_specs=pl.BlockSpec((SSM_CHUNK, SSM_D_INNER), row),
        scratch_shapes=[pltpu.VMEM((HALO + SSM_CHUNK, SSM_CONV_CH), F32), pltpu.VMEM((SSM_STATE, SSM_D_INNER), F32)],
        compiler_params=_cparams(("parallel", "arbitrary")),
        name="ssd",
    )(z, xbc, dt, conv_w, conv_b.reshape(1, -1), jnp.pad(dt_bias, (0, pad)).reshape(1, -1),
      jnp.pad(a_log, (0, pad)).reshape(1, -1), jnp.repeat(d_skip, SSM_HEAD_DIM).reshape(1, -1),
      norm_g.reshape(1, -1), expand)


def _swa_kernel(sink_ref, q_ref, kv_ref, kvp_ref, o_ref):
    L = ATTN_BLOCK
    lane_k = lax.broadcasted_iota(jnp.int32, (2 * L, LANES), 1)
    qi = lax.broadcasted_iota(jnp.int32, (2 * L, 2 * L), 0) % L
    kj = lax.broadcasted_iota(jnp.int32, (2 * L, 2 * L), 1)
    rel = qi + L - kj
    band = (rel >= 0) & (rel < L)
    for blk in range(ATTN_STEP_BLOCKS):
        rows = pl.ds(blk * L, L)
        prev = kvp_ref[...] if blk == 0 else kv_ref[pl.ds((blk - 1) * L, L), :]
        kv = jnp.concatenate([prev, kv_ref[rows, :]], axis=0)
        valid = band & (jnp.logical_not(pl.program_id(1) == 0) | (kj >= L)) if blk == 0 else band
        _swa_block(sink_ref, q_ref, o_ref, rows, kv, valid, lane_k)


def _swa_block(sink_ref, q_ref, o_ref, rows, kv, valid, lane_k):
    L = ATTN_BLOCK
    hd = ATTN_HEAD_DIM
    kk = kv[:, 0:LANES]
    vv = kv[:, LANES:2 * LANES].astype(BF)
    top = lax.broadcasted_iota(jnp.int32, (2 * L, 1), 0) < L
    lane_o = lax.broadcasted_iota(jnp.int32, (L, LANES), 1)
    scale = hd ** -0.5
    for j in range(ATTN_Q_HEADS // 2):
        g = (2 * j) // ATTN_REP
        q2 = q_ref[rows, j * LANES:(j + 1) * LANES]
        q2r = pltpu.roll(q2, hd, axis=1)
        kg = jnp.where((lane_k >= g * hd) & (lane_k < (g + 1) * hd), kk, 0.0).astype(BF)
        qa, qb = (q2, q2r) if g == 0 else (q2r, q2)
        qs = jnp.concatenate([qa, qb], axis=0).astype(BF)
        logits = lax.dot_general(qs, kg, (((1,), (1,)), ((), ())), preferred_element_type=F32) * scale
        logits = jnp.where(valid, logits, -jnp.inf)
        sink = jnp.where(top, sink_ref[2 * j], sink_ref[2 * j + 1])
        m = jnp.maximum(jnp.max(logits, axis=-1, keepdims=True), sink)
        p = jnp.exp(logits - m)
        probs = p * (1.0 / (jnp.sum(p, axis=-1, keepdims=True) + jnp.exp(sink - m)))
        o = jnp.dot(probs.astype(BF), vv, preferred_element_type=F32)
        oa, ob = o[0:L], o[L:2 * L]
        if g == 0:
            out2 = jnp.where(lane_o < hd, oa, pltpu.roll(ob, hd, axis=1))
        else:
            out2 = jnp.where(lane_o < hd, pltpu.roll(oa, hd, axis=1), ob)
        o_ref[rows, j * LANES:(j + 1) * LANES] = out2


def _swa(q, kv, seq, sinks):
    t = q.shape[0]
    sb = ATTN_STEP_BLOCKS
    rows = sb * ATTN_BLOCK
    ns = seq // rows
    tile = lambda b, n: (b * ns + n, 0)
    before = lambda b, n: (b * ns * sb + jnp.maximum(n * sb - 1, 0), 0)
    return pl.pallas_call(
        _swa_kernel,
        out_shape=jax.ShapeDtypeStruct((t, ATTN_Q_W), F32),
        grid=(t // seq, ns),
        in_specs=[pl.BlockSpec(memory_space=pltpu.SMEM),
                  pl.BlockSpec((rows, ATTN_Q_W), tile),
                  pl.BlockSpec((rows, ATTN_KV_W), tile),
                  pl.BlockSpec((ATTN_BLOCK, ATTN_KV_W), before)],
        out_specs=pl.BlockSpec((rows, ATTN_Q_W), tile),
        compiler_params=_cparams(("parallel", "parallel")),
        name="swa",
    )(sinks, q, kv, kv)


def _outproj_kernel(ya_ref, yb_ref, yc_ref, h_ref, wa_ref, wb_ref, wc_ref, g_ref, b_ref, o_ref):
    mix = (jnp.dot(ya_ref[...].astype(BF), wa_ref[...], preferred_element_type=F32)
           + jnp.dot(yb_ref[...].astype(BF), wb_ref[...], preferred_element_type=F32)
           + jnp.dot(yc_ref[...].astype(BF), wc_ref[...], preferred_element_type=F32))
    o_ref[...] = _ln_math(DN_ALPHA * h_ref[...] + mix, g_ref[...], b_ref[...])


def _out_proj(ya, yb, yc, h, w_out, g, b):
    t = h.shape[0]
    w = w_out.astype(BF)
    row = lambda i: (i, 0)
    const = lambda i: (0, 0)
    c1, c2 = W_LRU, W_LRU + SSM_D_INNER
    return pl.pallas_call(
        _outproj_kernel,
        out_shape=jax.ShapeDtypeStruct((t, D_MODEL), F32),
        grid=(t // ROW_TILE,),
        in_specs=[pl.BlockSpec((ROW_TILE, W_LRU), row), pl.BlockSpec((ROW_TILE, SSM_D_INNER), row),
                  pl.BlockSpec((ROW_TILE, ATTN_Q_W), row), pl.BlockSpec((ROW_TILE, D_MODEL), row),
                  pl.BlockSpec((W_LRU, D_MODEL), const), pl.BlockSpec((SSM_D_INNER, D_MODEL), const),
                  pl.BlockSpec((ATTN_Q_W, D_MODEL), const),
                  pl.BlockSpec((1, D_MODEL), const), pl.BlockSpec((1, D_MODEL), const)],
        out_specs=pl.BlockSpec((ROW_TILE, D_MODEL), row),
        compiler_params=_cparams(("parallel",)),
        name="out_proj",
    )(ya, yb, yc, h, w[0:c1], w[c1:c2], w[c2:MIX_WIDTH], g.reshape(1, -1), b.reshape(1, -1))


def _gather_group(idx_ref, tbl_ref, t, g):
    rs = ROW_WORDS_SUBLANES
    pieces = []
    for q in range(GROUP_PAIRS // 2):
        p = g * GROUP_PAIRS + 2 * q
        if p % SUBLANES == 0:
            window = idx_ref.at[t, pl.ds(p, SUBLANES)]
        ra = pl.multiple_of(window[p % SUBLANES], rs)
        rb = pl.multiple_of(window[p % SUBLANES + 1], rs)
        pieces.append(jnp.concatenate([tbl_ref[pl.ds(ra, rs), :], tbl_ref[pl.ds(rb, rs), :]], axis=0))
    return pltpu.bitcast(jnp.concatenate(pieces, axis=0), BF)


def _chunk_of_column(col):
    j = col % PAIR_LANES
    return j // 2 + ROW_WORDS_SUBLANES * (j % 2)


def _split_hi_lo(x):
    hi = x.astype(BF)
    lo = (x - hi.astype(F32)).astype(BF)
    return hi, lo


def _peer_in_kernel(idx_ref, x_ref, gate_ref, tbl_ref, o_ref, m_ref):
    n = x_ref.shape[0]
    row = lax.broadcasted_iota(jnp.int32, (2 * SUBLANES, GROUP_K), 0) % SUBLANES
    keep = row == _chunk_of_column(lax.broadcasted_iota(jnp.int32, (2 * SUBLANES, GROUP_K), 1))
    for t in range(n):
        x_row = x_ref[pl.ds(t, 1), :]
        x8 = jnp.concatenate([x_row[:, c * LANES:(c + 1) * LANES] for c in range(TOKEN_SUBLANES)], axis=0)
        x_hi, x_lo = _split_hi_lo(x8)
        x16 = jnp.concatenate([x_hi, x_lo], axis=0)
        for g in range(PEER_PAIRS // GROUP_PAIRS):
            r = lax.dot_general(x16, _gather_group(idx_ref, tbl_ref, t, g), (((1,), (1,)), ((), ())),
                                preferred_element_type=F32)
            m_ref[pl.ds(t, 1), g * GROUP_K:(g + 1) * GROUP_K] = jnp.sum(jnp.where(keep, r, 0.0), axis=0,
                                                                        keepdims=True)
    wide = PEER_PAIRS * PAIR_LANES
    fold = (lax.broadcasted_iota(jnp.int32, (wide, PEER_PAIRS), 0) // PAIR_LANES
            == lax.broadcasted_iota(jnp.int32, (wide, PEER_PAIRS), 1)).astype(BF)
    m_hi, m_lo = _split_hi_lo(m_ref[...])
    a = jnp.dot(m_hi, fold, preferred_element_type=F32) + jnp.dot(m_lo, fold, preferred_element_type=F32)
    o_ref[...] = gate_ref[...] * (0.5 * a * (1.0 + lax.erf(a * (2.0 ** -0.5))))


def _peer_in(rows, x, gate, tbl):
    t = x.shape[0]
    tt = GATHER_TOKENS
    pairs = pl.BlockSpec((tt, PEER_PAIRS), lambda i: (i, 0))
    return pl.pallas_call(
        _peer_in_kernel,
        out_shape=jax.ShapeDtypeStruct((t, PEER_PAIRS), F32),
        grid=(t // tt,),
        in_specs=[
            pl.BlockSpec((tt, PEER_PAIRS), lambda i: (i, 0), memory_space=pltpu.SMEM),
            pl.BlockSpec((tt, D_MODEL), lambda i: (i, 0)),
            pairs,
            pl.BlockSpec(memory_space=pltpu.VMEM),
        ],
        out_specs=pairs,
        scratch_shapes=[pltpu.VMEM((tt, PEER_PAIRS * PAIR_LANES), F32)],
        compiler_params=pltpu.CompilerParams(dimension_semantics=("parallel",), vmem_limit_bytes=PEER_VMEM_LIMIT),
        name="peer_in",
    )(rows, x, gate, tbl)


def _peer_out_kernel(idx_ref, w_ref, tbl_ref, h_ref, g_ref, b_ref, o_ref, wide_ref, ffn_ref):
    n = w_ref.shape[0]
    wide = PEER_PAIRS * PAIR_LANES
    rep = (lax.broadcasted_iota(jnp.int32, (PEER_PAIRS, wide), 1) // PAIR_LANES
           == lax.broadcasted_iota(jnp.int32, (PEER_PAIRS, wide), 0)).astype(BF)
    for k, wv in enumerate(_split_hi_lo(w_ref[...])):
        wide_ref[k] = jnp.dot(wv, rep, preferred_element_type=F32)
    keep = (lax.broadcasted_iota(jnp.int32, (SUBLANES, GROUP_K), 0)
            == _chunk_of_column(lax.broadcasted_iota(jnp.int32, (SUBLANES, GROUP_K), 1)))
    for t in range(n):
        acc = jnp.zeros((2 * SUBLANES, LANES), F32)
        for g in range(PEER_PAIRS // GROUP_PAIRS):
            cols = slice(g * GROUP_K, (g + 1) * GROUP_K)
            lhs = jnp.concatenate(
                [jnp.where(keep, jnp.broadcast_to(wide_ref[k, pl.ds(t, 1), cols], (SUBLANES, GROUP_K)), 0.0)
                 for k in range(2)], axis=0).astype(BF)
            acc = acc + jnp.dot(lhs, _gather_group(idx_ref, tbl_ref, t, g), preferred_element_type=F32)
        out8 = acc[0:SUBLANES] + acc[SUBLANES:]
        for c in range(TOKEN_SUBLANES):
            ffn_ref[pl.ds(t, 1), c * LANES:(c + 1) * LANES] = out8[c:c + 1, :]
    o_ref[...] = _ln_math(DN_ALPHA * h_ref[...] + ffn_ref[...], g_ref[...], b_ref[...])


def _peer_out(rows, w, tbl, h, g, b):
    t = h.shape[0]
    tt = GATHER_TOKENS
    row = pl.BlockSpec((tt, D_MODEL), lambda i: (i, 0))
    vec = pl.BlockSpec((1, D_MODEL), lambda i: (0, 0))
    return pl.pallas_call(
        _peer_out_kernel,
        out_shape=jax.ShapeDtypeStruct((t, D_MODEL), F32),
        grid=(t // tt,),
        in_specs=[
            pl.BlockSpec((tt, PEER_PAIRS), lambda i: (i, 0), memory_space=pltpu.SMEM),
            pl.BlockSpec((tt, PEER_PAIRS), lambda i: (i, 0)),
            pl.BlockSpec(memory_space=pltpu.VMEM),
            row, vec, vec,
        ],
        out_specs=row,
        scratch_shapes=[pltpu.VMEM((2, tt, PEER_PAIRS * PAIR_LANES), F32), pltpu.VMEM((tt, D_MODEL), F32)],
        compiler_params=pltpu.CompilerParams(vmem_limit_bytes=PEER_VMEM_LIMIT),
        name="peer_out",
    )(rows, w, tbl, h, g.reshape(1, D_MODEL), b.reshape(1, D_MODEL))


def _extract_topk(s, pos, k, payload=None):
    big = jnp.asarray(2 ** 30, pos.dtype)
    vals, poss, pays = [], [], []
    for _ in range(k):
        m = jnp.max(s, axis=0, keepdims=True)
        j = jnp.min(jnp.where(s == m, pos, big), axis=0, keepdims=True)
        sel = pos == j
        vals.append(m)
        poss.append(j)
        if payload is not None:
            pays.append(jnp.sum(jnp.where(sel, payload, jnp.zeros_like(payload)), axis=0, keepdims=True))
        s = jnp.where(sel, -jnp.inf, s)
    out = [jnp.concatenate(vals, axis=0), jnp.concatenate(poss, axis=0)]
    if payload is not None:
        out.append(jnp.concatenate(pays, axis=0))
    return out


def _topk_two_per_slot(s, k):
    m, n = s.shape[0] // 2, s.shape[1]
    big = jnp.asarray(2 ** 30, F32)
    pos_a = lax.broadcasted_iota(jnp.int32, (m, n), 0).astype(F32)
    a, b = s[:m], s[m:]
    b_wins = b > a
    win, wait = jnp.where(b_wins, b, a), jnp.where(b_wins, a, b)
    win_pos, wait_pos = jnp.where(b_wins, pos_a + m, pos_a), jnp.where(b_wins, pos_a, pos_a + m)
    vals, poss = [], []
    for _ in range(k):
        top = jnp.max(win, axis=0, keepdims=True)
        j = jnp.min(jnp.where(win == top, win_pos, big), axis=0, keepdims=True)
        sel = win_pos == j
        vals.append(top)
        poss.append(j)
        win = jnp.where(sel, wait, win)
        win_pos = jnp.where(sel, wait_pos, win_pos)
        wait = jnp.where(sel, -jnp.inf, wait)
    return jnp.concatenate(vals, axis=0), jnp.concatenate(poss, axis=0)


def _peer_route_kernel(h_ref, wqt_ref, keys_ref, idx_ref, gate_ref):
    half_dim = PEER_DKEY // 2
    hb = h_ref[...].astype(BF)
    n = hb.shape[0]
    iota16 = lax.broadcasted_iota(jnp.int32, (PEER_TOPK, n), 0).astype(F32)
    iota8 = lax.broadcasted_iota(jnp.int32, (SUBLANES, n), 0).astype(F32)
    qt = lax.dot_general(wqt_ref[...], hb, (((1,), (1,)), ((), ())), preferred_element_type=F32)
    idx_rows, gate_rows = [], []
    for hd in range(PEER_HEADS):
        tv, ti = [], []
        for half in range(2):
            r0 = hd * PEER_DKEY + half * half_dim
            s = jnp.dot(keys_ref[half], qt[r0:r0 + half_dim, :], preferred_element_type=F32)
            v, i = _topk_two_per_slot(s, PEER_TOPK)
            tv.append(v)
            ti.append(i)
        v1_lo, i1_lo = tv[1][0:SUBLANES, :], ti[1][0:SUBLANES, :]

        def block(a, first_row):
            shift = (lambda x: x) if first_row == 0 else (lambda x: pltpu.roll(x, first_row, axis=0))
            return (tv[0][a:a + 1, :] + shift(v1_lo), iota8 + (a * PEER_TOPK - first_row),
                    ti[0][a:a + 1, :] * PEER_NKEYS + shift(i1_lo))

        def pick(cond, x, y):
            return tuple(jnp.where(cond, xi, yi) for xi, yi in zip(x, y))

        padding = (jnp.full_like(v1_lo, -jnp.inf), iota8 + 2 ** 20, jnp.zeros_like(i1_lo))
        tiles = [block(1, 0),
                 pick(iota8 < 5, block(2, 0), block(4, 5)),
                 pick(iota8 < 4, block(3, 0), pick(iota8 < 6, block(5, 4), block(6, 6))),
                 pick(iota8 < 2, block(7, 0), padding),
                 (tv[0][SUBLANES:, :] + tv[1][0:1, :], (iota8 + SUBLANES) * PEER_TOPK,
                  ti[0][SUBLANES:, :] * PEER_NKEYS + ti[1][0:1, :])]
        cs = [tv[0][0:1, :] + tv[1]] + [tile[0] for tile in tiles]
        cp = [iota16] + [tile[1] for tile in tiles]
        ce = [ti[0][0:1, :] * PEER_NKEYS + ti[1]] + [tile[2] for tile in tiles]
        best_s, _, best_e = _extract_topk(jnp.concatenate(cs, axis=0), jnp.concatenate(cp, axis=0), PEER_TOPK,
                                          payload=jnp.concatenate(ce, axis=0))
        ex = jnp.exp(best_s - best_s[0:1, :])
        gate_rows.append(ex / jnp.sum(ex, axis=0, keepdims=True))
        idx_rows.append(best_e)
    idx_ref[...] = (jnp.concatenate(idx_rows, axis=0) * ROW_WORDS_SUBLANES).astype(jnp.int32).T
    gate_ref[...] = jnp.concatenate(gate_rows, axis=0).T


def _peer_route(h, wqt, keys):
    t = h.shape[0]
    rt = ROUTE_TOKENS
    pairs = pl.BlockSpec((rt, PEER_PAIRS), lambda i: (i, 0))
    return pl.pallas_call(
        _peer_route_kernel,
        out_shape=(jax.ShapeDtypeStruct((t, PEER_PAIRS), jnp.int32), jax.ShapeDtypeStruct((t, PEER_PAIRS), F32)),
        grid=(t // rt,),
        in_specs=[
            pl.BlockSpec((rt, D_MODEL), lambda i: (i, 0)),
            pl.BlockSpec((D_MODEL, D_MODEL), lambda i: (0, 0)),
            pl.BlockSpec((2, PEER_NKEYS, PEER_DKEY // 2), lambda i: (0, 0, 0)),
        ],
        out_specs=(pairs, pairs),
        compiler_params=_cparams(("parallel",)),
        name="peer_route",
    )(h, wqt, keys)


def _pack_table(u):
    ub = u.astype(jnp.bfloat16)
    half = D_MODEL // 2
    lo = lax.bitcast_convert_type(ub[:, :half], jnp.uint16).astype(jnp.uint32)
    hi = lax.bitcast_convert_type(ub[:, half:], jnp.uint16).astype(jnp.uint32)
    return (lo | (hi << 16)).reshape(u.shape[0] * ROW_WORDS_SUBLANES, LANES)


def _peer_sublayer(h, wq, keys, u, v, ln_g, ln_b):
    rows, gate = _peer_route(h, wq.T.astype(BF), keys)
    w = _peer_in(rows, h, gate, _pack_table(u))
    return _peer_out(rows, w, _pack_table(v), h, ln_g, ln_b)


def kernel(x, emb_ln_g, emb_ln_b, w_in, rg_conv_w, rg_conv_b, rg_wa, rg_ba, rg_wx, rg_bx, rg_lambda, ssm_conv_w, ssm_conv_b, ssm_dt_bias, ssm_a_log, ssm_d, ssm_norm_g, attn_sinks, w_out, ln1_g, ln1_b, peer_wq, peer_keys, peer_u, peer_v, ln2_g, ln2_b):
    bsz, seq, d = x.shape
    assert d == D_MODEL and seq % ROW_TILE == 0 and (bsz * seq) % ROUTE_TOKENS == 0
    h = _layer_norm(x.reshape(bsz * seq, d), emb_ln_g, emb_ln_b)
    for l in range(DEPTH):
        rg, z, xbc, dt, q, kv = _in_proj(h, _split_w_in(w_in[l]))
        y_a = _rg_lru(rg, seq, rg_conv_w[l], rg_conv_b[l], _block_diag(rg_wa[l]), rg_ba[l],
                      _block_diag(rg_wx[l]), rg_bx[l], rg_lambda[l])
        y_b = _ssd(z, xbc, dt, seq, ssm_conv_w[l], ssm_conv_b[l], ssm_dt_bias[l], ssm_a_log[l], ssm_d[l],
                   ssm_norm_g[l])
        y_c = _swa(q, kv, seq, attn_sinks[l])
        h = _out_proj(y_a, y_b, y_c, h, w_out[l], ln1_g[l], ln1_b[l])
        h = _peer_sublayer(h, peer_wq[l], peer_keys[l], peer_u[l], peer_v[l], ln2_g[l], ln2_b[l])
    return h.reshape(bsz, seq, d)
```

```python
import functools
import math

import jax
import jax.numpy as jnp
from jax import lax
from jax.experimental import pallas as pl
from jax.experimental.pallas import tpu as pltpu

D_MODEL = 1024
DEPTH = 2

W_LRU = D_MODEL // 2
LRU_BLOCKS = 8
LRU_C = 8.0
CONV_K = 4

SSM_HEAD_DIM = 64
SSM_D_INNER = D_MODEL
SSM_HEADS = SSM_D_INNER // SSM_HEAD_DIM
SSM_GROUPS = 2
SSM_STATE = 128
SSM_CHUNK = 128
SSM_CONV_CH = SSM_D_INNER + 2 * SSM_GROUPS * SSM_STATE
SSD_GROUP_W = SSM_D_INNER // SSM_GROUPS

ATTN_HEAD_DIM = 64
ATTN_Q_HEADS = (D_MODEL // 2) // ATTN_HEAD_DIM
ATTN_KV_HEADS = 2
ATTN_REP = ATTN_Q_HEADS // ATTN_KV_HEADS
ATTN_BLOCK = 128
ATTN_STEP_BLOCKS = 2
ATTN_Q_W = ATTN_Q_HEADS * ATTN_HEAD_DIM
ATTN_KV_W = 2 * ATTN_KV_HEADS * ATTN_HEAD_DIM

MIX_WIDTH = W_LRU + SSM_D_INNER + ATTN_Q_W

PEER_HEADS = 8
PEER_NKEYS = 128
PEER_DKEY = 128
PEER_TOPK = 16
PEER_PAIRS = PEER_HEADS * PEER_TOPK

DN_ALPHA = (2 * DEPTH) ** 0.25
LN_EPS = 1e-5

SUBLANES = 8
LANES = 128
HALO = SUBLANES
ROW_WORDS_SUBLANES = D_MODEL // 2 // LANES
TOKEN_SUBLANES = D_MODEL // LANES

ROW_TILE = 512
PAIR_LANES = 2 * ROW_WORDS_SUBLANES
GROUP_PAIRS = 32
GROUP_K = GROUP_PAIRS * PAIR_LANES
GATHER_TOKENS = 128
ROUTE_TOKENS = 256
VMEM_LIMIT = 48 * 1024 * 1024
PEER_VMEM_LIMIT = 56 * 1024 * 1024

BF = jnp.bfloat16
F32 = jnp.float32

PROJ_SEGS = (2 * W_LRU, SSM_D_INNER, SSM_CONV_CH, LANES, ATTN_Q_W, ATTN_KV_W)


def _cparams(sem):
    return pltpu.CompilerParams(dimension_semantics=sem, vmem_limit_bytes=VMEM_LIMIT)


def _softplus(x):
    return jnp.maximum(x, 0.0) + jnp.log1p(jnp.exp(-jnp.abs(x)))


def _sigmoid(x):
    return 1.0 / (1.0 + jnp.exp(-x))


def _silu(x):
    return x * _sigmoid(x)


def _split3(x):
    hi = x.astype(BF)
    rest = x - hi.astype(F32)
    mid = rest.astype(BF)
    return hi, mid, (rest - mid.astype(F32)).astype(BF)


def _gelu_tanh(x):
    return 0.5 * x * (1.0 + jnp.tanh(math.sqrt(2.0 / math.pi) * (x + 0.044715 * (x * x * x))))


def _ln_math(x, g, b):
    mu = jnp.mean(x, axis=-1, keepdims=True)
    xc = x - mu
    var = jnp.mean(xc * xc, axis=-1, keepdims=True)
    return xc * lax.rsqrt(var + LN_EPS) * g + b


def _inproj_kernel(*refs, entry_norm):
    n = len(PROJ_SEGS)
    if entry_norm:
        x_ref, g_ref, b_ref = refs[:3]
        w_refs, h_out_ref, o_refs = refs[3:3 + n], refs[3 + n], refs[4 + n:]
        h = _ln_math(x_ref[...], g_ref[...], b_ref[...])
        h_out_ref[...] = h
    else:
        w_refs, o_refs = refs[1:1 + n], refs[1 + n:]
        h = refs[0][...]
    hb = h.astype(BF)
    for w_ref, o_ref in zip(w_refs, o_refs):
        o_ref[...] = jnp.dot(hb, w_ref[...], preferred_element_type=F32)


def _in_proj(h, ws, entry_ln=None):
    t = h.shape[0]
    row = lambda w: pl.BlockSpec((ROW_TILE, w), lambda i: (i, 0))
    vec = pl.BlockSpec((1, D_MODEL), lambda i: (0, 0))
    norm = entry_ln is not None
    outs = pl.pallas_call(
        functools.partial(_inproj_kernel, entry_norm=norm),
        out_shape=tuple(jax.ShapeDtypeStruct((t, w), F32) for w in ((D_MODEL,) if norm else ()) + PROJ_SEGS),
        grid=(t // ROW_TILE,),
        in_specs=[row(D_MODEL)] + ([vec, vec] if norm else [])
        + [pl.BlockSpec((D_MODEL, w), lambda i: (0, 0)) for w in PROJ_SEGS],
        out_specs=tuple(row(w) for w in ((D_MODEL,) if norm else ()) + PROJ_SEGS),
        compiler_params=_cparams(("parallel",)),
        name="in_proj",
    )(h, *([v.reshape(1, D_MODEL) for v in entry_ln] if norm else []), *ws)
    return (outs[0], outs[1:]) if norm else (h, outs)


def _split_w_in(w_in):
    w = w_in.astype(BF)
    c = [0]
    for width in (2 * W_LRU, SSM_D_INNER, SSM_CONV_CH, SSM_HEADS, ATTN_Q_W, ATTN_KV_W):
        c.append(c[-1] + width)
    segs = [w[:, c[i]:c[i + 1]] for i in range(6)]
    segs[3] = jnp.pad(segs[3], ((0, 0), (0, LANES - SSM_HEADS)))
    return tuple(segs)


def _shift_rows(x, s, fill):
    n = x.shape[0]
    if s % SUBLANES == 0:
        return jnp.concatenate([jnp.full((s,) + x.shape[1:], fill, x.dtype), x[:n - s]], axis=0)
    rolled = pltpu.roll(x, s, axis=0)
    row = lax.broadcasted_iota(jnp.int32, x.shape, 0)
    return jnp.where(row < s, fill, rolled)


def _causal_conv(x, halo_ref, w_ref, b_ref):
    n = x.shape[0]
    halo_ref[pl.ds(HALO, n), :] = x
    out = b_ref[...] + w_ref[CONV_K - 1:CONV_K, :] * x
    for s in range(1, CONV_K):
        out = out + w_ref[CONV_K - 1 - s:CONV_K - s, :] * halo_ref[pl.ds(HALO - s, n), :]
    halo_ref[pl.ds(0, HALO), :] = x[n - HALO:n]
    return out


def _rglru_kernel(rg_ref, cw_ref, cb_ref, wa_ref, ba_ref, wx_ref, bx_ref, lam_ref, o_ref, halo_ref, carry_ref):
    n = rg_ref.shape[0]

    @pl.when(pl.program_id(1) == 0)
    def _():
        halo_ref[pl.ds(0, HALO), :] = jnp.zeros((HALO, halo_ref.shape[1]), F32)
        carry_ref[...] = jnp.zeros_like(carry_ref)

    x = rg_ref[:, 0:W_LRU]
    xc = _causal_conv(x, halo_ref, cw_ref, cb_ref)
    xb = xc.astype(BF)
    r = _sigmoid(jnp.dot(xb, wa_ref[...], preferred_element_type=F32) + ba_ref[...])
    i = _sigmoid(jnp.dot(xb, wx_ref[...], preferred_element_type=F32) + bx_ref[...])
    log_a = (-LRU_C * r) * _softplus(-lam_ref[...])
    a = jnp.exp(log_a)
    u = jnp.sqrt(-jnp.tanh(log_a) * (a * a + 1.0)) * (i * xc)
    k = 1
    while k < n:
        u = a * _shift_rows(u, k, 0.0) + u
        a = a * _shift_rows(a, k, 1.0)
        k *= 2
    h = u + a * carry_ref[0:1, :]
    carry_ref[...] = jnp.broadcast_to(h[n - 1:n, :], carry_ref.shape)
    o_ref[...] = _gelu_tanh(rg_ref[:, W_LRU:2 * W_LRU]) * h


def _rg_lru(rg, seq, conv_w, conv_b, wa, ba, wx, bx, lam):
    t = rg.shape[0]
    nb = seq // ROW_TILE
    row = lambda b, j: (b * nb + j, 0)
    const = lambda b, j: (0, 0)
    vec = pl.BlockSpec((1, W_LRU), const)
    return pl.pallas_call(
        _rglru_kernel,
        out_shape=jax.ShapeDtypeStruct((t, W_LRU), F32),
        grid=(t // seq, nb),
        in_specs=[pl.BlockSpec((ROW_TILE, 2 * W_LRU), row),
                  pl.BlockSpec((CONV_K, W_LRU), const), vec,
                  pl.BlockSpec((W_LRU, W_LRU), const), vec,
                  pl.BlockSpec((W_LRU, W_LRU), const), vec, vec],
        out_specs=pl.BlockSpec((ROW_TILE, W_LRU), row),
        scratch_shapes=[pltpu.VMEM((HALO + ROW_TILE, W_LRU), F32), pltpu.VMEM((SUBLANES, W_LRU), F32)],
        compiler_params=_cparams(("parallel", "arbitrary")),
        name="rg_lru",
    )(rg, conv_w, conv_b.reshape(1, -1), wa, ba.reshape(1, -1), wx, bx.reshape(1, -1), lam.reshape(1, -1))


def _block_diag(w):
    nb, c, _ = w.shape
    eye = jnp.eye(nb, dtype=w.dtype)
    return (eye[:, None, :, None] * w[:, :, None, :]).reshape(nb * c, nb * c).astype(BF)


def _ssd_kernel(z_ref, xbc_ref, dt_ref, cw_ref, cb_ref, dtb_ref, alog_ref, dskip_ref, ng_ref, expand_ref,
                o_ref, halo_ref, state_ref):
    L = SSM_CHUNK

    @pl.when(pl.program_id(1) == 0)
    def _():
        halo_ref[pl.ds(0, HALO), :] = jnp.zeros((HALO, halo_ref.shape[1]), F32)
        state_ref[...] = jnp.zeros_like(state_ref)

    xbc = xbc_ref[...]
    conv = _silu(_causal_conv(xbc, halo_ref, cw_ref, cb_ref))
    xs = conv[:, 0:SSM_D_INNER]
    bm = conv[:, SSM_D_INNER:SSM_D_INNER + SSM_GROUPS * SSM_STATE]
    cm = conv[:, SSM_D_INNER + SSM_GROUPS * SSM_STATE:]

    dt = _softplus(dt_ref[...] + dtb_ref[...])
    da = dt * (-jnp.exp(alog_ref[...]))
    row = lax.broadcasted_iota(jnp.int32, (L, L), 0)
    col = lax.broadcasted_iota(jnp.int32, (L, L), 1)
    causal = col <= row
    a_cs = sum(jnp.dot(causal.astype(BF), part, preferred_element_type=F32) for part in _split3(da))
    a_cs_t = a_cs.T
    expand = expand_ref[...]
    dt_full = sum(jnp.dot(part, expand, preferred_element_type=F32) for part in _split3(dt))
    acs_full = sum(jnp.dot(part, expand, preferred_element_type=F32) for part in _split3(a_cs))
    alast_full = acs_full[L - 1:L, :]
    xdt = xs * dt_full
    xdt_b = xdt.astype(BF)
    xst_b = (xdt * jnp.exp(alast_full - acs_full)).astype(BF)
    left = lax.broadcasted_iota(jnp.int32, (L, LANES), 1) < SSM_HEAD_DIM

    y_parts = []
    for g in range(SSM_GROUPS):
        cg = cm[:, g * SSM_STATE:(g + 1) * SSM_STATE].astype(BF)
        bg = bm[:, g * SSM_STATE:(g + 1) * SSM_STATE].astype(BF)
        cb = lax.dot_general(cg, bg, (((1,), (1,)), ((), ())), preferred_element_type=F32)
        st = state_ref[:, g * SSD_GROUP_W:(g + 1) * SSD_GROUP_W]
        y_off = jnp.dot(cg, st.astype(BF), preferred_element_type=F32)
        for j in range(SSD_GROUP_W // LANES):
            h0 = g * (SSM_HEADS // SSM_GROUPS) + 2 * j
            ms = []
            for h in (h0, h0 + 1):
                seg = a_cs[:, h:h + 1] - a_cs_t[h:h + 1, :]
                ms.append((cb * jnp.exp(jnp.where(causal, seg, -jnp.inf))).astype(BF))
            c0 = h0 * SSM_HEAD_DIM
            x2 = xdt_b[:, c0:c0 + LANES]
            zero = jnp.zeros_like(x2)
            xblk = jnp.concatenate([jnp.where(left, x2, zero), jnp.where(left, zero, x2)], axis=0)
            y_parts.append(jnp.dot(jnp.concatenate(ms, axis=1), xblk, preferred_element_type=F32)
                           + y_off[:, j * LANES:(j + 1) * LANES] * jnp.exp(acs_full[:, c0:c0 + LANES]))
        new = lax.dot_general(bg, xst_b[:, g * SSD_GROUP_W:(g + 1) * SSD_GROUP_W], (((0,), (0,)), ((), ())),
                              preferred_element_type=F32)
        state_ref[:, g * SSD_GROUP_W:(g + 1) * SSD_GROUP_W] = (
            st * jnp.exp(alast_full[:, g * SSD_GROUP_W:(g + 1) * SSD_GROUP_W]) + new)
    y = jnp.concatenate(y_parts, axis=1) + dskip_ref[...] * xs
    y = y * _silu(z_ref[...])
    outs = []
    for g in range(SSM_GROUPS):
        yg = y[:, g * SSD_GROUP_W:(g + 1) * SSD_GROUP_W]
        outs.append(yg * lax.rsqrt(jnp.mean(yg * yg, axis=-1, keepdims=True) + LN_EPS))
    o_ref[...] = jnp.concatenate(outs, axis=1) * ng_ref[...]


def _ssd(z, xbc, dt, seq, conv_w, conv_b, dt_bias, a_log, d_skip, norm_g):
    t = z.shape[0]
    nc = seq // SSM_CHUNK
    row = lambda b, c: (b * nc + c, 0)
    const = lambda b, c: (0, 0)
    pad = LANES - SSM_HEADS
    expand = (jnp.arange(LANES)[:, None] == (jnp.arange(SSM_D_INNER)[None, :] // SSM_HEAD_DIM)).astype(BF)
    return pl.pallas_call(
        _ssd_kernel,
        out_shape=jax.ShapeDtypeStruct((t, SSM_D_INNER), F32),
        grid=(t // seq, nc),
        in_specs=[pl.BlockSpec((SSM_CHUNK, SSM_D_INNER), row),
                  pl.BlockSpec((SSM_CHUNK, SSM_CONV_CH), row),
                  pl.BlockSpec((SSM_CHUNK, LANES), row),
                  pl.BlockSpec((CONV_K, SSM_CONV_CH), const),
                  pl.BlockSpec((1, SSM_CONV_CH), const),
                  pl.BlockSpec((1, LANES), const),
                  pl.BlockSpec((1, LANES), const),
                  pl.BlockSpec((1, SSM_D_INNER), const),
                  pl.BlockSpec((1, SSM_D_INNER), const),
                  pl.BlockSpec((LANES, SSM_D_INNER), const)],
        out_specs=pl.BlockSpec((SSM_CHUNK, SSM_D_INNER), row),
        scratch_shapes=[pltpu.VMEM((HALO + SSM_CHUNK, SSM_CONV_CH), F32), pltpu.VMEM((SSM_STATE, SSM_D_INNER), F32)],
        compiler_params=_cparams(("parallel", "arbitrary")),
        name="ssd",
    )(z, xbc, dt, conv_w, conv_b.reshape(1, -1), jnp.pad(dt_bias, (0, pad)).reshape(1, -1),
      jnp.pad(a_log, (0, pad)).reshape(1, -1), jnp.repeat(d_skip, SSM_HEAD_DIM).reshape(1, -1),
      norm_g.reshape(1, -1), expand)


def _swa_kernel(sink_ref, q_ref, kv_ref, kvp_ref, o_ref):
    L = ATTN_BLOCK
    lane_k = lax.broadcasted_iota(jnp.int32, (2 * L, LANES), 1)
    qi = lax.broadcasted_iota(jnp.int32, (2 * L, 2 * L), 0) % L
    kj = lax.broadcasted_iota(jnp.int32, (2 * L, 2 * L), 1)
    rel = qi + L - kj
    band = (rel >= 0) & (rel < L)
    for blk in range(ATTN_STEP_BLOCKS):
        rows = pl.ds(blk * L, L)
        prev = kvp_ref[...] if blk == 0 else kv_ref[pl.ds((blk - 1) * L, L), :]
        kv = jnp.concatenate([prev, kv_ref[rows, :]], axis=0)
        valid = band & (jnp.logical_not(pl.program_id(1) == 0) | (kj >= L)) if blk == 0 else band
        _swa_block(sink_ref, q_ref, o_ref, rows, kv, valid, lane_k)


def _swa_block(sink_ref, q_ref, o_ref, rows, kv, valid, lane_k):
    L = ATTN_BLOCK
    hd = ATTN_HEAD_DIM
    kk = kv[:, 0:LANES]
    vv = kv[:, LANES:2 * LANES].astype(BF)
    top = lax.broadcasted_iota(jnp.int32, (2 * L, 1), 0) < L
    lane_o = lax.broadcasted_iota(jnp.int32, (L, LANES), 1)
    scale = hd ** -0.5
    for j in range(ATTN_Q_HEADS // 2):
        g = (2 * j) // ATTN_REP
        q2 = q_ref[rows, j * LANES:(j + 1) * LANES]
        q2r = pltpu.roll(q2, hd, axis=1)
        kg = jnp.where((lane_k >= g * hd) & (lane_k < (g + 1) * hd), kk, 0.0).astype(BF)
        qa, qb = (q2, q2r) if g == 0 else (q2r, q2)
        qs = jnp.concatenate([qa, qb], axis=0).astype(BF)
        logits = lax.dot_general(qs, kg, (((1,), (1,)), ((), ())), preferred_element_type=F32) * scale
        logits = jnp.where(valid, logits, -jnp.inf)
        sink = jnp.where(top, sink_ref[2 * j], sink_ref[2 * j + 1])
        m = jnp.maximum(jnp.max(logits, axis=-1, keepdims=True), sink)
        p = jnp.exp(logits - m)
        probs = p * (1.0 / (jnp.sum(p, axis=-1, keepdims=True) + jnp.exp(sink - m)))
        o = jnp.dot(probs.astype(BF), vv, preferred_element_type=F32)
        oa, ob = o[0:L], o[L:2 * L]
        if g == 0:
            out2 = jnp.where(lane_o < hd, oa, pltpu.roll(ob, hd, axis=1))
        else:
            out2 = jnp.where(lane_o < hd, pltpu.roll(oa, hd, axis=1), ob)
        o_ref[rows, j * LANES:(j + 1) * LANES] = out2


def _swa(q, kv, seq, sinks):
    t = q.shape[0]
    sb = ATTN_STEP_BLOCKS
    rows = sb * ATTN_BLOCK
    ns = seq // rows
    tile = lambda b, n: (b * ns + n, 0)
    before = lambda b, n: (b * ns * sb + jnp.maximum(n * sb - 1, 0), 0)
    return pl.pallas_call(
        _swa_kernel,
        out_shape=jax.ShapeDtypeStruct((t, ATTN_Q_W), F32),
        grid=(t // seq, ns),
        in_specs=[pl.BlockSpec(memory_space=pltpu.SMEM),
                  pl.BlockSpec((rows, ATTN_Q_W), tile),
                  pl.BlockSpec((rows, ATTN_KV_W), tile),
                  pl.BlockSpec((ATTN_BLOCK, ATTN_KV_W), before)],
        out_specs=pl.BlockSpec((rows, ATTN_Q_W), tile),
        compiler_params=_cparams(("parallel", "parallel")),
        name="swa",
    )(sinks, q, kv, kv)


def _outproj_kernel(ya_ref, yb_ref, yc_ref, h_ref, wa_ref, wb_ref, wc_ref, g_ref, b_ref, o_ref):
    mix = (jnp.dot(ya_ref[...].astype(BF), wa_ref[...], preferred_element_type=F32)
           + jnp.dot(yb_ref[...].astype(BF), wb_ref[...], preferred_element_type=F32)
           + jnp.dot(yc_ref[...].astype(BF), wc_ref[...], preferred_element_type=F32))
    o_ref[...] = _ln_math(DN_ALPHA * h_ref[...] + mix, g_ref[...], b_ref[...])


def _out_proj(ya, yb, yc, h, w_out, g, b):
    t = h.shape[0]
    w = w_out.astype(BF)
    row = lambda i: (i, 0)
    const = lambda i: (0, 0)
    c1, c2 = W_LRU, W_LRU + SSM_D_INNER
    return pl.pallas_call(
        _outproj_kernel,
        out_shape=jax.ShapeDtypeStruct((t, D_MODEL), F32),
        grid=(t // ROW_TILE,),
        in_specs=[pl.BlockSpec((ROW_TILE, W_LRU), row), pl.BlockSpec((ROW_TILE, SSM_D_INNER), row),
                  pl.BlockSpec((ROW_TILE, ATTN_Q_W), row), pl.BlockSpec((ROW_TILE, D_MODEL), row),
                  pl.BlockSpec((W_LRU, D_MODEL), const), pl.BlockSpec((SSM_D_INNER, D_MODEL), const),
                  pl.BlockSpec((ATTN_Q_W, D_MODEL), const),
                  pl.BlockSpec((1, D_MODEL), const), pl.BlockSpec((1, D_MODEL), const)],
        out_specs=pl.BlockSpec((ROW_TILE, D_MODEL), row),
        compiler_params=_cparams(("parallel",)),
        name="out_proj",
    )(ya, yb, yc, h, w[0:c1], w[c1:c2], w[c2:MIX_WIDTH], g.reshape(1, -1), b.reshape(1, -1))


def _gather_group(idx_ref, tbl_ref, t, g):
    rs = ROW_WORDS_SUBLANES
    pieces = []
    for q in range(GROUP_PAIRS // 2):
        p = g * GROUP_PAIRS + 2 * q
        if p % SUBLANES == 0:
            window = idx_ref.at[t, pl.ds(p, SUBLANES)]
        ra = pl.multiple_of(window[p % SUBLANES], rs)
        rb = pl.multiple_of(window[p % SUBLANES + 1], rs)
        pieces.append(jnp.concatenate([tbl_ref[pl.ds(ra, rs), :], tbl_ref[pl.ds(rb, rs), :]], axis=0))
    return pltpu.bitcast(jnp.concatenate(pieces, axis=0), BF)


def _chunk_of_column(col):
    j = col % PAIR_LANES
    return j // 2 + ROW_WORDS_SUBLANES * (j % 2)


def _split_hi_lo(x):
    hi = x.astype(BF)
    lo = (x - hi.astype(F32)).astype(BF)
    return hi, lo


def _peer_in_kernel(idx_ref, x_ref, gate_ref, tbl_ref, o_ref, m_ref):
    n = x_ref.shape[0]
    row = lax.broadcasted_iota(jnp.int32, (2 * SUBLANES, GROUP_K), 0) % SUBLANES
    keep = row == _chunk_of_column(lax.broadcasted_iota(jnp.int32, (2 * SUBLANES, GROUP_K), 1))
    for t in range(n):
        x_row = x_ref[pl.ds(t, 1), :]
        x8 = jnp.concatenate([x_row[:, c * LANES:(c + 1) * LANES] for c in range(TOKEN_SUBLANES)], axis=0)
        x_hi, x_lo = _split_hi_lo(x8)
        x16 = jnp.concatenate([x_hi, x_lo], axis=0)
        for g in range(PEER_PAIRS // GROUP_PAIRS):
            r = lax.dot_general(x16, _gather_group(idx_ref, tbl_ref, t, g), (((1,), (1,)), ((), ())),
                                preferred_element_type=F32)
            m_ref[pl.ds(t, 1), g * GROUP_K:(g + 1) * GROUP_K] = jnp.sum(jnp.where(keep, r, 0.0), axis=0,
                                                                        keepdims=True)
    wide = PEER_PAIRS * PAIR_LANES
    fold = (lax.broadcasted_iota(jnp.int32, (wide, PEER_PAIRS), 0) // PAIR_LANES
            == lax.broadcasted_iota(jnp.int32, (wide, PEER_PAIRS), 1)).astype(BF)
    m_hi, m_lo = _split_hi_lo(m_ref[...])
    a = jnp.dot(m_hi, fold, preferred_element_type=F32) + jnp.dot(m_lo, fold, preferred_element_type=F32)
    o_ref[...] = gate_ref[...] * (0.5 * a * (1.0 + lax.erf(a * (2.0 ** -0.5))))


def _peer_in(rows, x, gate, tbl):
    t = x.shape[0]
    tt = GATHER_TOKENS
    pairs = pl.BlockSpec((tt, PEER_PAIRS), lambda i: (i, 0))
    return pl.pallas_call(
        _peer_in_kernel,
        out_shape=jax.ShapeDtypeStruct((t, PEER_PAIRS), F32),
        grid=(t // tt,),
        in_specs=[
            pl.BlockSpec((tt, PEER_PAIRS), lambda i: (i, 0), memory_space=pltpu.SMEM),
            pl.BlockSpec((tt, D_MODEL), lambda i: (i, 0)),
            pairs,
            pl.BlockSpec(memory_space=pltpu.VMEM),
        ],
        out_specs=pairs,
        scratch_shapes=[pltpu.VMEM((tt, PEER_PAIRS * PAIR_LANES), F32)],
        compiler_params=pltpu.CompilerParams(dimension_semantics=("parallel",), vmem_limit_bytes=PEER_VMEM_LIMIT),
        name="peer_in",
    )(rows, x, gate, tbl)


def _peer_out_kernel(idx_ref, w_ref, tbl_ref, h_ref, g_ref, b_ref, o_ref, wide_ref, ffn_ref):
    n = w_ref.shape[0]
    wide = PEER_PAIRS * PAIR_LANES
    rep = (lax.broadcasted_iota(jnp.int32, (PEER_PAIRS, wide), 1) // PAIR_LANES
           == lax.broadcasted_iota(jnp.int32, (PEER_PAIRS, wide), 0)).astype(BF)
    for k, wv in enumerate(_split_hi_lo(w_ref[...])):
        wide_ref[k] = jnp.dot(wv, rep, preferred_element_type=F32)
    keep = (lax.broadcasted_iota(jnp.int32, (SUBLANES, GROUP_K), 0)
            == _chunk_of_column(lax.broadcasted_iota(jnp.int32, (SUBLANES, GROUP_K), 1)))
    for t in range(n):
        acc = jnp.zeros((2 * SUBLANES, LANES), F32)
        for g in range(PEER_PAIRS // GROUP_PAIRS):
            cols = slice(g * GROUP_K, (g + 1) * GROUP_K)
            lhs = jnp.concatenate(
                [jnp.where(keep, jnp.broadcast_to(wide_ref[k, pl.ds(t, 1), cols], (SUBLANES, GROUP_K)), 0.0)
                 for k in range(2)], axis=0).astype(BF)
            acc = acc + jnp.dot(lhs, _gather_group(idx_ref, tbl_ref, t, g), preferred_element_type=F32)
        out8 = acc[0:SUBLANES] + acc[SUBLANES:]
        for c in range(TOKEN_SUBLANES):
            ffn_ref[pl.ds(t, 1), c * LANES:(c + 1) * LANES] = out8[c:c + 1, :]
    o_ref[...] = _ln_math(DN_ALPHA * h_ref[...] + ffn_ref[...], g_ref[...], b_ref[...])


def _peer_out(rows, w, tbl, h, g, b):
    t = h.shape[0]
    tt = GATHER_TOKENS
    row = pl.BlockSpec((tt, D_MODEL), lambda i: (i, 0))
    vec = pl.BlockSpec((1, D_MODEL), lambda i: (0, 0))
    return pl.pallas_call(
        _peer_out_kernel,
        out_shape=jax.ShapeDtypeStruct((t, D_MODEL), F32),
        grid=(t // tt,),
        in_specs=[
            pl.BlockSpec((tt, PEER_PAIRS), lambda i: (i, 0), memory_space=pltpu.SMEM),
            pl.BlockSpec((tt, PEER_PAIRS), lambda i: (i, 0)),
            pl.BlockSpec(memory_space=pltpu.VMEM),
            row, vec, vec,
        ],
        out_specs=row,
        scratch_shapes=[pltpu.VMEM((2, tt, PEER_PAIRS * PAIR_LANES), F32), pltpu.VMEM((tt, D_MODEL), F32)],
        compiler_params=pltpu.CompilerParams(vmem_limit_bytes=PEER_VMEM_LIMIT),
        name="peer_out",
    )(rows, w, tbl, h, g.reshape(1, D_MODEL), b.reshape(1, D_MODEL))


def _extract_topk(s, pos, k, payload=None):
    big = jnp.asarray(2 ** 30, pos.dtype)
    vals, poss, pays = [], [], []
    for _ in range(k):
        m = jnp.max(s, axis=0, keepdims=True)
        j = jnp.min(jnp.where(s == m, pos, big), axis=0, keepdims=True)
        sel = pos == j
        vals.append(m)
        poss.append(j)
        if payload is not None:
            pays.append(jnp.sum(jnp.where(sel, payload, jnp.zeros_like(payload)), axis=0, keepdims=True))
        s = jnp.where(sel, -jnp.inf, s)
    out = [jnp.concatenate(vals, axis=0), jnp.concatenate(poss, axis=0)]
    if payload is not None:
        out.append(jnp.concatenate(pays, axis=0))
    return out


def _topk_two_per_slot(s, k):
    m, n = s.shape[0] // 2, s.shape[1]
    big = jnp.asarray(2 ** 30, F32)
    pos_a = lax.broadcasted_iota(jnp.int32, (m, n), 0).astype(F32)
    a, b = s[:m], s[m:]
    b_wins = b > a
    win, wait = jnp.where(b_wins, b, a), jnp.where(b_wins, a, b)
    win_pos, wait_pos = jnp.where(b_wins, pos_a + m, pos_a), jnp.where(b_wins, pos_a, pos_a + m)
    vals, poss = [], []
    for _ in range(k):
        top = jnp.max(win, axis=0, keepdims=True)
        j = jnp.min(jnp.where(win == top, win_pos, big), axis=0, keepdims=True)
        sel = win_pos == j
        vals.append(top)
        poss.append(j)
        win = jnp.where(sel, wait, win)
        win_pos = jnp.where(sel, wait_pos, win_pos)
        wait = jnp.where(sel, -jnp.inf, wait)
    return jnp.concatenate(vals, axis=0), jnp.concatenate(poss, axis=0)


def _peer_route_kernel(h_ref, wqt_ref, keys_ref, idx_ref, gate_ref):
    half_dim = PEER_DKEY // 2
    hb = h_ref[...].astype(BF)
    n = hb.shape[0]
    iota16 = lax.broadcasted_iota(jnp.int32, (PEER_TOPK, n), 0).astype(F32)
    iota8 = lax.broadcasted_iota(jnp.int32, (SUBLANES, n), 0).astype(F32)
    qt = lax.dot_general(wqt_ref[...], hb, (((1,), (1,)), ((), ())), preferred_element_type=F32)
    idx_rows, gate_rows = [], []
    for hd in range(PEER_HEADS):
        tv, ti = [], []
        for half in range(2):
            r0 = hd * PEER_DKEY + half * half_dim
            s = jnp.dot(keys_ref[half], qt[r0:r0 + half_dim, :], preferred_element_type=F32)
            v, i = _topk_two_per_slot(s, PEER_TOPK)
            tv.append(v)
            ti.append(i)
        v1_lo, i1_lo = tv[1][0:SUBLANES, :], ti[1][0:SUBLANES, :]

        def block(a, first_row):
            shift = (lambda x: x) if first_row == 0 else (lambda x: pltpu.roll(x, first_row, axis=0))
            return (tv[0][a:a + 1, :] + shift(v1_lo), iota8 + (a * PEER_TOPK - first_row),
                    ti[0][a:a + 1, :] * PEER_NKEYS + shift(i1_lo))

        def pick(cond, x, y):
            return tuple(jnp.where(cond, xi, yi) for xi, yi in zip(x, y))

        padding = (jnp.full_like(v1_lo, -jnp.inf), iota8 + 2 ** 20, jnp.zeros_like(i1_lo))
        tiles = [block(1, 0),
                 pick(iota8 < 5, block(2, 0), block(4, 5)),
                 pick(iota8 < 4, block(3, 0), pick(iota8 < 6, block(5, 4), block(6, 6))),
                 pick(iota8 < 2, block(7, 0), padding),
                 (tv[0][SUBLANES:, :] + tv[1][0:1, :], (iota8 + SUBLANES) * PEER_TOPK,
                  ti[0][SUBLANES:, :] * PEER_NKEYS + ti[1][0:1, :])]
        cs = [tv[0][0:1, :] + tv[1]] + [tile[0] for tile in tiles]
        cp = [iota16] + [tile[1] for tile in tiles]
        ce = [ti[0][0:1, :] * PEER_NKEYS + ti[1]] + [tile[2] for tile in tiles]
        best_s, _, best_e = _extract_topk(jnp.concatenate(cs, axis=0), jnp.concatenate(cp, axis=0), PEER_TOPK,
                                          payload=jnp.concatenate(ce, axis=0))
        ex = jnp.exp(best_s - best_s[0:1, :])
        gate_rows.append(ex / jnp.sum(ex, axis=0, keepdims=True))
        idx_rows.append(best_e)
    idx_ref[...] = (jnp.concatenate(idx_rows, axis=0) * ROW_WORDS_SUBLANES).astype(jnp.int32).T
    gate_ref[...] = jnp.concatenate(gate_rows, axis=0).T


def _peer_route(h, wqt, keys):
    t = h.shape[0]
    rt = ROUTE_TOKENS
    pairs = pl.BlockSpec((rt, PEER_PAIRS), lambda i: (i, 0))
    return pl.pallas_call(
        _peer_route_kernel,
        out_shape=(jax.ShapeDtypeStruct((t, PEER_PAIRS), jnp.int32), jax.ShapeDtypeStruct((t, PEER_PAIRS), F32)),
        grid=(t // rt,),
        in_specs=[
            pl.BlockSpec((rt, D_MODEL), lambda i: (i, 0)),
            pl.BlockSpec((D_MODEL, D_MODEL), lambda i: (0, 0)),
            pl.BlockSpec((2, PEER_NKEYS, PEER_DKEY // 2), lambda i: (0, 0, 0)),
        ],
        out_specs=(pairs, pairs),
        compiler_params=_cparams(("parallel",)),
        name="peer_route",
    )(h, wqt, keys)


def _pack_table(u):
    ub = u.astype(jnp.bfloat16)
    half = D_MODEL // 2
    lo = lax.bitcast_convert_type(ub[:, :half], jnp.uint16).astype(jnp.uint32)
    hi = lax.bitcast_convert_type(ub[:, half:], jnp.uint16).astype(jnp.uint32)
    return (lo | (hi << 16)).reshape(u.shape[0] * ROW_WORDS_SUBLANES, LANES)


def _peer_sublayer(h, wq, keys, u, v, ln_g, ln_b):
    rows, gate = _peer_route(h, wq.T.astype(BF), keys)
    w = _peer_in(rows, h, gate, _pack_table(u))
    return _peer_out(rows, w, _pack_table(v), h, ln_g, ln_b)


def kernel(x, emb_ln_g, emb_ln_b, w_in, rg_conv_w, rg_conv_b, rg_wa, rg_ba, rg_wx, rg_bx, rg_lambda, ssm_conv_w, ssm_conv_b, ssm_dt_bias, ssm_a_log, ssm_d, ssm_norm_g, attn_sinks, w_out, ln1_g, ln1_b, peer_wq, peer_keys, peer_u, peer_v, ln2_g, ln2_b):
    bsz, seq, d = x.shape
    assert d == D_MODEL and seq % ROW_TILE == 0 and (bsz * seq) % ROUTE_TOKENS == 0
    assert rg_wa.shape[1:] == (LRU_BLOCKS, W_LRU // LRU_BLOCKS, W_LRU // LRU_BLOCKS)
    h = x.reshape(bsz * seq, d)
    for l in range(DEPTH):
        h, (rg, z, xbc, dt, q, kv) = _in_proj(h, _split_w_in(w_in[l]), (emb_ln_g, emb_ln_b) if l == 0 else None)
        y_a = _rg_lru(rg, seq, rg_conv_w[l], rg_conv_b[l], _block_diag(rg_wa[l]), rg_ba[l],
                      _block_diag(rg_wx[l]), rg_bx[l], rg_lambda[l])
        y_b = _ssd(z, xbc, dt, seq, ssm_conv_w[l], ssm_conv_b[l], ssm_dt_bias[l], ssm_a_log[l], ssm_d[l],
                   ssm_norm_g[l])
        y_c = _swa(q, kv, seq, attn_sinks[l])
        h = _out_proj(y_a, y_b, y_c, h, w_out[l], ln1_g[l], ln1_b[l])
        h = _peer_sublayer(h, peer_wq[l], peer_keys[l], peer_u[l], peer_v[l], ln2_g[l], ln2_b[l])
    return h.reshape(bsz, seq, d)
```

```python
import functools
import math

import jax
import jax.numpy as jnp
from jax import lax
from jax.experimental import pallas as pl
from jax.experimental.pallas import tpu as pltpu

D_MODEL = 1024
DEPTH = 2

W_LRU = D_MODEL // 2
LRU_BLOCKS = 8
LRU_C = 8.0
CONV_K = 4

SSM_HEAD_DIM = 64
SSM_D_INNER = D_MODEL
SSM_HEADS = SSM_D_INNER // SSM_HEAD_DIM
SSM_GROUPS = 2
SSM_STATE = 128
SSM_CHUNK = 128
SSM_CONV_CH = SSM_D_INNER + 2 * SSM_GROUPS * SSM_STATE
SSD_GROUP_W = SSM_D_INNER // SSM_GROUPS

ATTN_HEAD_DIM = 64
ATTN_Q_HEADS = (D_MODEL // 2) // ATTN_HEAD_DIM
ATTN_KV_HEADS = 2
ATTN_REP = ATTN_Q_HEADS // ATTN_KV_HEADS
ATTN_BLOCK = 128
ATTN_STEP_BLOCKS = 2
ATTN_Q_W = ATTN_Q_HEADS * ATTN_HEAD_DIM
ATTN_KV_W = 2 * ATTN_KV_HEADS * ATTN_HEAD_DIM

MIX_WIDTH = W_LRU + SSM_D_INNER + ATTN_Q_W

PEER_HEADS = 8
PEER_NKEYS = 128
PEER_DKEY = 128
PEER_TOPK = 16
PEER_PAIRS = PEER_HEADS * PEER_TOPK

DN_ALPHA = (2 * DEPTH) ** 0.25
LN_EPS = 1e-5

SUBLANES = 8
LANES = 128
HALO = SUBLANES
ROW_WORDS_SUBLANES = D_MODEL // 2 // LANES
TOKEN_SUBLANES = D_MODEL // LANES

ROW_TILE = 512
PAIR_LANES = 2 * ROW_WORDS_SUBLANES
GROUP_PAIRS = 32
GROUP_K = GROUP_PAIRS * PAIR_LANES
GATHER_TOKENS = 128
ROUTE_TOKENS = 256
VMEM_LIMIT = 48 * 1024 * 1024
PEER_VMEM_LIMIT = 56 * 1024 * 1024

BF = jnp.bfloat16
F32 = jnp.float32

PROJ_SEGS = (2 * W_LRU, SSM_D_INNER, SSM_CONV_CH, LANES, ATTN_Q_W, ATTN_KV_W)


def _cparams(sem):
    return pltpu.CompilerParams(dimension_semantics=sem, vmem_limit_bytes=VMEM_LIMIT)


def _softplus(x):
    return jnp.maximum(x, 0.0) + jnp.log1p(jnp.exp(-jnp.abs(x)))


def _sigmoid(x):
    return 1.0 / (1.0 + jnp.exp(-x))


def _silu(x):
    return x * _sigmoid(x)


def _split3(x):
    hi = x.astype(BF)
    rest = x - hi.astype(F32)
    mid = rest.astype(BF)
    return hi, mid, (rest - mid.astype(F32)).astype(BF)


def _gelu_tanh(x):
    return 0.5 * x * (1.0 + jnp.tanh(math.sqrt(2.0 / math.pi) * (x + 0.044715 * (x * x * x))))


def _ln_math(x, g, b):
    mu = jnp.mean(x, axis=-1, keepdims=True)
    xc = x - mu
    var = jnp.mean(xc * xc, axis=-1, keepdims=True)
    return xc * lax.rsqrt(var + LN_EPS) * g + b


def _inproj_kernel(*refs, entry_norm):
    n = len(PROJ_SEGS)
    if entry_norm:
        x_ref, g_ref, b_ref = refs[:3]
        w_refs, h_out_ref, o_refs = refs[3:3 + n], refs[3 + n], refs[4 + n:]
        h = _ln_math(x_ref[...], g_ref[...], b_ref[...])
        h_out_ref[...] = h
    else:
        w_refs, o_refs = refs[1:1 + n], refs[1 + n:]
        h = refs[0][...]
    hb = h.astype(BF)
    for w_ref, o_ref in zip(w_refs, o_refs):
        o_ref[...] = jnp.dot(hb, w_ref[...], preferred_element_type=F32)


def _in_proj(h, ws, entry_ln=None):
    t = h.shape[0]
    row = lambda w: pl.BlockSpec((ROW_TILE, w), lambda i: (i, 0))
    vec = pl.BlockSpec((1, D_MODEL), lambda i: (0, 0))
    norm = entry_ln is not None
    outs = pl.pallas_call(
        functools.partial(_inproj_kernel, entry_norm=norm),
        out_shape=tuple(jax.ShapeDtypeStruct((t, w), F32) for w in ((D_MODEL,) if norm else ()) + PROJ_SEGS),
        grid=(t // ROW_TILE,),
        in_specs=[row(D_MODEL)] + ([vec, vec] if norm else [])
        + [pl.BlockSpec((D_MODEL, w), lambda i: (0, 0)) for w in PROJ_SEGS],
        out_specs=tuple(row(w) for w in ((D_MODEL,) if norm else ()) + PROJ_SEGS),
        compiler_params=_cparams(("parallel",)),
        name="in_proj",
    )(h, *([v.reshape(1, D_MODEL) for v in entry_ln] if norm else []), *ws)
    return (outs[0], outs[1:]) if norm else (h, outs)


def _split_w_in(w_in):
    w = w_in.astype(BF)
    c = [0]
    for width in (2 * W_LRU, SSM_D_INNER, SSM_CONV_CH, SSM_HEADS, ATTN_Q_W, ATTN_KV_W):
        c.append(c[-1] + width)
    segs = [w[:, c[i]:c[i + 1]] for i in range(6)]
    segs[3] = jnp.pad(segs[3], ((0, 0), (0, LANES - SSM_HEADS)))
    return tuple(segs)


def _shift_rows(x, s, fill):
    n = x.shape[0]
    if s % SUBLANES == 0:
        return jnp.concatenate([jnp.full((s,) + x.shape[1:], fill, x.dtype), x[:n - s]], axis=0)
    rolled = pltpu.roll(x, s, axis=0)
    row = lax.broadcasted_iota(jnp.int32, x.shape, 0)
    return jnp.where(row < s, fill, rolled)


def _causal_conv(x, halo_ref, w_ref, b_ref):
    n = x.shape[0]
    halo_ref[pl.ds(HALO, n), :] = x
    out = b_ref[...] + w_ref[CONV_K - 1:CONV_K, :] * x
    for s in range(1, CONV_K):
        out = out + w_ref[CONV_K - 1 - s:CONV_K - s, :] * halo_ref[pl.ds(HALO - s, n), :]
    halo_ref[pl.ds(0, HALO), :] = x[n - HALO:n]
    return out


def _rglru_kernel(rg_ref, cw_ref, cb_ref, wa_ref, ba_ref, wx_ref, bx_ref, lam_ref, o_ref, halo_ref, carry_ref):
    n = rg_ref.shape[0]

    @pl.when(pl.program_id(1) == 0)
    def _():
        halo_ref[pl.ds(0, HALO), :] = jnp.zeros((HALO, halo_ref.shape[1]), F32)
        carry_ref[...] = jnp.zeros_like(carry_ref)

    x = rg_ref[:, 0:W_LRU]
    xc = _causal_conv(x, halo_ref, cw_ref, cb_ref)
    xb = xc.astype(BF)
    r = _sigmoid(jnp.dot(xb, wa_ref[...], preferred_element_type=F32) + ba_ref[...])
    i = _sigmoid(jnp.dot(xb, wx_ref[...], preferred_element_type=F32) + bx_ref[...])
    log_a = (-LRU_C * r) * _softplus(-lam_ref[...])
    a = jnp.exp(log_a)
    u = jnp.sqrt(-jnp.tanh(log_a) * (a * a + 1.0)) * (i * xc)
    k = 1
    while k < n:
        u = a * _shift_rows(u, k, 0.0) + u
        a = a * _shift_rows(a, k, 1.0)
        k *= 2
    h = u + a * carry_ref[0:1, :]
    carry_ref[...] = jnp.broadcast_to(h[n - 1:n, :], carry_ref.shape)
    o_ref[...] = _gelu_tanh(rg_ref[:, W_LRU:2 * W_LRU]) * h


def _rg_lru(rg, seq, conv_w, conv_b, wa, ba, wx, bx, lam):
    t = rg.shape[0]
    nb = seq // ROW_TILE
    row = lambda b, j: (b * nb + j, 0)
    const = lambda b, j: (0, 0)
    vec = pl.BlockSpec((1, W_LRU), const)
    return pl.pallas_call(
        _rglru_kernel,
        out_shape=jax.ShapeDtypeStruct((t, W_LRU), F32),
        grid=(t // seq, nb),
        in_specs=[pl.BlockSpec((ROW_TILE, 2 * W_LRU), row),
                  pl.BlockSpec((CONV_K, W_LRU), const), vec,
                  pl.BlockSpec((W_LRU, W_LRU), const), vec,
                  pl.BlockSpec((W_LRU, W_LRU), const), vec, vec],
        out_specs=pl.BlockSpec((ROW_TILE, W_LRU), row),
        scratch_shapes=[pltpu.VMEM((HALO + ROW_TILE, W_LRU), F32), pltpu.VMEM((SUBLANES, W_LRU), F32)],
        compiler_params=_cparams(("parallel", "arbitrary")),
        name="rg_lru",
    )(rg, conv_w, conv_b.reshape(1, -1), wa, ba.reshape(1, -1), wx, bx.reshape(1, -1), lam.reshape(1, -1))


def _block_diag(w):
    nb, c, _ = w.shape
    eye = jnp.eye(nb, dtype=w.dtype)
    return (eye[:, None, :, None] * w[:, :, None, :]).reshape(nb * c, nb * c).astype(BF)


def _ssd_kernel(z_ref, xbc_ref, dt_ref, cw_ref, cb_ref, dtb_ref, alog_ref, dskip_ref, ng_ref, expand_ref,
                o_ref, halo_ref, state_ref):
    L = SSM_CHUNK

    @pl.when(pl.program_id(1) == 0)
    def _():
        halo_ref[pl.ds(0, HALO), :] = jnp.zeros((HALO, halo_ref.shape[1]), F32)
        state_ref[...] = jnp.zeros_like(state_ref)

    xbc = xbc_ref[...]
    conv = _silu(_causal_conv(xbc, halo_ref, cw_ref, cb_ref))
    xs = conv[:, 0:SSM_D_INNER]
    bm = conv[:, SSM_D_INNER:SSM_D_INNER + SSM_GROUPS * SSM_STATE]
    cm = conv[:, SSM_D_INNER + SSM_GROUPS * SSM_STATE:]

    dt = _softplus(dt_ref[...] + dtb_ref[...])
    da = dt * (-jnp.exp(alog_ref[...]))
    row = lax.broadcasted_iota(jnp.int32, (L, L), 0)
    col = lax.broadcasted_iota(jnp.int32, (L, L), 1)
    causal = col <= row
    a_cs = sum(jnp.dot(causal.astype(BF), part, preferred_element_type=F32) for part in _split3(da))
    a_cs_t = a_cs.T
    expand = expand_ref[...]
    dt_full = sum(jnp.dot(part, expand, preferred_element_type=F32) for part in _split3(dt))
    acs_full = sum(jnp.dot(part, expand, preferred_element_type=F32) for part in _split3(a_cs))
    alast_full = acs_full[L - 1:L, :]
    xdt = xs * dt_full
    xdt_b = xdt.astype(BF)
    xst_b = (xdt * jnp.exp(alast_full - acs_full)).astype(BF)
    left = lax.broadcasted_iota(jnp.int32, (L, LANES), 1) < SSM_HEAD_DIM

    y_parts = []
    for g in range(SSM_GROUPS):
        cg = cm[:, g * SSM_STATE:(g + 1) * SSM_STATE].astype(BF)
        bg = bm[:, g * SSM_STATE:(g + 1) * SSM_STATE].astype(BF)
        cb = lax.dot_general(cg, bg, (((1,), (1,)), ((), ())), preferred_element_type=F32)
        st = state_ref[:, g * SSD_GROUP_W:(g + 1) * SSD_GROUP_W]
        y_off = jnp.dot(cg, st.astype(BF), preferred_element_type=F32)
        for j in range(SSD_GROUP_W // LANES):
            h0 = g * (SSM_HEADS // SSM_GROUPS) + 2 * j
            ms = []
            for h in (h0, h0 + 1):
                seg = a_cs[:, h:h + 1] - a_cs_t[h:h + 1, :]
                ms.append((cb * jnp.exp(jnp.where(causal, seg, -jnp.inf))).astype(BF))
            c0 = h0 * SSM_HEAD_DIM
            x2 = xdt_b[:, c0:c0 + LANES]
            zero = jnp.zeros_like(x2)
            xblk = jnp.concatenate([jnp.where(left, x2, zero), jnp.where(left, zero, x2)], axis=0)
            y_parts.append(jnp.dot(jnp.concatenate(ms, axis=1), xblk, preferred_element_type=F32)
                           + y_off[:, j * LANES:(j + 1) * LANES] * jnp.exp(acs_full[:, c0:c0 + LANES]))
        new = lax.dot_general(bg, xst_b[:, g * SSD_GROUP_W:(g + 1) * SSD_GROUP_W], (((0,), (0,)), ((), ())),
                              preferred_element_type=F32)
        state_ref[:, g * SSD_GROUP_W:(g + 1) * SSD_GROUP_W] = (
            st * jnp.exp(alast_full[:, g * SSD_GROUP_W:(g + 1) * SSD_GROUP_W]) + new)
    y = jnp.concatenate(y_parts, axis=1) + dskip_ref[...] * xs
    y = y * _silu(z_ref[...])
    outs = []
    for g in range(SSM_GROUPS):
        yg = y[:, g * SSD_GROUP_W:(g + 1) * SSD_GROUP_W]
        outs.append(yg * lax.rsqrt(jnp.mean(yg * yg, axis=-1, keepdims=True) + LN_EPS))
    o_ref[...] = jnp.concatenate(outs, axis=1) * ng_ref[...]


def _ssd(z, xbc, dt, seq, conv_w, conv_b, dt_bias, a_log, d_skip, norm_g):
    t = z.shape[0]
    nc = seq // SSM_CHUNK
    row = lambda b, c: (b * nc + c, 0)
    const = lambda b, c: (0, 0)
    pad = LANES - SSM_HEADS
    expand = (jnp.arange(LANES)[:, None] == (jnp.arange(SSM_D_INNER)[None, :] // SSM_HEAD_DIM)).astype(BF)
    return pl.pallas_call(
        _ssd_kernel,
        out_shape=jax.ShapeDtypeStruct((t, SSM_D_INNER), F32),
        grid=(t // seq, nc),
        in_specs=[pl.BlockSpec((SSM_CHUNK, SSM_D_INNER), row),
                  pl.BlockSpec((SSM_CHUNK, SSM_CONV_CH), row),
                  pl.BlockSpec((SSM_CHUNK, LANES), row),
                  pl.BlockSpec((CONV_K, SSM_CONV_CH), const),
                  pl.BlockSpec((1, SSM_CONV_CH), const),
                  pl.BlockSpec((1, LANES), const),
                  pl.BlockSpec((1, LANES), const),
                  pl.BlockSpec((1, SSM_D_INNER), const),
                  pl.BlockSpec((1, SSM_D_INNER), const),
                  pl.BlockSpec((LANES, SSM_D_INNER), const)],
        out_specs=pl.BlockSpec((SSM_CHUNK, SSM_D_INNER), row),
        scratch_shapes=[pltpu.VMEM((HALO + SSM_CHUNK, SSM_CONV_CH), F32), pltpu.VMEM((SSM_STATE, SSM_D_INNER), F32)],
        compiler_params=_cparams(("parallel", "arbitrary")),
        name="ssd",
    )(z, xbc, dt, conv_w, conv_b.reshape(1, -1), jnp.pad(dt_bias, (0, pad)).reshape(1, -1),
      jnp.pad(a_log, (0, pad)).reshape(1, -1), jnp.repeat(d_skip, SSM_HEAD_DIM).reshape(1, -1),
      norm_g.reshape(1, -1), expand)


def _swa_kernel(sink_ref, q_ref, kv_ref, kvp_ref, o_ref):
    L = ATTN_BLOCK
    lane_k = lax.broadcasted_iota(jnp.int32, (2 * L, LANES), 1)
    qi = lax.broadcasted_iota(jnp.int32, (2 * L, 2 * L), 0) % L
    kj = lax.broadcasted_iota(jnp.int32, (2 * L, 2 * L), 1)
    rel = qi + L - kj
    band = (rel >= 0) & (rel < L)
    for blk in range(ATTN_STEP_BLOCKS):
        rows = pl.ds(blk * L, L)
        prev = kvp_ref[...] if blk == 0 else kv_ref[pl.ds((blk - 1) * L, L), :]
        kv = jnp.concatenate([prev, kv_ref[rows, :]], axis=0)
        valid = band & (jnp.logical_not(pl.program_id(1) == 0) | (kj >= L)) if blk == 0 else band
        _swa_block(sink_ref, q_ref, o_ref, rows, kv, valid, lane_k)


def _swa_block(sink_ref, q_ref, o_ref, rows, kv, valid, lane_k):
    L = ATTN_BLOCK
    hd = ATTN_HEAD_DIM
    kk = kv[:, 0:LANES]
    vv = kv[:, LANES:2 * LANES].astype(BF)
    top = lax.broadcasted_iota(jnp.int32, (2 * L, 1), 0) < L
    lane_o = lax.broadcasted_iota(jnp.int32, (L, LANES), 1)
    scale = hd ** -0.5
    for j in range(ATTN_Q_HEADS // 2):
        g = (2 * j) // ATTN_REP
        q2 = q_ref[rows, j * LANES:(j + 1) * LANES]
        q2r = pltpu.roll(q2, hd, axis=1)
        kg = jnp.where((lane_k >= g * hd) & (lane_k < (g + 1) * hd), kk, 0.0).astype(BF)
        qa, qb = (q2, q2r) if g == 0 else (q2r, q2)
        qs = jnp.concatenate([qa, qb], axis=0).astype(BF)
        logits = lax.dot_general(qs, kg, (((1,), (1,)), ((), ())), preferred_element_type=F32) * scale
        logits = jnp.where(valid, logits, -jnp.inf)
        sink = jnp.where(top, sink_ref[2 * j], sink_ref[2 * j + 1])
        m = jnp.maximum(jnp.max(logits, axis=-1, keepdims=True), sink)
        p = jnp.exp(logits - m)
        probs = p * (1.0 / (jnp.sum(p, axis=-1, keepdims=True) + jnp.exp(sink - m)))
        o = jnp.dot(probs.astype(BF), vv, preferred_element_type=F32)
        oa, ob = o[0:L], o[L:2 * L]
        if g == 0:
            out2 = jnp.where(lane_o < hd, oa, pltpu.roll(ob, hd, axis=1))
        else:
            out2 = jnp.where(lane_o < hd, pltpu.roll(oa, hd, axis=1), ob)
        o_ref[rows, j * LANES:(j + 1) * LANES] = out2


def _swa(q, kv, seq, sinks):
    t = q.shape[0]
    sb = ATTN_STEP_BLOCKS
    rows = sb * ATTN_BLOCK
    ns = seq // rows
    tile = lambda b, n: (b * ns + n, 0)
    before = lambda b, n: (b * ns * sb + jnp.maximum(n * sb - 1, 0), 0)
    return pl.pallas_call(
        _swa_kernel,
        out_shape=jax.ShapeDtypeStruct((t, ATTN_Q_W), F32),
        grid=(t // seq, ns),
        in_specs=[pl.BlockSpec(memory_space=pltpu.SMEM),
                  pl.BlockSpec((rows, ATTN_Q_W), tile),
                  pl.BlockSpec((rows, ATTN_KV_W), tile),
                  pl.BlockSpec((ATTN_BLOCK, ATTN_KV_W), before)],
        out_specs=pl.BlockSpec((rows, ATTN_Q_W), tile),
        compiler_params=_cparams(("parallel", "parallel")),
        name="swa",
    )(sinks, q, kv, kv)


def _outproj_kernel(ya_ref, yb_ref, yc_ref, h_ref, wa_ref, wb_ref, wc_ref, g_ref, b_ref, o_ref):
    mix = (jnp.dot(ya_ref[...].astype(BF), wa_ref[...], preferred_element_type=F32)
           + jnp.dot(yb_ref[...].astype(BF), wb_ref[...], preferred_element_type=F32)
           + jnp.dot(yc_ref[...].astype(BF), wc_ref[...], preferred_element_type=F32))
    o_ref[...] = _ln_math(DN_ALPHA * h_ref[...] + mix, g_ref[...], b_ref[...])


def _out_proj(ya, yb, yc, h, w_out, g, b):
    t = h.shape[0]
    w = w_out.astype(BF)
    row = lambda i: (i, 0)
    const = lambda i: (0, 0)
    c1, c2 = W_LRU, W_LRU + SSM_D_INNER
    return pl.pallas_call(
        _outproj_kernel,
        out_shape=jax.ShapeDtypeStruct((t, D_MODEL), F32),
        grid=(t // ROW_TILE,),
        in_specs=[pl.BlockSpec((ROW_TILE, W_LRU), row), pl.BlockSpec((ROW_TILE, SSM_D_INNER), row),
                  pl.BlockSpec((ROW_TILE, ATTN_Q_W), row), pl.BlockSpec((ROW_TILE, D_MODEL), row),
                  pl.BlockSpec((W_LRU, D_MODEL), const), pl.BlockSpec((SSM_D_INNER, D_MODEL), const),
                  pl.BlockSpec((ATTN_Q_W, D_MODEL), const),
                  pl.BlockSpec((1, D_MODEL), const), pl.BlockSpec((1, D_MODEL), const)],
        out_specs=pl.BlockSpec((ROW_TILE, D_MODEL), row),
        compiler_params=_cparams(("parallel",)),
        name="out_proj",
    )(ya, yb, yc, h, w[0:c1], w[c1:c2], w[c2:MIX_WIDTH], g.reshape(1, -1), b.reshape(1, -1))


def _gather_group(idx_ref, tbl_ref, t, g):
    rs = ROW_WORDS_SUBLANES
    pieces = []
    for q in range(GROUP_PAIRS // 2):
        p = g * GROUP_PAIRS + 2 * q
        if p % SUBLANES == 0:
            window = idx_ref.at[t, pl.ds(p, SUBLANES)]
        ra = pl.multiple_of(window[p % SUBLANES], rs)
        rb = pl.multiple_of(window[p % SUBLANES + 1], rs)
        pieces.append(jnp.concatenate([tbl_ref[pl.ds(ra, rs), :], tbl_ref[pl.ds(rb, rs), :]], axis=0))
    return pltpu.bitcast(jnp.concatenate(pieces, axis=0), BF)


def _chunk_of_column(col):
    j = col % PAIR_LANES
    return j // 2 + ROW_WORDS_SUBLANES * (j % 2)


def _split_hi_lo(x):
    hi = x.astype(BF)
    lo = (x - hi.astype(F32)).astype(BF)
    return hi, lo


def _peer_in_kernel(idx_ref, x_ref, gate_ref, tbl_ref, o_ref, m_ref):
    n = x_ref.shape[0]
    row = lax.broadcasted_iota(jnp.int32, (2 * SUBLANES, GROUP_K), 0) % SUBLANES
    keep = row == _chunk_of_column(lax.broadcasted_iota(jnp.int32, (2 * SUBLANES, GROUP_K), 1))
    for t in range(n):
        x_row = x_ref[pl.ds(t, 1), :]
        x8 = jnp.concatenate([x_row[:, c * LANES:(c + 1) * LANES] for c in range(TOKEN_SUBLANES)], axis=0)
        x_hi, x_lo = _split_hi_lo(x8)
        x16 = jnp.concatenate([x_hi, x_lo], axis=0)
        for g in range(PEER_PAIRS // GROUP_PAIRS):
            r = lax.dot_general(x16, _gather_group(idx_ref, tbl_ref, t, g), (((1,), (1,)), ((), ())),
                                preferred_element_type=F32)
            m_ref[pl.ds(t, 1), g * GROUP_K:(g + 1) * GROUP_K] = jnp.sum(jnp.where(keep, r, 0.0), axis=0,
                                                                        keepdims=True)
    wide = PEER_PAIRS * PAIR_LANES
    fold = (lax.broadcasted_iota(jnp.int32, (wide, PEER_PAIRS), 0) // PAIR_LANES
            == lax.broadcasted_iota(jnp.int32, (wide, PEER_PAIRS), 1)).astype(BF)
    m_hi, m_lo = _split_hi_lo(m_ref[...])
    a = jnp.dot(m_hi, fold, preferred_element_type=F32) + jnp.dot(m_lo, fold, preferred_element_type=F32)
    o_ref[...] = gate_ref[...] * (0.5 * a * (1.0 + lax.erf(a * (2.0 ** -0.5))))


def _peer_in(rows, x, gate, tbl):
    t = x.shape[0]
    tt = GATHER_TOKENS
    pairs = pl.BlockSpec((tt, PEER_PAIRS), lambda i: (i, 0))
    return pl.pallas_call(
        _peer_in_kernel,
        out_shape=jax.ShapeDtypeStruct((t, PEER_PAIRS), F32),
        grid=(t // tt,),
        in_specs=[
            pl.BlockSpec((tt, PEER_PAIRS), lambda i: (i, 0), memory_space=pltpu.SMEM),
            pl.BlockSpec((tt, D_MODEL), lambda i: (i, 0)),
            pairs,
            pl.BlockSpec(memory_space=pltpu.VMEM),
        ],
        out_specs=pairs,
        scratch_shapes=[pltpu.VMEM((tt, PEER_PAIRS * PAIR_LANES), F32)],
        compiler_params=pltpu.CompilerParams(dimension_semantics=("parallel",), vmem_limit_bytes=PEER_VMEM_LIMIT),
        name="peer_in",
    )(rows, x, gate, tbl)


def _peer_out_kernel(idx_ref, w_ref, tbl_ref, h_ref, g_ref, b_ref, o_ref, wide_ref, ffn_ref):
    n = w_ref.shape[0]
    wide = PEER_PAIRS * PAIR_LANES
    rep = (lax.broadcasted_iota(jnp.int32, (PEER_PAIRS, wide), 1) // PAIR_LANES
           == lax.broadcasted_iota(jnp.int32, (PEER_PAIRS, wide), 0)).astype(BF)
    for k, wv in enumerate(_split_hi_lo(w_ref[...])):
        wide_ref[k] = jnp.dot(wv, rep, preferred_element_type=F32)
    keep = (lax.broadcasted_iota(jnp.int32, (SUBLANES, GROUP_K), 0)
            == _chunk_of_column(lax.broadcasted_iota(jnp.int32, (SUBLANES, GROUP_K), 1)))
    for t in range(n):
        acc = jnp.zeros((2 * SUBLANES, LANES), F32)
        for g in range(PEER_PAIRS // GROUP_PAIRS):
            cols = slice(g * GROUP_K, (g + 1) * GROUP_K)
            lhs = jnp.concatenate(
                [jnp.where(keep, jnp.broadcast_to(wide_ref[k, pl.ds(t, 1), cols], (SUBLANES, GROUP_K)), 0.0)
                 for k in range(2)], axis=0).astype(BF)
            acc = acc + jnp.dot(lhs, _gather_group(idx_ref, tbl_ref, t, g), preferred_element_type=F32)
        out8 = acc[0:SUBLANES] + acc[SUBLANES:]
        for c in range(TOKEN_SUBLANES):
            ffn_ref[pl.ds(t, 1), c * LANES:(c + 1) * LANES] = out8[c:c + 1, :]
    o_ref[...] = _ln_math(DN_ALPHA * h_ref[...] + ffn_ref[...], g_ref[...], b_ref[...])


def _peer_out(rows, w, tbl, h, g, b):
    t = h.shape[0]
    tt = GATHER_TOKENS
    row = pl.BlockSpec((tt, D_MODEL), lambda i: (i, 0))
    vec = pl.BlockSpec((1, D_MODEL), lambda i: (0, 0))
    return pl.pallas_call(
        _peer_out_kernel,
        out_shape=jax.ShapeDtypeStruct((t, D_MODEL), F32),
        grid=(t // tt,),
        in_specs=[
            pl.BlockSpec((tt, PEER_PAIRS), lambda i: (i, 0), memory_space=pltpu.SMEM),
            pl.BlockSpec((tt, PEER_PAIRS), lambda i: (i, 0)),
            pl.BlockSpec(memory_space=pltpu.VMEM),
            row, vec, vec,
        ],
        out_specs=row,
        scratch_shapes=[pltpu.VMEM((2, tt, PEER_PAIRS * PAIR_LANES), F32), pltpu.VMEM((tt, D_MODEL), F32)],
        compiler_params=pltpu.CompilerParams(vmem_limit_bytes=PEER_VMEM_LIMIT),
        name="peer_out",
    )(rows, w, tbl, h, g.reshape(1, D_MODEL), b.reshape(1, D_MODEL))


def _extract_topk(s, pos, k, payload=None):
    big = jnp.asarray(2 ** 30, pos.dtype)
    vals, poss, pays = [], [], []
    for _ in range(k):
        m = jnp.max(s, axis=0, keepdims=True)
        j = jnp.min(jnp.where(s == m, pos, big), axis=0, keepdims=True)
        sel = pos == j
        vals.append(m)
        poss.append(j)
        if payload is not None:
            pays.append(jnp.sum(jnp.where(sel, payload, jnp.zeros_like(payload)), axis=0, keepdims=True))
        s = jnp.where(sel, -jnp.inf, s)
    out = [jnp.concatenate(vals, axis=0), jnp.concatenate(poss, axis=0)]
    if payload is not None:
        out.append(jnp.concatenate(pays, axis=0))
    return out


def _topk_two_per_slot(s, k):
    m, n = s.shape[0] // 2, s.shape[1]
    big = jnp.asarray(2 ** 30, F32)
    pos_a = lax.broadcasted_iota(jnp.int32, (m, n), 0).astype(F32)
    a, b = s[:m], s[m:]
    b_wins = b > a
    win, wait = jnp.where(b_wins, b, a), jnp.where(b_wins, a, b)
    win_pos, wait_pos = jnp.where(b_wins, pos_a + m, pos_a), jnp.where(b_wins, pos_a, pos_a + m)
    vals, poss = [], []
    for _ in range(k):
        top = jnp.max(win, axis=0, keepdims=True)
        j = jnp.min(jnp.where(win == top, win_pos, big), axis=0, keepdims=True)
        sel = win_pos == j
        vals.append(top)
        poss.append(j)
        win = jnp.where(sel, wait, win)
        win_pos = jnp.where(sel, wait_pos, win_pos)
        wait = jnp.where(sel, -jnp.inf, wait)
    return jnp.concatenate(vals, axis=0), jnp.concatenate(poss, axis=0)


def _peer_route_kernel(h_ref, wqt_ref, keys_ref, idx_ref, gate_ref):
    half_dim = PEER_DKEY // 2
    hb = h_ref[...].astype(BF)
    n = hb.shape[0]
    iota16 = lax.broadcasted_iota(jnp.int32, (PEER_TOPK, n), 0).astype(F32)
    iota8 = lax.broadcasted_iota(jnp.int32, (SUBLANES, n), 0).astype(F32)
    qt = lax.dot_general(wqt_ref[...], hb, (((1,), (1,)), ((), ())), preferred_element_type=F32)
    idx_rows, gate_rows = [], []
    for hd in range(PEER_HEADS):
        tv, ti = [], []
        for half in range(2):
            r0 = hd * PEER_DKEY + half * half_dim
            s = jnp.dot(keys_ref[half], qt[r0:r0 + half_dim, :], preferred_element_type=F32)
            v, i = _topk_two_per_slot(s, PEER_TOPK)
            tv.append(v)
            ti.append(i)
        v1_lo, i1_lo = tv[1][0:SUBLANES, :], ti[1][0:SUBLANES, :]

        def block(a, first_row):
            shift = (lambda x: x) if first_row == 0 else (lambda x: pltpu.roll(x, first_row, axis=0))
            return (tv[0][a:a + 1, :] + shift(v1_lo), iota8 + (a * PEER_TOPK - first_row),
                    ti[0][a:a + 1, :] * PEER_NKEYS + shift(i1_lo))

        def pick(cond, x, y):
            return tuple(jnp.where(cond, xi, yi) for xi, yi in zip(x, y))

        padding = (jnp.full_like(v1_lo, -jnp.inf), iota8 + 2 ** 20, jnp.zeros_like(i1_lo))
        tiles = [block(1, 0),
                 pick(iota8 < 5, block(2, 0), block(4, 5)),
                 pick(iota8 < 4, block(3, 0), pick(iota8 < 6, block(5, 4), block(6, 6))),
                 pick(iota8 < 2, block(7, 0), padding),
                 (tv[0][SUBLANES:, :] + tv[1][0:1, :], (iota8 + SUBLANES) * PEER_TOPK,
                  ti[0][SUBLANES:, :] * PEER_NKEYS + ti[1][0:1, :])]
        cs = [tv[0][0:1, :] + tv[1]] + [tile[0] for tile in tiles]
        cp = [iota16] + [tile[1] for tile in tiles]
        ce = [ti[0][0:1, :] * PEER_NKEYS + ti[1]] + [tile[2] for tile in tiles]
        best_s, _, best_e = _extract_topk(jnp.concatenate(cs, axis=0), jnp.concatenate(cp, axis=0), PEER_TOPK,
                                          payload=jnp.concatenate(ce, axis=0))
        ex = jnp.exp(best_s - best_s[0:1, :])
        gate_rows.append(ex / jnp.sum(ex, axis=0, keepdims=True))
        idx_rows.append(best_e)
    idx_ref[...] = (jnp.concatenate(idx_rows, axis=0) * ROW_WORDS_SUBLANES).astype(jnp.int32).T
    gate_ref[...] = jnp.concatenate(gate_rows, axis=0).T


def _peer_route(h, wqt, keys):
    t = h.shape[0]
    rt = ROUTE_TOKENS
    pairs = pl.BlockSpec((rt, PEER_PAIRS), lambda i: (i, 0))
    return pl.pallas_call(
        _peer_route_kernel,
        out_shape=(jax.ShapeDtypeStruct((t, PEER_PAIRS), jnp.int32), jax.ShapeDtypeStruct((t, PEER_PAIRS), F32)),
        grid=(t // rt,),
        in_specs=[
            pl.BlockSpec((rt, D_MODEL), lambda i: (i, 0)),
            pl.BlockSpec((D_MODEL, D_MODEL), lambda i: (0, 0)),
            pl.BlockSpec((2, PEER_NKEYS, PEER_DKEY // 2), lambda i: (0, 0, 0)),
        ],
        out_specs=(pairs, pairs),
        compiler_params=_cparams(("parallel",)),
        name="peer_route",
    )(h, wqt, keys)


def _pack_kernel(u_ref, o_ref):
    half = D_MODEL // 2
    n = u_ref.shape[0]
    rs = ROW_WORDS_SUBLANES
    lo = lax.bitcast_convert_type(u_ref[:, 0:half].astype(BF).astype(F32), jnp.uint32) >> 16
    hi = lax.bitcast_convert_type(u_ref[:, half:D_MODEL].astype(BF).astype(F32), jnp.uint32) & jnp.uint32(0xFFFF0000)
    w = lo | hi
    for s in range(rs):
        o_ref[pl.ds(s, n, stride=rs), :] = w[:, s * LANES:(s + 1) * LANES]


def _pack_table(u):
    e = u.shape[0]
    return pl.pallas_call(
        _pack_kernel,
        out_shape=jax.ShapeDtypeStruct((e * ROW_WORDS_SUBLANES, LANES), jnp.uint32),
        grid=(e // ROW_TILE,),
        in_specs=[pl.BlockSpec((ROW_TILE, D_MODEL), lambda i: (i, 0))],
        out_specs=pl.BlockSpec((ROW_TILE * ROW_WORDS_SUBLANES, LANES), lambda i: (i, 0)),
        compiler_params=_cparams(("parallel",)),
        name="pack_table",
    )(u)


def _peer_sublayer(h, wq, keys, u, v, ln_g, ln_b):
    rows, gate = _peer_route(h, wq.T.astype(BF), keys)
    w = _peer_in(rows, h, gate, _pack_table(u))
    return _peer_out(rows, w, _pack_table(v), h, ln_g, ln_b)


def kernel(x, emb_ln_g, emb_ln_b, w_in, rg_conv_w, rg_conv_b, rg_wa, rg_ba, rg_wx, rg_bx, rg_lambda, ssm_conv_w, ssm_conv_b, ssm_dt_bias, ssm_a_log, ssm_d, ssm_norm_g, attn_sinks, w_out, ln1_g, ln1_b, peer_wq, peer_keys, peer_u, peer_v, ln2_g, ln2_b):
    bsz, seq, d = x.shape
    assert d == D_MODEL and seq % ROW_TILE == 0 and (bsz * seq) % ROUTE_TOKENS == 0
    assert rg_wa.shape[1:] == (LRU_BLOCKS, W_LRU // LRU_BLOCKS, W_LRU // LRU_BLOCKS)
    h = x.reshape(bsz * seq, d)
    for l in range(DEPTH):
        h, (rg, z, xbc, dt, q, kv) = _in_proj(h, _split_w_in(w_in[l]), (emb_ln_g, emb_ln_b) if l == 0 else None)
        y_a = _rg_lru(rg, seq, rg_conv_w[l], rg_conv_b[l], _block_diag(rg_wa[l]), rg_ba[l],
                      _block_diag(rg_wx[l]), rg_bx[l], rg_lambda[l])
        y_b = _ssd(z, xbc, dt, seq, ssm_conv_w[l], ssm_conv_b[l], ssm_dt_bias[l], ssm_a_log[l], ssm_d[l],
                   ssm_norm_g[l])
        y_c = _swa(q, kv, seq, attn_sinks[l])
        h = _out_proj(y_a, y_b, y_c, h, w_out[l], ln1_g[l], ln1_b[l])
        h = _peer_sublayer(h, peer_wq[l], peer_keys[l], peer_u[l], peer_v[l], ln2_g[l], ln2_b[l])
    return h.reshape(bsz, seq, d)
```

```python
import functools
import math

import jax
import jax.numpy as jnp
from jax import lax
from jax.experimental import pallas as pl
from jax.experimental.pallas import tpu as pltpu

D_MODEL = 1024
DEPTH = 2

W_LRU = D_MODEL // 2
LRU_BLOCKS = 8
LRU_C = 8.0
CONV_K = 4

SSM_HEAD_DIM = 64
SSM_D_INNER = D_MODEL
SSM_HEADS = SSM_D_INNER // SSM_HEAD_DIM
SSM_GROUPS = 2
SSM_STATE = 128
SSM_CHUNK = 128
SSM_CONV_CH = SSM_D_INNER + 2 * SSM_GROUPS * SSM_STATE
SSD_GROUP_W = SSM_D_INNER // SSM_GROUPS

ATTN_HEAD_DIM = 64
ATTN_Q_HEADS = (D_MODEL // 2) // ATTN_HEAD_DIM
ATTN_KV_HEADS = 2
ATTN_REP = ATTN_Q_HEADS // ATTN_KV_HEADS
ATTN_BLOCK = 128
ATTN_STEP_BLOCKS = 2
ATTN_Q_W = ATTN_Q_HEADS * ATTN_HEAD_DIM
ATTN_KV_W = 2 * ATTN_KV_HEADS * ATTN_HEAD_DIM

MIX_WIDTH = W_LRU + SSM_D_INNER + ATTN_Q_W

PEER_HEADS = 8
PEER_NKEYS = 128
PEER_DKEY = 128
PEER_TOPK = 16
PEER_PAIRS = PEER_HEADS * PEER_TOPK

DN_ALPHA = (2 * DEPTH) ** 0.25
LN_EPS = 1e-5

SUBLANES = 8
LANES = 128
HALO = SUBLANES
ROW_WORDS_SUBLANES = D_MODEL // 2 // LANES
TOKEN_SUBLANES = D_MODEL // LANES

ROW_TILE = 512
PAIR_LANES = 2 * ROW_WORDS_SUBLANES
GROUP_PAIRS = 32
GROUP_K = GROUP_PAIRS * PAIR_LANES
GATHER_TOKENS = 128
ROUTE_TOKENS = 256
VMEM_LIMIT = 48 * 1024 * 1024
PEER_VMEM_LIMIT = 56 * 1024 * 1024

BF = jnp.bfloat16
F32 = jnp.float32

PROJ_SEGS = (2 * W_LRU, SSM_D_INNER, SSM_CONV_CH, LANES, ATTN_Q_W, ATTN_KV_W)


def _cparams(sem):
    return pltpu.CompilerParams(dimension_semantics=sem, vmem_limit_bytes=VMEM_LIMIT)


def _softplus(x):
    return jnp.maximum(x, 0.0) + jnp.log1p(jnp.exp(-jnp.abs(x)))


def _sigmoid(x):
    return 1.0 / (1.0 + jnp.exp(-x))


def _silu(x):
    return x * _sigmoid(x)


def _split3(x):
    hi = x.astype(BF)
    rest = x - hi.astype(F32)
    mid = rest.astype(BF)
    return hi, mid, (rest - mid.astype(F32)).astype(BF)


def _gelu_tanh(x):
    return 0.5 * x * (1.0 + jnp.tanh(math.sqrt(2.0 / math.pi) * (x + 0.044715 * (x * x * x))))


def _ln_math(x, g, b):
    mu = jnp.mean(x, axis=-1, keepdims=True)
    xc = x - mu
    var = jnp.mean(xc * xc, axis=-1, keepdims=True)
    return xc * lax.rsqrt(var + LN_EPS) * g + b


def _inproj_kernel(*refs, entry_norm):
    n = len(PROJ_SEGS)
    if entry_norm:
        x_ref, g_ref, b_ref = refs[:3]
        w_refs, h_out_ref, o_refs = refs[3:3 + n], refs[3 + n], refs[4 + n:]
        h = _ln_math(x_ref[...], g_ref[...], b_ref[...])
        h_out_ref[...] = h
    else:
        w_refs, o_refs = refs[1:1 + n], refs[1 + n:]
        h = refs[0][...]
    hb = h.astype(BF)
    for w_ref, o_ref in zip(w_refs, o_refs):
        o_ref[...] = jnp.dot(hb, w_ref[...], preferred_element_type=F32)


def _in_proj(h, ws, entry_ln=None):
    t = h.shape[0]
    row = lambda w: pl.BlockSpec((ROW_TILE, w), lambda i: (i, 0))
    vec = pl.BlockSpec((1, D_MODEL), lambda i: (0, 0))
    norm = entry_ln is not None
    outs = pl.pallas_call(
        functools.partial(_inproj_kernel, entry_norm=norm),
        out_shape=tuple(jax.ShapeDtypeStruct((t, w), F32) for w in ((D_MODEL,) if norm else ()) + PROJ_SEGS),
        grid=(t // ROW_TILE,),
        in_specs=[row(D_MODEL)] + ([vec, vec] if norm else [])
        + [pl.BlockSpec((D_MODEL, w), lambda i: (0, 0)) for w in PROJ_SEGS],
        out_specs=tuple(row(w) for w in ((D_MODEL,) if norm else ()) + PROJ_SEGS),
        compiler_params=_cparams(("parallel",)),
        name="in_proj",
    )(h, *([v.reshape(1, D_MODEL) for v in entry_ln] if norm else []), *ws)
    return (outs[0], outs[1:]) if norm else (h, outs)


def _split_w_in(w_in):
    w = w_in.astype(BF)
    c = [0]
    for width in (2 * W_LRU, SSM_D_INNER, SSM_CONV_CH, SSM_HEADS, ATTN_Q_W, ATTN_KV_W):
        c.append(c[-1] + width)
    segs = [w[:, c[i]:c[i + 1]] for i in range(6)]
    segs[3] = jnp.pad(segs[3], ((0, 0), (0, LANES - SSM_HEADS)))
    return tuple(segs)


def _shift_rows(x, s, fill):
    n = x.shape[0]
    if s % SUBLANES == 0:
        return jnp.concatenate([jnp.full((s,) + x.shape[1:], fill, x.dtype), x[:n - s]], axis=0)
    rolled = pltpu.roll(x, s, axis=0)
    row = lax.broadcasted_iota(jnp.int32, x.shape, 0)
    return jnp.where(row < s, fill, rolled)


def _causal_conv(x, halo_ref, w_ref, b_ref):
    n = x.shape[0]
    halo_ref[pl.ds(HALO, n), :] = x
    out = b_ref[...] + w_ref[CONV_K - 1:CONV_K, :] * x
    for s in range(1, CONV_K):
        out = out + w_ref[CONV_K - 1 - s:CONV_K - s, :] * halo_ref[pl.ds(HALO - s, n), :]
    halo_ref[pl.ds(0, HALO), :] = x[n - HALO:n]
    return out


def _rglru_kernel(rg_ref, cw_ref, cb_ref, wa_ref, ba_ref, wx_ref, bx_ref, lam_ref, o_ref, halo_ref, carry_ref):
    n = rg_ref.shape[0]

    @pl.when(pl.program_id(1) == 0)
    def _():
        halo_ref[pl.ds(0, HALO), :] = jnp.zeros((HALO, halo_ref.shape[1]), F32)
        carry_ref[...] = jnp.zeros_like(carry_ref)

    x = rg_ref[:, 0:W_LRU]
    xc = _causal_conv(x, halo_ref, cw_ref, cb_ref)
    xb = xc.astype(BF)
    r = _sigmoid(jnp.dot(xb, wa_ref[...], preferred_element_type=F32) + ba_ref[...])
    i = _sigmoid(jnp.dot(xb, wx_ref[...], preferred_element_type=F32) + bx_ref[...])
    log_a = (-LRU_C * r) * _softplus(-lam_ref[...])
    a = jnp.exp(log_a)
    u = jnp.sqrt(-jnp.tanh(log_a) * (a * a + 1.0)) * (i * xc)
    k = 1
    while k < n:
        u = a * _shift_rows(u, k, 0.0) + u
        a = a * _shift_rows(a, k, 1.0)
        k *= 2
    h = u + a * carry_ref[0:1, :]
    carry_ref[...] = jnp.broadcast_to(h[n - 1:n, :], carry_ref.shape)
    o_ref[...] = (_gelu_tanh(rg_ref[:, W_LRU:2 * W_LRU]) * h).astype(BF)


def _rg_lru(rg, seq, conv_w, conv_b, wa, ba, wx, bx, lam):
    t = rg.shape[0]
    nb = seq // ROW_TILE
    row = lambda b, j: (b * nb + j, 0)
    const = lambda b, j: (0, 0)
    vec = pl.BlockSpec((1, W_LRU), const)
    return pl.pallas_call(
        _rglru_kernel,
        out_shape=jax.ShapeDtypeStruct((t, W_LRU), BF),
        grid=(t // seq, nb),
        in_specs=[pl.BlockSpec((ROW_TILE, 2 * W_LRU), row),
                  pl.BlockSpec((CONV_K, W_LRU), const), vec,
                  pl.BlockSpec((W_LRU, W_LRU), const), vec,
                  pl.BlockSpec((W_LRU, W_LRU), const), vec, vec],
        out_specs=pl.BlockSpec((ROW_TILE, W_LRU), row),
        scratch_shapes=[pltpu.VMEM((HALO + ROW_TILE, W_LRU), F32), pltpu.VMEM((SUBLANES, W_LRU), F32)],
        compiler_params=_cparams(("parallel", "arbitrary")),
        name="rg_lru",
    )(rg, conv_w, conv_b.reshape(1, -1), wa, ba.reshape(1, -1), wx, bx.reshape(1, -1), lam.reshape(1, -1))


def _block_diag(w):
    nb, c, _ = w.shape
    eye = jnp.eye(nb, dtype=w.dtype)
    return (eye[:, None, :, None] * w[:, :, None, :]).reshape(nb * c, nb * c).astype(BF)


def _ssd_kernel(z_ref, xbc_ref, dt_ref, cw_ref, cb_ref, dtb_ref, alog_ref, dskip_ref, ng_ref, expand_ref,
                o_ref, halo_ref, state_ref):
    L = SSM_CHUNK

    @pl.when(pl.program_id(1) == 0)
    def _():
        halo_ref[pl.ds(0, HALO), :] = jnp.zeros((HALO, halo_ref.shape[1]), F32)
        state_ref[...] = jnp.zeros_like(state_ref)

    xbc = xbc_ref[...]
    conv = _silu(_causal_conv(xbc, halo_ref, cw_ref, cb_ref))
    xs = conv[:, 0:SSM_D_INNER]
    bm = conv[:, SSM_D_INNER:SSM_D_INNER + SSM_GROUPS * SSM_STATE]
    cm = conv[:, SSM_D_INNER + SSM_GROUPS * SSM_STATE:]

    dt = _softplus(dt_ref[...] + dtb_ref[...])
    da = dt * (-jnp.exp(alog_ref[...]))
    row = lax.broadcasted_iota(jnp.int32, (L, L), 0)
    col = lax.broadcasted_iota(jnp.int32, (L, L), 1)
    causal = col <= row
    a_cs = sum(jnp.dot(causal.astype(BF), part, preferred_element_type=F32) for part in _split3(da))
    a_cs_t = a_cs.T
    expand = expand_ref[...]
    dt_full = sum(jnp.dot(part, expand, preferred_element_type=F32) for part in _split3(dt))
    acs_full = sum(jnp.dot(part, expand, preferred_element_type=F32) for part in _split3(a_cs))
    alast_full = acs_full[L - 1:L, :]
    xdt = xs * dt_full
    xdt_b = xdt.astype(BF)
    xst_b = (xdt * jnp.exp(alast_full - acs_full)).astype(BF)
    left = lax.broadcasted_iota(jnp.int32, (L, LANES), 1) < SSM_HEAD_DIM

    y_parts = []
    for g in range(SSM_GROUPS):
        cg = cm[:, g * SSM_STATE:(g + 1) * SSM_STATE].astype(BF)
        bg = bm[:, g * SSM_STATE:(g + 1) * SSM_STATE].astype(BF)
        cb = lax.dot_general(cg, bg, (((1,), (1,)), ((), ())), preferred_element_type=F32)
        st = state_ref[:, g * SSD_GROUP_W:(g + 1) * SSD_GROUP_W]
        y_off = jnp.dot(cg, st.astype(BF), preferred_element_type=F32)
        for j in range(SSD_GROUP_W // LANES):
            h0 = g * (SSM_HEADS // SSM_GROUPS) + 2 * j
            ms = []
            for h in (h0, h0 + 1):
                seg = a_cs[:, h:h + 1] - a_cs_t[h:h + 1, :]
                ms.append((cb * jnp.exp(jnp.where(causal, seg, -jnp.inf))).astype(BF))
            c0 = h0 * SSM_HEAD_DIM
            x2 = xdt_b[:, c0:c0 + LANES]
            zero = jnp.zeros_like(x2)
            xblk = jnp.concatenate([jnp.where(left, x2, zero), jnp.where(left, zero, x2)], axis=0)
            y_parts.append(jnp.dot(jnp.concatenate(ms, axis=1), xblk, preferred_element_type=F32)
                           + y_off[:, j * LANES:(j + 1) * LANES] * jnp.exp(acs_full[:, c0:c0 + LANES]))
        new = lax.dot_general(bg, xst_b[:, g * SSD_GROUP_W:(g + 1) * SSD_GROUP_W], (((0,), (0,)), ((), ())),
                              preferred_element_type=F32)
        state_ref[:, g * SSD_GROUP_W:(g + 1) * SSD_GROUP_W] = (
            st * jnp.exp(alast_full[:, g * SSD_GROUP_W:(g + 1) * SSD_GROUP_W]) + new)
    y = jnp.concatenate(y_parts, axis=1) + dskip_ref[...] * xs
    y = y * _silu(z_ref[...])
    outs = []
    for g in range(SSM_GROUPS):
        yg = y[:, g * SSD_GROUP_W:(g + 1) * SSD_GROUP_W]
        outs.append(yg * lax.rsqrt(jnp.mean(yg * yg, axis=-1, keepdims=True) + LN_EPS))
    o_ref[...] = (jnp.concatenate(outs, axis=1) * ng_ref[...]).astype(BF)


def _ssd(z, xbc, dt, seq, conv_w, conv_b, dt_bias, a_log, d_skip, norm_g):
    t = z.shape[0]
    nc = seq // SSM_CHUNK
    row = lambda b, c: (b * nc + c, 0)
    const = lambda b, c: (0, 0)
    pad = LANES - SSM_HEADS
    expand = (jnp.arange(LANES)[:, None] == (jnp.arange(SSM_D_INNER)[None, :] // SSM_HEAD_DIM)).astype(BF)
    return pl.pallas_call(
        _ssd_kernel,
        out_shape=jax.ShapeDtypeStruct((t, SSM_D_INNER), BF),
        grid=(t // seq, nc),
        in_specs=[pl.BlockSpec((SSM_CHUNK, SSM_D_INNER), row),
                  pl.BlockSpec((SSM_CHUNK, SSM_CONV_CH), row),
                  pl.BlockSpec((SSM_CHUNK, LANES), row),
                  pl.BlockSpec((CONV_K, SSM_CONV_CH), const),
                  pl.BlockSpec((1, SSM_CONV_CH), const),
                  pl.BlockSpec((1, LANES), const),
                  pl.BlockSpec((1, LANES), const),
                  pl.BlockSpec((1, SSM_D_INNER), const),
                  pl.BlockSpec((1, SSM_D_INNER), const),
                  pl.BlockSpec((LANES, SSM_D_INNER), const)],
        out_specs=pl.BlockSpec((SSM_CHUNK, SSM_D_INNER), row),
        scratch_shapes=[pltpu.VMEM((HALO + SSM_CHUNK, SSM_CONV_CH), F32), pltpu.VMEM((SSM_STATE, SSM_D_INNER), F32)],
        compiler_params=_cparams(("parallel", "arbitrary")),
        name="ssd",
    )(z, xbc, dt, conv_w, conv_b.reshape(1, -1), jnp.pad(dt_bias, (0, pad)).reshape(1, -1),
      jnp.pad(a_log, (0, pad)).reshape(1, -1), jnp.repeat(d_skip, SSM_HEAD_DIM).reshape(1, -1),
      norm_g.reshape(1, -1), expand)


def _swa_kernel(sink_ref, q_ref, kv_ref, kvp_ref, o_ref):
    L = ATTN_BLOCK
    lane_k = lax.broadcasted_iota(jnp.int32, (2 * L, LANES), 1)
    qi = lax.broadcasted_iota(jnp.int32, (2 * L, 2 * L), 0) % L
    kj = lax.broadcasted_iota(jnp.int32, (2 * L, 2 * L), 1)
    rel = qi + L - kj
    band = (rel >= 0) & (rel < L)
    for blk in range(ATTN_STEP_BLOCKS):
        rows = pl.ds(blk * L, L)
        prev = kvp_ref[...] if blk == 0 else kv_ref[pl.ds((blk - 1) * L, L), :]
        kv = jnp.concatenate([prev, kv_ref[rows, :]], axis=0)
        valid = band & (jnp.logical_not(pl.program_id(1) == 0) | (kj >= L)) if blk == 0 else band
        _swa_block(sink_ref, q_ref, o_ref, rows, kv, valid, lane_k)


def _swa_block(sink_ref, q_ref, o_ref, rows, kv, valid, lane_k):
    L = ATTN_BLOCK
    hd = ATTN_HEAD_DIM
    kk = kv[:, 0:LANES]
    vv = kv[:, LANES:2 * LANES].astype(BF)
    top = lax.broadcasted_iota(jnp.int32, (2 * L, 1), 0) < L
    lane_o = lax.broadcasted_iota(jnp.int32, (L, LANES), 1)
    scale = hd ** -0.5
    for j in range(ATTN_Q_HEADS // 2):
        g = (2 * j) // ATTN_REP
        q2 = q_ref[rows, j * LANES:(j + 1) * LANES]
        q2r = pltpu.roll(q2, hd, axis=1)
        kg = jnp.where((lane_k >= g * hd) & (lane_k < (g + 1) * hd), kk, 0.0).astype(BF)
        qa, qb = (q2, q2r) if g == 0 else (q2r, q2)
        qs = jnp.concatenate([qa, qb], axis=0).astype(BF)
        logits = lax.dot_general(qs, kg, (((1,), (1,)), ((), ())), preferred_element_type=F32) * scale
        logits = jnp.where(valid, logits, -jnp.inf)
        sink = jnp.where(top, sink_ref[2 * j], sink_ref[2 * j + 1])
        m = jnp.maximum(jnp.max(logits, axis=-1, keepdims=True), sink)
        p = jnp.exp(logits - m)
        probs = p * (1.0 / (jnp.sum(p, axis=-1, keepdims=True) + jnp.exp(sink - m)))
        o = jnp.dot(probs.astype(BF), vv, preferred_element_type=F32)
        oa, ob = o[0:L], o[L:2 * L]
        if g == 0:
            out2 = jnp.where(lane_o < hd, oa, pltpu.roll(ob, hd, axis=1))
        else:
            out2 = jnp.where(lane_o < hd, pltpu.roll(oa, hd, axis=1), ob)
        o_ref[rows, j * LANES:(j + 1) * LANES] = out2


def _swa(q, kv, seq, sinks):
    t = q.shape[0]
    sb = ATTN_STEP_BLOCKS
    rows = sb * ATTN_BLOCK
    ns = seq // rows
    tile = lambda b, n: (b * ns + n, 0)
    before = lambda b, n: (b * ns * sb + jnp.maximum(n * sb - 1, 0), 0)
    return pl.pallas_call(
        _swa_kernel,
        out_shape=jax.ShapeDtypeStruct((t, ATTN_Q_W), F32),
        grid=(t // seq, ns),
        in_specs=[pl.BlockSpec(memory_space=pltpu.SMEM),
                  pl.BlockSpec((rows, ATTN_Q_W), tile),
                  pl.BlockSpec((rows, ATTN_KV_W), tile),
                  pl.BlockSpec((ATTN_BLOCK, ATTN_KV_W), before)],
        out_specs=pl.BlockSpec((rows, ATTN_Q_W), tile),
        compiler_params=_cparams(("parallel", "parallel")),
        name="swa",
    )(sinks, q, kv, kv)


def _outproj_kernel(ya_ref, yb_ref, yc_ref, h_ref, wa_ref, wb_ref, wc_ref, g_ref, b_ref, o_ref):
    mix = (jnp.dot(ya_ref[...], wa_ref[...], preferred_element_type=F32)
           + jnp.dot(yb_ref[...], wb_ref[...], preferred_element_type=F32)
           + jnp.dot(yc_ref[...].astype(BF), wc_ref[...], preferred_element_type=F32))
    o_ref[...] = _ln_math(DN_ALPHA * h_ref[...] + mix, g_ref[...], b_ref[...])


def _out_proj(ya, yb, yc, h, w_out, g, b):
    t = h.shape[0]
    w = w_out.astype(BF)
    row = lambda i: (i, 0)
    const = lambda i: (0, 0)
    c1, c2 = W_LRU, W_LRU + SSM_D_INNER
    return pl.pallas_call(
        _outproj_kernel,
        out_shape=jax.ShapeDtypeStruct((t, D_MODEL), F32),
        grid=(t // ROW_TILE,),
        in_specs=[pl.BlockSpec((ROW_TILE, W_LRU), row), pl.BlockSpec((ROW_TILE, SSM_D_INNER), row),
                  pl.BlockSpec((ROW_TILE, ATTN_Q_W), row), pl.BlockSpec((ROW_TILE, D_MODEL), row),
                  pl.BlockSpec((W_LRU, D_MODEL), const), pl.BlockSpec((SSM_D_INNER, D_MODEL), const),
                  pl.BlockSpec((ATTN_Q_W, D_MODEL), const),
                  pl.BlockSpec((1, D_MODEL), const), pl.BlockSpec((1, D_MODEL), const)],
        out_specs=pl.BlockSpec((ROW_TILE, D_MODEL), row),
        compiler_params=_cparams(("parallel",)),
        name="out_proj",
    )(ya, yb, yc, h, w[0:c1], w[c1:c2], w[c2:MIX_WIDTH], g.reshape(1, -1), b.reshape(1, -1))


def _gather_group(idx_ref, tbl_ref, t, g):
    rs = ROW_WORDS_SUBLANES
    pieces = []
    for q in range(GROUP_PAIRS // 2):
        p = g * GROUP_PAIRS + 2 * q
        if p % SUBLANES == 0:
            window = idx_ref.at[t, pl.ds(p, SUBLANES)]
        ra = pl.multiple_of(window[p % SUBLANES], rs)
        rb = pl.multiple_of(window[p % SUBLANES + 1], rs)
        pieces.append(jnp.concatenate([tbl_ref[pl.ds(ra, rs), :], tbl_ref[pl.ds(rb, rs), :]], axis=0))
    return pltpu.bitcast(jnp.concatenate(pieces, axis=0), BF)


def _chunk_of_column(col):
    j = col % PAIR_LANES
    return j // 2 + ROW_WORDS_SUBLANES * (j % 2)


def _split_hi_lo(x):
    hi = x.astype(BF)
    lo = (x - hi.astype(F32)).astype(BF)
    return hi, lo


def _peer_in_kernel(idx_ref, x_ref, gate_ref, tbl_ref, o_ref, m_ref):
    n = x_ref.shape[0]
    row = lax.broadcasted_iota(jnp.int32, (2 * SUBLANES, GROUP_K), 0) % SUBLANES
    keep = row == _chunk_of_column(lax.broadcasted_iota(jnp.int32, (2 * SUBLANES, GROUP_K), 1))
    for t in range(n):
        x_row = x_ref[pl.ds(t, 1), :]
        x8 = jnp.concatenate([x_row[:, c * LANES:(c + 1) * LANES] for c in range(TOKEN_SUBLANES)], axis=0)
        x_hi, x_lo = _split_hi_lo(x8)
        x16 = jnp.concatenate([x_hi, x_lo], axis=0)
        for g in range(PEER_PAIRS // GROUP_PAIRS):
            r = lax.dot_general(x16, _gather_group(idx_ref, tbl_ref, t, g), (((1,), (1,)), ((), ())),
                                preferred_element_type=F32)
            m_ref[pl.ds(t, 1), g * GROUP_K:(g + 1) * GROUP_K] = jnp.sum(jnp.where(keep, r, 0.0), axis=0,
                                                                        keepdims=True)
    wide = PEER_PAIRS * PAIR_LANES
    fold = (lax.broadcasted_iota(jnp.int32, (wide, PEER_PAIRS), 0) // PAIR_LANES
            == lax.broadcasted_iota(jnp.int32, (wide, PEER_PAIRS), 1)).astype(BF)
    m_hi, m_lo = _split_hi_lo(m_ref[...])
    a = jnp.dot(m_hi, fold, preferred_element_type=F32) + jnp.dot(m_lo, fold, preferred_element_type=F32)
    o_ref[...] = gate_ref[...] * (0.5 * a * (1.0 + lax.erf(a * (2.0 ** -0.5))))


def _peer_in(rows, x, gate, tbl):
    t = x.shape[0]
    tt = GATHER_TOKENS
    pairs = pl.BlockSpec((tt, PEER_PAIRS), lambda i: (i, 0))
    return pl.pallas_call(
        _peer_in_kernel,
        out_shape=jax.ShapeDtypeStruct((t, PEER_PAIRS), F32),
        grid=(t // tt,),
        in_specs=[
            pl.BlockSpec((tt, PEER_PAIRS), lambda i: (i, 0), memory_space=pltpu.SMEM),
            pl.BlockSpec((tt, D_MODEL), lambda i: (i, 0)),
            pairs,
            pl.BlockSpec(memory_space=pltpu.VMEM),
        ],
        out_specs=pairs,
        scratch_shapes=[pltpu.VMEM((tt, PEER_PAIRS * PAIR_LANES), F32)],
        compiler_params=pltpu.CompilerParams(dimension_semantics=("parallel",), vmem_limit_bytes=PEER_VMEM_LIMIT),
        name="peer_in",
    )(rows, x, gate, tbl)


def _peer_out_kernel(idx_ref, w_ref, tbl_ref, h_ref, g_ref, b_ref, o_ref, wide_ref, ffn_ref):
    n = w_ref.shape[0]
    wide = PEER_PAIRS * PAIR_LANES
    rep = (lax.broadcasted_iota(jnp.int32, (PEER_PAIRS, wide), 1) // PAIR_LANES
           == lax.broadcasted_iota(jnp.int32, (PEER_PAIRS, wide), 0)).astype(BF)
    for k, wv in enumerate(_split_hi_lo(w_ref[...])):
        wide_ref[k] = jnp.dot(wv, rep, preferred_element_type=F32)
    keep = (lax.broadcasted_iota(jnp.int32, (SUBLANES, GROUP_K), 0)
            == _chunk_of_column(lax.broadcasted_iota(jnp.int32, (SUBLANES, GROUP_K), 1)))
    for t in range(n):
        acc = jnp.zeros((2 * SUBLANES, LANES), F32)
        for g in range(PEER_PAIRS // GROUP_PAIRS):
            cols = slice(g * GROUP_K, (g + 1) * GROUP_K)
            lhs = jnp.concatenate(
                [jnp.where(keep, jnp.broadcast_to(wide_ref[k, pl.ds(t, 1), cols], (SUBLANES, GROUP_K)), 0.0)
                 for k in range(2)], axis=0).astype(BF)
            acc = acc + jnp.dot(lhs, _gather_group(idx_ref, tbl_ref, t, g), preferred_element_type=F32)
        out8 = acc[0:SUBLANES] + acc[SUBLANES:]
        for c in range(TOKEN_SUBLANES):
            ffn_ref[pl.ds(t, 1), c * LANES:(c + 1) * LANES] = out8[c:c + 1, :]
    o_ref[...] = _ln_math(DN_ALPHA * h_ref[...] + ffn_ref[...], g_ref[...], b_ref[...])


def _peer_out(rows, w, tbl, h, g, b):
    t = h.shape[0]
    tt = GATHER_TOKENS
    row = pl.BlockSpec((tt, D_MODEL), lambda i: (i, 0))
    vec = pl.BlockSpec((1, D_MODEL), lambda i: (0, 0))
    return pl.pallas_call(
        _peer_out_kernel,
        out_shape=jax.ShapeDtypeStruct((t, D_MODEL), F32),
        grid=(t // tt,),
        in_specs=[
            pl.BlockSpec((tt, PEER_PAIRS), lambda i: (i, 0), memory_space=pltpu.SMEM),
            pl.BlockSpec((tt, PEER_PAIRS), lambda i: (i, 0)),
            pl.BlockSpec(memory_space=pltpu.VMEM),
            row, vec, vec,
        ],
        out_specs=row,
        scratch_shapes=[pltpu.VMEM((2, tt, PEER_PAIRS * PAIR_LANES), F32), pltpu.VMEM((tt, D_MODEL), F32)],
        compiler_params=pltpu.CompilerParams(vmem_limit_bytes=PEER_VMEM_LIMIT),
        name="peer_out",
    )(rows, w, tbl, h, g.reshape(1, D_MODEL), b.reshape(1, D_MODEL))


def _extract_topk(s, pos, k, payload=None):
    big = jnp.asarray(2 ** 30, pos.dtype)
    vals, poss, pays = [], [], []
    for _ in range(k):
        m = jnp.max(s, axis=0, keepdims=True)
        j = jnp.min(jnp.where(s == m, pos, big), axis=0, keepdims=True)
        sel = pos == j
        vals.append(m)
        poss.append(j)
        if payload is not None:
            pays.append(jnp.sum(jnp.where(sel, payload, jnp.zeros_like(payload)), axis=0, keepdims=True))
        s = jnp.where(sel, -jnp.inf, s)
    out = [jnp.concatenate(vals, axis=0), jnp.concatenate(poss, axis=0)]
    if payload is not None:
        out.append(jnp.concatenate(pays, axis=0))
    return out


def _topk_two_per_slot(s, k):
    m, n = s.shape[0] // 2, s.shape[1]
    big = jnp.asarray(2 ** 30, F32)
    pos_a = lax.broadcasted_iota(jnp.int32, (m, n), 0).astype(F32)
    a, b = s[:m], s[m:]
    b_wins = b > a
    win, wait = jnp.where(b_wins, b, a), jnp.where(b_wins, a, b)
    win_pos, wait_pos = jnp.where(b_wins, pos_a + m, pos_a), jnp.where(b_wins, pos_a, pos_a + m)
    vals, poss = [], []
    for _ in range(k):
        top = jnp.max(win, axis=0, keepdims=True)
        j = jnp.min(jnp.where(win == top, win_pos, big), axis=0, keepdims=True)
        sel = win_pos == j
        vals.append(top)
        poss.append(j)
        win = jnp.where(sel, wait, win)
        win_pos = jnp.where(sel, wait_pos, win_pos)
        wait = jnp.where(sel, -jnp.inf, wait)
    return jnp.concatenate(vals, axis=0), jnp.concatenate(poss, axis=0)


def _peer_route_kernel(h_ref, wqt_ref, keys_ref, idx_ref, gate_ref):
    half_dim = PEER_DKEY // 2
    hb = h_ref[...].astype(BF)
    n = hb.shape[0]
    iota16 = lax.broadcasted_iota(jnp.int32, (PEER_TOPK, n), 0).astype(F32)
    iota8 = lax.broadcasted_iota(jnp.int32, (SUBLANES, n), 0).astype(F32)
    qt = lax.dot_general(wqt_ref[...], hb, (((1,), (1,)), ((), ())), preferred_element_type=F32)
    idx_rows, gate_rows = [], []
    for hd in range(PEER_HEADS):
        tv, ti = [], []
        for half in range(2):
            r0 = hd * PEER_DKEY + half * half_dim
            s = jnp.dot(keys_ref[half], qt[r0:r0 + half_dim, :], preferred_element_type=F32)
            v, i = _topk_two_per_slot(s, PEER_TOPK)
            tv.append(v)
            ti.append(i)
        v1_lo, i1_lo = tv[1][0:SUBLANES, :], ti[1][0:SUBLANES, :]

        def block(a, first_row):
            shift = (lambda x: x) if first_row == 0 else (lambda x: pltpu.roll(x, first_row, axis=0))
            return (tv[0][a:a + 1, :] + shift(v1_lo), iota8 + (a * PEER_TOPK - first_row),
                    ti[0][a:a + 1, :] * PEER_NKEYS + shift(i1_lo))

        def pick(cond, x, y):
            return tuple(jnp.where(cond, xi, yi) for xi, yi in zip(x, y))

        padding = (jnp.full_like(v1_lo, -jnp.inf), iota8 + 2 ** 20, jnp.zeros_like(i1_lo))
        tiles = [block(1, 0),
                 pick(iota8 < 5, block(2, 0), block(4, 5)),
                 pick(iota8 < 4, block(3, 0), pick(iota8 < 6, block(5, 4), block(6, 6))),
                 pick(iota8 < 2, block(7, 0), padding),
                 (tv[0][SUBLANES:, :] + tv[1][0:1, :], (iota8 + SUBLANES) * PEER_TOPK,
                  ti[0][SUBLANES:, :] * PEER_NKEYS + ti[1][0:1, :])]
        cs = [tv[0][0:1, :] + tv[1]] + [tile[0] for tile in tiles]
        cp = [iota16] + [tile[1] for tile in tiles]
        ce = [ti[0][0:1, :] * PEER_NKEYS + ti[1]] + [tile[2] for tile in tiles]
        best_s, _, best_e = _extract_topk(jnp.concatenate(cs, axis=0), jnp.concatenate(cp, axis=0), PEER_TOPK,
                                          payload=jnp.concatenate(ce, axis=0))
        ex = jnp.exp(best_s - best_s[0:1, :])
        gate_rows.append(ex / jnp.sum(ex, axis=0, keepdims=True))
        idx_rows.append(best_e)
    idx_ref[...] = (jnp.concatenate(idx_rows, axis=0) * ROW_WORDS_SUBLANES).astype(jnp.int32).T
    gate_ref[...] = jnp.concatenate(gate_rows, axis=0).T


def _peer_route(h, wqt, keys):
    t = h.shape[0]
    rt = ROUTE_TOKENS
    pairs = pl.BlockSpec((rt, PEER_PAIRS), lambda i: (i, 0))
    return pl.pallas_call(
        _peer_route_kernel,
        out_shape=(jax.ShapeDtypeStruct((t, PEER_PAIRS), jnp.int32), jax.ShapeDtypeStruct((t, PEER_PAIRS), F32)),
        grid=(t // rt,),
        in_specs=[
            pl.BlockSpec((rt, D_MODEL), lambda i: (i, 0)),
            pl.BlockSpec((D_MODEL, D_MODEL), lambda i: (0, 0)),
            pl.BlockSpec((2, PEER_NKEYS, PEER_DKEY // 2), lambda i: (0, 0, 0)),
        ],
        out_specs=(pairs, pairs),
        compiler_params=_cparams(("parallel",)),
        name="peer_route",
    )(h, wqt, keys)


def _pack_kernel(u_ref, o_ref):
    half = D_MODEL // 2
    n = u_ref.shape[0]
    rs = ROW_WORDS_SUBLANES
    lo = lax.bitcast_convert_type(u_ref[:, 0:half].astype(BF).astype(F32), jnp.uint32) >> 16
    hi = lax.bitcast_convert_type(u_ref[:, half:D_MODEL].astype(BF).astype(F32), jnp.uint32) & jnp.uint32(0xFFFF0000)
    w = lo | hi
    for s in range(rs):
        o_ref[pl.ds(s, n, stride=rs), :] = w[:, s * LANES:(s + 1) * LANES]


def _pack_table(u):
    e = u.shape[0]
    return pl.pallas_call(
        _pack_kernel,
        out_shape=jax.ShapeDtypeStruct((e * ROW_WORDS_SUBLANES, LANES), jnp.uint32),
        grid=(e // ROW_TILE,),
        in_specs=[pl.BlockSpec((ROW_TILE, D_MODEL), lambda i: (i, 0))],
        out_specs=pl.BlockSpec((ROW_TILE * ROW_WORDS_SUBLANES, LANES), lambda i: (i, 0)),
        compiler_params=_cparams(("parallel",)),
        name="pack_table",
    )(u)


def _peer_sublayer(h, wq, keys, u, v, ln_g, ln_b):
    rows, gate = _peer_route(h, wq.T.astype(BF), keys)
    w = _peer_in(rows, h, gate, _pack_table(u))
    return _peer_out(rows, w, _pack_table(v), h, ln_g, ln_b)


def kernel(x, emb_ln_g, emb_ln_b, w_in, rg_conv_w, rg_conv_b, rg_wa, rg_ba, rg_wx, rg_bx, rg_lambda, ssm_conv_w, ssm_conv_b, ssm_dt_bias, ssm_a_log, ssm_d, ssm_norm_g, attn_sinks, w_out, ln1_g, ln1_b, peer_wq, peer_keys, peer_u, peer_v, ln2_g, ln2_b):
    bsz, seq, d = x.shape
    assert d == D_MODEL and seq % ROW_TILE == 0 and (bsz * seq) % ROUTE_TOKENS == 0
    assert rg_wa.shape[1:] == (LRU_BLOCKS, W_LRU // LRU_BLOCKS, W_LRU // LRU_BLOCKS)
    h = x.reshape(bsz * seq, d)
    for l in range(DEPTH):
        h, (rg, z, xbc, dt, q, kv) = _in_proj(h, _split_w_in(w_in[l]), (emb_ln_g, emb_ln_b) if l == 0 else None)
        y_a = _rg_lru(rg, seq, rg_conv_w[l], rg_conv_b[l], _block_diag(rg_wa[l]), rg_ba[l],
                      _block_diag(rg_wx[l]), rg_bx[l], rg_lambda[l])
        y_b = _ssd(z, xbc, dt, seq, ssm_conv_w[l], ssm_conv_b[l], ssm_dt_bias[l], ssm_a_log[l], ssm_d[l],
                   ssm_norm_g[l])
        y_c = _swa(q, kv, seq, attn_sinks[l])
        h = _out_proj(y_a, y_b, y_c, h, w_out[l], ln1_g[l], ln1_b[l])
        h = _peer_sublayer(h, peer_wq[l], peer_keys[l], peer_u[l], peer_v[l], ln2_g[l], ln2_b[l])
    return h.reshape(bsz, seq, d)
```

```python
import functools
import math

import jax
import jax.numpy as jnp
from jax import lax
from jax.experimental import pallas as pl
from jax.experimental.pallas import tpu as pltpu

D_MODEL = 1024
DEPTH = 2

W_LRU = D_MODEL // 2
LRU_BLOCKS = 8
LRU_C = 8.0
CONV_K = 4

SSM_HEAD_DIM = 64
SSM_D_INNER = D_MODEL
SSM_HEADS = SSM_D_INNER // SSM_HEAD_DIM
SSM_GROUPS = 2
SSM_STATE = 128
SSM_CHUNK = 128
SSM_STEP_CHUNKS = 2
SSM_CONV_CH = SSM_D_INNER + 2 * SSM_GROUPS * SSM_STATE
SSD_GROUP_W = SSM_D_INNER // SSM_GROUPS

ATTN_HEAD_DIM = 64
ATTN_Q_HEADS = (D_MODEL // 2) // ATTN_HEAD_DIM
ATTN_KV_HEADS = 2
ATTN_REP = ATTN_Q_HEADS // ATTN_KV_HEADS
ATTN_BLOCK = 128
ATTN_STEP_BLOCKS = 2
ATTN_Q_W = ATTN_Q_HEADS * ATTN_HEAD_DIM
ATTN_KV_W = 2 * ATTN_KV_HEADS * ATTN_HEAD_DIM

MIX_WIDTH = W_LRU + SSM_D_INNER + ATTN_Q_W

PEER_HEADS = 8
PEER_NKEYS = 128
PEER_DKEY = 128
PEER_TOPK = 16
PEER_PAIRS = PEER_HEADS * PEER_TOPK

DN_ALPHA = (2 * DEPTH) ** 0.25
LN_EPS = 1e-5

SUBLANES = 8
LANES = 128
HALO = SUBLANES
ROW_WORDS_SUBLANES = D_MODEL // 2 // LANES
TOKEN_SUBLANES = D_MODEL // LANES

ROW_TILE = 512
PAIR_LANES = 2 * ROW_WORDS_SUBLANES
GROUP_PAIRS = 32
GROUP_K = GROUP_PAIRS * PAIR_LANES
GATHER_TOKENS = 128
ROUTE_TOKENS = 256
VMEM_LIMIT = 48 * 1024 * 1024
PEER_VMEM_LIMIT = 56 * 1024 * 1024

BF = jnp.bfloat16
F32 = jnp.float32

PROJ_SEGS = (2 * W_LRU, SSM_D_INNER, SSM_CONV_CH, LANES, ATTN_Q_W, ATTN_KV_W)


def _cparams(sem):
    return pltpu.CompilerParams(dimension_semantics=sem, vmem_limit_bytes=VMEM_LIMIT)


def _softplus(x):
    return jnp.maximum(x, 0.0) + jnp.log1p(jnp.exp(-jnp.abs(x)))


def _sigmoid(x):
    return 1.0 / (1.0 + jnp.exp(-x))


def _silu(x):
    return x * _sigmoid(x)


def _split3(x):
    hi = x.astype(BF)
    rest = x - hi.astype(F32)
    mid = rest.astype(BF)
    return hi, mid, (rest - mid.astype(F32)).astype(BF)


def _gelu_tanh(x):
    return 0.5 * x * (1.0 + jnp.tanh(math.sqrt(2.0 / math.pi) * (x + 0.044715 * (x * x * x))))


def _ln_math(x, g, b):
    mu = jnp.mean(x, axis=-1, keepdims=True)
    xc = x - mu
    var = jnp.mean(xc * xc, axis=-1, keepdims=True)
    return xc * lax.rsqrt(var + LN_EPS) * g + b


def _inproj_kernel(*refs, entry_norm):
    n = len(PROJ_SEGS)
    if entry_norm:
        x_ref, g_ref, b_ref = refs[:3]
        w_refs, h_out_ref, o_refs = refs[3:3 + n], refs[3 + n], refs[4 + n:]
        h = _ln_math(x_ref[...], g_ref[...], b_ref[...])
        h_out_ref[...] = h
    else:
        w_refs, o_refs = refs[1:1 + n], refs[1 + n:]
        h = refs[0][...]
    hb = h.astype(BF)
    for w_ref, o_ref in zip(w_refs, o_refs):
        o_ref[...] = jnp.dot(hb, w_ref[...], preferred_element_type=F32)


def _in_proj(h, ws, entry_ln=None):
    t = h.shape[0]
    row = lambda w: pl.BlockSpec((ROW_TILE, w), lambda i: (i, 0))
    vec = pl.BlockSpec((1, D_MODEL), lambda i: (0, 0))
    norm = entry_ln is not None
    outs = pl.pallas_call(
        functools.partial(_inproj_kernel, entry_norm=norm),
        out_shape=tuple(jax.ShapeDtypeStruct((t, w), F32) for w in ((D_MODEL,) if norm else ()) + PROJ_SEGS),
        grid=(t // ROW_TILE,),
        in_specs=[row(D_MODEL)] + ([vec, vec] if norm else [])
        + [pl.BlockSpec((D_MODEL, w), lambda i: (0, 0)) for w in PROJ_SEGS],
        out_specs=tuple(row(w) for w in ((D_MODEL,) if norm else ()) + PROJ_SEGS),
        compiler_params=_cparams(("parallel",)),
        name="in_proj",
    )(h, *([v.reshape(1, D_MODEL) for v in entry_ln] if norm else []), *ws)
    return (outs[0], outs[1:]) if norm else (h, outs)


def _split_w_in(w_in):
    w = w_in.astype(BF)
    c = [0]
    for width in (2 * W_LRU, SSM_D_INNER, SSM_CONV_CH, SSM_HEADS, ATTN_Q_W, ATTN_KV_W):
        c.append(c[-1] + width)
    segs = [w[:, c[i]:c[i + 1]] for i in range(6)]
    segs[3] = jnp.pad(segs[3], ((0, 0), (0, LANES - SSM_HEADS)))
    return tuple(segs)


def _shift_rows(x, s, fill):
    n = x.shape[0]
    if s % SUBLANES == 0:
        return jnp.concatenate([jnp.full((s,) + x.shape[1:], fill, x.dtype), x[:n - s]], axis=0)
    rolled = pltpu.roll(x, s, axis=0)
    row = lax.broadcasted_iota(jnp.int32, x.shape, 0)
    return jnp.where(row < s, fill, rolled)


def _causal_conv(x, halo_ref, w_ref, b_ref):
    n = x.shape[0]
    halo_ref[pl.ds(HALO, n), :] = x
    out = b_ref[...] + w_ref[CONV_K - 1:CONV_K, :] * x
    for s in range(1, CONV_K):
        out = out + w_ref[CONV_K - 1 - s:CONV_K - s, :] * halo_ref[pl.ds(HALO - s, n), :]
    halo_ref[pl.ds(0, HALO), :] = x[n - HALO:n]
    return out


def _rglru_kernel(rg_ref, cw_ref, cb_ref, wa_ref, ba_ref, wx_ref, bx_ref, lam_ref, o_ref, halo_ref, carry_ref):
    n = rg_ref.shape[0]

    @pl.when(pl.program_id(1) == 0)
    def _():
        halo_ref[pl.ds(0, HALO), :] = jnp.zeros((HALO, halo_ref.shape[1]), F32)
        carry_ref[...] = jnp.zeros_like(carry_ref)

    x = rg_ref[:, 0:W_LRU]
    xc = _causal_conv(x, halo_ref, cw_ref, cb_ref)
    xb = xc.astype(BF)
    r = _sigmoid(jnp.dot(xb, wa_ref[...], preferred_element_type=F32) + ba_ref[...])
    i = _sigmoid(jnp.dot(xb, wx_ref[...], preferred_element_type=F32) + bx_ref[...])
    log_a = (-LRU_C * r) * _softplus(-lam_ref[...])
    a = jnp.exp(log_a)
    u = jnp.sqrt(-jnp.tanh(log_a) * (a * a + 1.0)) * (i * xc)
    k = 1
    while k < n:
        u = a * _shift_rows(u, k, 0.0) + u
        a = a * _shift_rows(a, k, 1.0)
        k *= 2
    h = u + a * carry_ref[0:1, :]
    carry_ref[...] = jnp.broadcast_to(h[n - 1:n, :], carry_ref.shape)
    o_ref[...] = (_gelu_tanh(rg_ref[:, W_LRU:2 * W_LRU]) * h).astype(BF)


def _rg_lru(rg, seq, conv_w, conv_b, wa, ba, wx, bx, lam):
    t = rg.shape[0]
    nb = seq // ROW_TILE
    row = lambda b, j: (b * nb + j, 0)
    const = lambda b, j: (0, 0)
    vec = pl.BlockSpec((1, W_LRU), const)
    return pl.pallas_call(
        _rglru_kernel,
        out_shape=jax.ShapeDtypeStruct((t, W_LRU), BF),
        grid=(t // seq, nb),
        in_specs=[pl.BlockSpec((ROW_TILE, 2 * W_LRU), row),
                  pl.BlockSpec((CONV_K, W_LRU), const), vec,
                  pl.BlockSpec((W_LRU, W_LRU), const), vec,
                  pl.BlockSpec((W_LRU, W_LRU), const), vec, vec],
        out_specs=pl.BlockSpec((ROW_TILE, W_LRU), row),
        scratch_shapes=[pltpu.VMEM((HALO + ROW_TILE, W_LRU), F32), pltpu.VMEM((SUBLANES, W_LRU), F32)],
        compiler_params=_cparams(("parallel", "arbitrary")),
        name="rg_lru",
    )(rg, conv_w, conv_b.reshape(1, -1), wa, ba.reshape(1, -1), wx, bx.reshape(1, -1), lam.reshape(1, -1))


def _block_diag(w):
    nb, c, _ = w.shape
    eye = jnp.eye(nb, dtype=w.dtype)
    return (eye[:, None, :, None] * w[:, :, None, :]).reshape(nb * c, nb * c).astype(BF)


def _ssd_kernel(z_ref, xbc_ref, dt_ref, cw_ref, cb_ref, dtb_ref, alog_ref, dskip_ref, ng_ref, expand_ref,
                o_ref, halo_ref, state_ref):
    @pl.when(pl.program_id(1) == 0)
    def _():
        halo_ref[pl.ds(0, HALO), :] = jnp.zeros((HALO, halo_ref.shape[1]), F32)
        state_ref[...] = jnp.zeros_like(state_ref)

    for ck in range(SSM_STEP_CHUNKS):
        _ssd_chunk(z_ref, xbc_ref, dt_ref, cw_ref, cb_ref, dtb_ref, alog_ref, dskip_ref, ng_ref, expand_ref,
                   o_ref, halo_ref, state_ref, pl.ds(ck * SSM_CHUNK, SSM_CHUNK))


def _ssd_chunk(z_ref, xbc_ref, dt_ref, cw_ref, cb_ref, dtb_ref, alog_ref, dskip_ref, ng_ref, expand_ref,
               o_ref, halo_ref, state_ref, rows):
    L = SSM_CHUNK
    xbc = xbc_ref[rows, :]
    conv = _silu(_causal_conv(xbc, halo_ref, cw_ref, cb_ref))
    xs = conv[:, 0:SSM_D_INNER]
    bm = conv[:, SSM_D_INNER:SSM_D_INNER + SSM_GROUPS * SSM_STATE]
    cm = conv[:, SSM_D_INNER + SSM_GROUPS * SSM_STATE:]

    dt = _softplus(dt_ref[rows, :] + dtb_ref[...])
    da = dt * (-jnp.exp(alog_ref[...]))
    row = lax.broadcasted_iota(jnp.int32, (L, L), 0)
    col = lax.broadcasted_iota(jnp.int32, (L, L), 1)
    causal = col <= row
    a_cs = sum(jnp.dot(causal.astype(BF), part, preferred_element_type=F32) for part in _split3(da))
    a_cs_t = a_cs.T
    expand = expand_ref[...]
    dt_full = sum(jnp.dot(part, expand, preferred_element_type=F32) for part in _split3(dt))
    acs_full = sum(jnp.dot(part, expand, preferred_element_type=F32) for part in _split3(a_cs))
    alast_full = acs_full[L - 1:L, :]
    xdt = xs * dt_full
    xdt_b = xdt.astype(BF)
    xst_b = (xdt * jnp.exp(alast_full - acs_full)).astype(BF)
    left = lax.broadcasted_iota(jnp.int32, (L, LANES), 1) < SSM_HEAD_DIM

    y_parts = []
    for g in range(SSM_GROUPS):
        cg = cm[:, g * SSM_STATE:(g + 1) * SSM_STATE].astype(BF)
        bg = bm[:, g * SSM_STATE:(g + 1) * SSM_STATE].astype(BF)
        cb = lax.dot_general(cg, bg, (((1,), (1,)), ((), ())), preferred_element_type=F32)
        st = state_ref[:, g * SSD_GROUP_W:(g + 1) * SSD_GROUP_W]
        y_off = jnp.dot(cg, st.astype(BF), preferred_element_type=F32)
        for j in range(SSD_GROUP_W // LANES):
            h0 = g * (SSM_HEADS // SSM_GROUPS) + 2 * j
            ms = []
            for h in (h0, h0 + 1):
                seg = a_cs[:, h:h + 1] - a_cs_t[h:h + 1, :]
                ms.append((cb * jnp.exp(jnp.where(causal, seg, -jnp.inf))).astype(BF))
            c0 = h0 * SSM_HEAD_DIM
            x2 = xdt_b[:, c0:c0 + LANES]
            zero = jnp.zeros_like(x2)
            xblk = jnp.concatenate([jnp.where(left, x2, zero), jnp.where(left, zero, x2)], axis=0)
            y_parts.append(jnp.dot(jnp.concatenate(ms, axis=1), xblk, preferred_element_type=F32)
                           + y_off[:, j * LANES:(j + 1) * LANES] * jnp.exp(acs_full[:, c0:c0 + LANES]))
        new = lax.dot_general(bg, xst_b[:, g * SSD_GROUP_W:(g + 1) * SSD_GROUP_W], (((0,), (0,)), ((), ())),
                              preferred_element_type=F32)
        state_ref[:, g * SSD_GROUP_W:(g + 1) * SSD_GROUP_W] = (
            st * jnp.exp(alast_full[:, g * SSD_GROUP_W:(g + 1) * SSD_GROUP_W]) + new)
    y = jnp.concatenate(y_parts, axis=1) + dskip_ref[...] * xs
    y = y * _silu(z_ref[rows, :])
    outs = []
    for g in range(SSM_GROUPS):
        yg = y[:, g * SSD_GROUP_W:(g + 1) * SSD_GROUP_W]
        outs.append(yg * lax.rsqrt(jnp.mean(yg * yg, axis=-1, keepdims=True) + LN_EPS))
    o_ref[rows, :] = (jnp.concatenate(outs, axis=1) * ng_ref[...]).astype(BF)


def _ssd(z, xbc, dt, seq, conv_w, conv_b, dt_bias, a_log, d_skip, norm_g):
    t = z.shape[0]
    step = SSM_STEP_CHUNKS * SSM_CHUNK
    nc = seq // step
    row = lambda b, c: (b * nc + c, 0)
    const = lambda b, c: (0, 0)
    pad = LANES - SSM_HEADS
    expand = (jnp.arange(LANES)[:, None] == (jnp.arange(SSM_D_INNER)[None, :] // SSM_HEAD_DIM)).astype(BF)
    return pl.pallas_call(
        _ssd_kernel,
        out_shape=jax.ShapeDtypeStruct((t, SSM_D_INNER), BF),
        grid=(t // seq, nc),
        in_specs=[pl.BlockSpec((step, SSM_D_INNER), row),
                  pl.BlockSpec((step, SSM_CONV_CH), row),
                  pl.BlockSpec((step, LANES), row),
                  pl.BlockSpec((CONV_K, SSM_CONV_CH), const),
                  pl.BlockSpec((1, SSM_CONV_CH), const),
                  pl.BlockSpec((1, LANES), const),
                  pl.BlockSpec((1, LANES), const),
                  pl.BlockSpec((1, SSM_D_INNER), const),
                  pl.BlockSpec((1, SSM_D_INNER), const),
                  pl.BlockSpec((LANES, SSM_D_INNER), const)],
        out_specs=pl.BlockSpec((step, SSM_D_INNER), row),
        scratch_shapes=[pltpu.VMEM((HALO + SSM_CHUNK, SSM_CONV_CH), F32), pltpu.VMEM((SSM_STATE, SSM_D_INNER), F32)],
        compiler_params=_cparams(("parallel", "arbitrary")),
        name="ssd",
    )(z, xbc, dt, conv_w, conv_b.reshape(1, -1), jnp.pad(dt_bias, (0, pad)).reshape(1, -1),
      jnp.pad(a_log, (0, pad)).reshape(1, -1), jnp.repeat(d_skip, SSM_HEAD_DIM).reshape(1, -1),
      norm_g.reshape(1, -1), expand)


def _swa_kernel(sink_ref, q_ref, kv_ref, kvp_ref, o_ref):
    L = ATTN_BLOCK
    lane_k = lax.broadcasted_iota(jnp.int32, (2 * L, LANES), 1)
    qi = lax.broadcasted_iota(jnp.int32, (2 * L, 2 * L), 0) % L
    kj = lax.broadcasted_iota(jnp.int32, (2 * L, 2 * L), 1)
    rel = qi + L - kj
    band = (rel >= 0) & (rel < L)
    for blk in range(ATTN_STEP_BLOCKS):
        rows = pl.ds(blk * L, L)
        prev = kvp_ref[...] if blk == 0 else kv_ref[pl.ds((blk - 1) * L, L), :]
        kv = jnp.concatenate([prev, kv_ref[rows, :]], axis=0)
        valid = band & (jnp.logical_not(pl.program_id(1) == 0) | (kj >= L)) if blk == 0 else band
        _swa_block(sink_ref, q_ref, o_ref, rows, kv, valid, lane_k)


def _swa_block(sink_ref, q_ref, o_ref, rows, kv, valid, lane_k):
    L = ATTN_BLOCK
    hd = ATTN_HEAD_DIM
    kk = kv[:, 0:LANES]
    vv = kv[:, LANES:2 * LANES].astype(BF)
    top = lax.broadcasted_iota(jnp.int32, (2 * L, 1), 0) < L
    lane_o = lax.broadcasted_iota(jnp.int32, (L, LANES), 1)
    scale = hd ** -0.5
    for j in range(ATTN_Q_HEADS // 2):
        g = (2 * j) // ATTN_REP
        q2 = q_ref[rows, j * LANES:(j + 1) * LANES]
        q2r = pltpu.roll(q2, hd, axis=1)
        kg = jnp.where((lane_k >= g * hd) & (lane_k < (g + 1) * hd), kk, 0.0).astype(BF)
        qa, qb = (q2, q2r) if g == 0 else (q2r, q2)
        qs = jnp.concatenate([qa, qb], axis=0).astype(BF)
        logits = lax.dot_general(qs, kg, (((1,), (1,)), ((), ())), preferred_element_type=F32) * scale
        logits = jnp.where(valid, logits, -jnp.inf)
        sink = jnp.where(top, sink_ref[2 * j], sink_ref[2 * j + 1])
        m = jnp.maximum(jnp.max(logits, axis=-1, keepdims=True), sink)
        p = jnp.exp(logits - m)
        probs = p * (1.0 / (jnp.sum(p, axis=-1, keepdims=True) + jnp.exp(sink - m)))
        o = jnp.dot(probs.astype(BF), vv, preferred_element_type=F32)
        oa, ob = o[0:L], o[L:2 * L]
        if g == 0:
            out2 = jnp.where(lane_o < hd, oa, pltpu.roll(ob, hd, axis=1))
        else:
            out2 = jnp.where(lane_o < hd, pltpu.roll(oa, hd, axis=1), ob)
        o_ref[rows, j * LANES:(j + 1) * LANES] = out2


def _swa(q, kv, seq, sinks):
    t = q.shape[0]
    sb = ATTN_STEP_BLOCKS
    rows = sb * ATTN_BLOCK
    ns = seq // rows
    tile = lambda b, n: (b * ns + n, 0)
    before = lambda b, n: (b * ns * sb + jnp.maximum(n * sb - 1, 0), 0)
    return pl.pallas_call(
        _swa_kernel,
        out_shape=jax.ShapeDtypeStruct((t, ATTN_Q_W), F32),
        grid=(t // seq, ns),
        in_specs=[pl.BlockSpec(memory_space=pltpu.SMEM),
                  pl.BlockSpec((rows, ATTN_Q_W), tile),
                  pl.BlockSpec((rows, ATTN_KV_W), tile),
                  pl.BlockSpec((ATTN_BLOCK, ATTN_KV_W), before)],
        out_specs=pl.BlockSpec((rows, ATTN_Q_W), tile),
        compiler_params=_cparams(("parallel", "parallel")),
        name="swa",
    )(sinks, q, kv, kv)


def _outproj_kernel(ya_ref, yb_ref, yc_ref, h_ref, wa_ref, wb_ref, wc_ref, g_ref, b_ref, o_ref):
    mix = (jnp.dot(ya_ref[...], wa_ref[...], preferred_element_type=F32)
           + jnp.dot(yb_ref[...], wb_ref[...], preferred_element_type=F32)
           + jnp.dot(yc_ref[...].astype(BF), wc_ref[...], preferred_element_type=F32))
    o_ref[...] = _ln_math(DN_ALPHA * h_ref[...] + mix, g_ref[...], b_ref[...])


def _out_proj(ya, yb, yc, h, w_out, g, b):
    t = h.shape[0]
    w = w_out.astype(BF)
    row = lambda i: (i, 0)
    const = lambda i: (0, 0)
    c1, c2 = W_LRU, W_LRU + SSM_D_INNER
    return pl.pallas_call(
        _outproj_kernel,
        out_shape=jax.ShapeDtypeStruct((t, D_MODEL), F32),
        grid=(t // ROW_TILE,),
        in_specs=[pl.BlockSpec((ROW_TILE, W_LRU), row), pl.BlockSpec((ROW_TILE, SSM_D_INNER), row),
                  pl.BlockSpec((ROW_TILE, ATTN_Q_W), row), pl.BlockSpec((ROW_TILE, D_MODEL), row),
                  pl.BlockSpec((W_LRU, D_MODEL), const), pl.BlockSpec((SSM_D_INNER, D_MODEL), const),
                  pl.BlockSpec((ATTN_Q_W, D_MODEL), const),
                  pl.BlockSpec((1, D_MODEL), const), pl.BlockSpec((1, D_MODEL), const)],
        out_specs=pl.BlockSpec((ROW_TILE, D_MODEL), row),
        compiler_params=_cparams(("parallel",)),
        name="out_proj",
    )(ya, yb, yc, h, w[0:c1], w[c1:c2], w[c2:MIX_WIDTH], g.reshape(1, -1), b.reshape(1, -1))


def _gather_group(idx_ref, tbl_ref, t, g):
    rs = ROW_WORDS_SUBLANES
    pieces = []
    for q in range(GROUP_PAIRS // 2):
        p = g * GROUP_PAIRS + 2 * q
        if p % SUBLANES == 0:
            window = idx_ref.at[t, pl.ds(p, SUBLANES)]
        ra = pl.multiple_of(window[p % SUBLANES], rs)
        rb = pl.multiple_of(window[p % SUBLANES + 1], rs)
        pieces.append(jnp.concatenate([tbl_ref[pl.ds(ra, rs), :], tbl_ref[pl.ds(rb, rs), :]], axis=0))
    return pltpu.bitcast(jnp.concatenate(pieces, axis=0), BF)


def _chunk_of_column(col):
    j = col % PAIR_LANES
    return j // 2 + ROW_WORDS_SUBLANES * (j % 2)


def _split_hi_lo(x):
    hi = x.astype(BF)
    lo = (x - hi.astype(F32)).astype(BF)
    return hi, lo


def _peer_in_kernel(idx_ref, x_ref, gate_ref, tbl_ref, o_ref, m_ref):
    n = x_ref.shape[0]
    row = lax.broadcasted_iota(jnp.int32, (2 * SUBLANES, GROUP_K), 0) % SUBLANES
    keep = row == _chunk_of_column(lax.broadcasted_iota(jnp.int32, (2 * SUBLANES, GROUP_K), 1))
    for t in range(n):
        x_row = x_ref[pl.ds(t, 1), :]
        x8 = jnp.concatenate([x_row[:, c * LANES:(c + 1) * LANES] for c in range(TOKEN_SUBLANES)], axis=0)
        x_hi, x_lo = _split_hi_lo(x8)
        x16 = jnp.concatenate([x_hi, x_lo], axis=0)
        for g in range(PEER_PAIRS // GROUP_PAIRS):
            r = lax.dot_general(x16, _gather_group(idx_ref, tbl_ref, t, g), (((1,), (1,)), ((), ())),
                                preferred_element_type=F32)
            m_ref[pl.ds(t, 1), g * GROUP_K:(g + 1) * GROUP_K] = jnp.sum(jnp.where(keep, r, 0.0), axis=0,
                                                                        keepdims=True)
    wide = PEER_PAIRS * PAIR_LANES
    fold = (lax.broadcasted_iota(jnp.int32, (wide, PEER_PAIRS), 0) // PAIR_LANES
            == lax.broadcasted_iota(jnp.int32, (wide, PEER_PAIRS), 1)).astype(BF)
    m_hi, m_lo = _split_hi_lo(m_ref[...])
    a = jnp.dot(m_hi, fold, preferred_element_type=F32) + jnp.dot(m_lo, fold, preferred_element_type=F32)
    o_ref[...] = gate_ref[...] * (0.5 * a * (1.0 + lax.erf(a * (2.0 ** -0.5))))


def _peer_in(rows, x, gate, tbl):
    t = x.shape[0]
    tt = GATHER_TOKENS
    pairs = pl.BlockSpec((tt, PEER_PAIRS), lambda i: (i, 0))
    return pl.pallas_call(
        _peer_in_kernel,
        out_shape=jax.ShapeDtypeStruct((t, PEER_PAIRS), F32),
        grid=(t // tt,),
        in_specs=[
            pl.BlockSpec((tt, PEER_PAIRS), lambda i: (i, 0), memory_space=pltpu.SMEM),
            pl.BlockSpec((tt, D_MODEL), lambda i: (i, 0)),
            pairs,
            pl.BlockSpec(memory_space=pltpu.VMEM),
        ],
        out_specs=pairs,
        scratch_shapes=[pltpu.VMEM((tt, PEER_PAIRS * PAIR_LANES), F32)],
        compiler_params=pltpu.CompilerParams(dimension_semantics=("parallel",), vmem_limit_bytes=PEER_VMEM_LIMIT),
        name="peer_in",
    )(rows, x, gate, tbl)


def _peer_out_kernel(idx_ref, w_ref, tbl_ref, h_ref, g_ref, b_ref, o_ref, wide_ref, ffn_ref):
    n = w_ref.shape[0]
    wide = PEER_PAIRS * PAIR_LANES
    rep = (lax.broadcasted_iota(jnp.int32, (PEER_PAIRS, wide), 1) // PAIR_LANES
           == lax.broadcasted_iota(jnp.int32, (PEER_PAIRS, wide), 0)).astype(BF)
    for k, wv in enumerate(_split_hi_lo(w_ref[...])):
        wide_ref[k] = jnp.dot(wv, rep, preferred_element_type=F32)
    keep = (lax.broadcasted_iota(jnp.int32, (SUBLANES, GROUP_K), 0)
            == _chunk_of_column(lax.broadcasted_iota(jnp.int32, (SUBLANES, GROUP_K), 1)))
    for t in range(n):
        acc = jnp.zeros((2 * SUBLANES, LANES), F32)
        for g in range(PEER_PAIRS // GROUP_PAIRS):
            cols = slice(g * GROUP_K, (g + 1) * GROUP_K)
            lhs = jnp.concatenate(
                [jnp.where(keep, jnp.broadcast_to(wide_ref[k, pl.ds(t, 1), cols], (SUBLANES, GROUP_K)), 0.0)
                 for k in range(2)], axis=0).astype(BF)
            acc = acc + jnp.dot(lhs, _gather_group(idx_ref, tbl_ref, t, g), preferred_element_type=F32)
        out8 = acc[0:SUBLANES] + acc[SUBLANES:]
        for c in range(TOKEN_SUBLANES):
            ffn_ref[pl.ds(t, 1), c * LANES:(c + 1) * LANES] = out8[c:c + 1, :]
    o_ref[...] = _ln_math(DN_ALPHA * h_ref[...] + ffn_ref[...], g_ref[...], b_ref[...])


def _peer_out(rows, w, tbl, h, g, b):
    t = h.shape[0]
    tt = GATHER_TOKENS
    row = pl.BlockSpec((tt, D_MODEL), lambda i: (i, 0))
    vec = pl.BlockSpec((1, D_MODEL), lambda i: (0, 0))
    return pl.pallas_call(
        _peer_out_kernel,
        out_shape=jax.ShapeDtypeStruct((t, D_MODEL), F32),
        grid=(t // tt,),
        in_specs=[
            pl.BlockSpec((tt, PEER_PAIRS), lambda i: (i, 0), memory_space=pltpu.SMEM),
            pl.BlockSpec((tt, PEER_PAIRS), lambda i: (i, 0)),
            pl.BlockSpec(memory_space=pltpu.VMEM),
            row, vec, vec,
        ],
        out_specs=row,
        scratch_shapes=[pltpu.VMEM((2, tt, PEER_PAIRS * PAIR_LANES), F32), pltpu.VMEM((tt, D_MODEL), F32)],
        compiler_params=pltpu.CompilerParams(vmem_limit_bytes=PEER_VMEM_LIMIT),
        name="peer_out",
    )(rows, w, tbl, h, g.reshape(1, D_MODEL), b.reshape(1, D_MODEL))


def _extract_topk(s, pos, k, payload=None):
    big = jnp.asarray(2 ** 30, pos.dtype)
    vals, poss, pays = [], [], []
    for _ in range(k):
        m = jnp.max(s, axis=0, keepdims=True)
        j = jnp.min(jnp.where(s == m, pos, big), axis=0, keepdims=True)
        sel = pos == j
        vals.append(m)
        poss.append(j)
        if payload is not None:
            pays.append(jnp.sum(jnp.where(sel, payload, jnp.zeros_like(payload)), axis=0, keepdims=True))
        s = jnp.where(sel, -jnp.inf, s)
    out = [jnp.concatenate(vals, axis=0), jnp.concatenate(poss, axis=0)]
    if payload is not None:
        out.append(jnp.concatenate(pays, axis=0))
    return out


def _topk_two_per_slot(s, k):
    m, n = s.shape[0] // 2, s.shape[1]
    big = jnp.asarray(2 ** 30, F32)
    pos_a = lax.broadcasted_iota(jnp.int32, (m, n), 0).astype(F32)
    a, b = s[:m], s[m:]
    b_wins = b > a
    win, wait = jnp.where(b_wins, b, a), jnp.where(b_wins, a, b)
    win_pos, wait_pos = jnp.where(b_wins, pos_a + m, pos_a), jnp.where(b_wins, pos_a, pos_a + m)
    vals, poss = [], []
    for _ in range(k):
        top = jnp.max(win, axis=0, keepdims=True)
        j = jnp.min(jnp.where(win == top, win_pos, big), axis=0, keepdims=True)
        sel = win_pos == j
        vals.append(top)
        poss.append(j)
        win = jnp.where(sel, wait, win)
        win_pos = jnp.where(sel, wait_pos, win_pos)
        wait = jnp.where(sel, -jnp.inf, wait)
    return jnp.concatenate(vals, axis=0), jnp.concatenate(poss, axis=0)


def _peer_route_kernel(h_ref, wqt_ref, keys_ref, idx_ref, gate_ref):
    half_dim = PEER_DKEY // 2
    hb = h_ref[...].astype(BF)
    n = hb.shape[0]
    iota16 = lax.broadcasted_iota(jnp.int32, (PEER_TOPK, n), 0).astype(F32)
    iota8 = lax.broadcasted_iota(jnp.int32, (SUBLANES, n), 0).astype(F32)
    qt = lax.dot_general(wqt_ref[...], hb, (((1,), (1,)), ((), ())), preferred_element_type=F32)
    idx_rows, gate_rows = [], []
    for hd in range(PEER_HEADS):
        tv, ti = [], []
        for half in range(2):
            r0 = hd * PEER_DKEY + half * half_dim
            s = jnp.dot(keys_ref[half], qt[r0:r0 + half_dim, :], preferred_element_type=F32)
            v, i = _topk_two_per_slot(s, PEER_TOPK)
            tv.append(v)
            ti.append(i)
        v1_lo, i1_lo = tv[1][0:SUBLANES, :], ti[1][0:SUBLANES, :]

        def block(a, first_row):
            shift = (lambda x: x) if first_row == 0 else (lambda x: pltpu.roll(x, first_row, axis=0))
            return (tv[0][a:a + 1, :] + shift(v1_lo), iota8 + (a * PEER_TOPK - first_row),
                    ti[0][a:a + 1, :] * PEER_NKEYS + shift(i1_lo))

        def pick(cond, x, y):
            return tuple(jnp.where(cond, xi, yi) for xi, yi in zip(x, y))

        padding = (jnp.full_like(v1_lo, -jnp.inf), iota8 + 2 ** 20, jnp.zeros_like(i1_lo))
        tiles = [block(1, 0),
                 pick(iota8 < 5, block(2, 0), block(4, 5)),
                 pick(iota8 < 4, block(3, 0), pick(iota8 < 6, block(5, 4), block(6, 6))),
                 pick(iota8 < 2, block(7, 0), padding),
                 (tv[0][SUBLANES:, :] + tv[1][0:1, :], (iota8 + SUBLANES) * PEER_TOPK,
                  ti[0][SUBLANES:, :] * PEER_NKEYS + ti[1][0:1, :])]
        cs = [tv[0][0:1, :] + tv[1]] + [tile[0] for tile in tiles]
        cp = [iota16] + [tile[1] for tile in tiles]
        ce = [ti[0][0:1, :] * PEER_NKEYS + ti[1]] + [tile[2] for tile in tiles]
        best_s, _, best_e = _extract_topk(jnp.concatenate(cs, axis=0), jnp.concatenate(cp, axis=0), PEER_TOPK,
                                          payload=jnp.concatenate(ce, axis=0))
        ex = jnp.exp(best_s - best_s[0:1, :])
        gate_rows.append(ex / jnp.sum(ex, axis=0, keepdims=True))
        idx_rows.append(best_e)
    idx_ref[...] = (jnp.concatenate(idx_rows, axis=0) * ROW_WORDS_SUBLANES).astype(jnp.int32).T
    gate_ref[...] = jnp.concatenate(gate_rows, axis=0).T


def _peer_route(h, wqt, keys):
    t = h.shape[0]
    rt = ROUTE_TOKENS
    pairs = pl.BlockSpec((rt, PEER_PAIRS), lambda i: (i, 0))
    return pl.pallas_call(
        _peer_route_kernel,
        out_shape=(jax.ShapeDtypeStruct((t, PEER_PAIRS), jnp.int32), jax.ShapeDtypeStruct((t, PEER_PAIRS), F32)),
        grid=(t // rt,),
        in_specs=[
            pl.BlockSpec((rt, D_MODEL), lambda i: (i, 0)),
            pl.BlockSpec((D_MODEL, D_MODEL), lambda i: (0, 0)),
            pl.BlockSpec((2, PEER_NKEYS, PEER_DKEY // 2), lambda i: (0, 0, 0)),
        ],
        out_specs=(pairs, pairs),
        compiler_params=_cparams(("parallel",)),
        name="peer_route",
    )(h, wqt, keys)


def _pack_kernel(u_ref, o_ref):
    half = D_MODEL // 2
    n = u_ref.shape[0]
    rs = ROW_WORDS_SUBLANES
    lo = lax.bitcast_convert_type(u_ref[:, 0:half].astype(BF).astype(F32), jnp.uint32) >> 16
    hi = lax.bitcast_convert_type(u_ref[:, half:D_MODEL].astype(BF).astype(F32), jnp.uint32) & jnp.uint32(0xFFFF0000)
    w = lo | hi
    for s in range(rs):
        o_ref[pl.ds(s, n, stride=rs), :] = w[:, s * LANES:(s + 1) * LANES]


def _pack_table(u):
    e = u.shape[0]
    return pl.pallas_call(
        _pack_kernel,
        out_shape=jax.ShapeDtypeStruct((e * ROW_WORDS_SUBLANES, LANES), jnp.uint32),
        grid=(e // ROW_TILE,),
        in_specs=[pl.BlockSpec((ROW_TILE, D_MODEL), lambda i: (i, 0))],
        out_specs=pl.BlockSpec((ROW_TILE * ROW_WORDS_SUBLANES, LANES), lambda i: (i, 0)),
        compiler_params=_cparams(("parallel",)),
        name="pack_table",
    )(u)


def _peer_sublayer(h, wq, keys, u, v, ln_g, ln_b):
    rows, gate = _peer_route(h, wq.T.astype(BF), keys)
    w = _peer_in(rows, h, gate, _pack_table(u))
    return _peer_out(rows, w, _pack_table(v), h, ln_g, ln_b)


def kernel(x, emb_ln_g, emb_ln_b, w_in, rg_conv_w, rg_conv_b, rg_wa, rg_ba, rg_wx, rg_bx, rg_lambda, ssm_conv_w, ssm_conv_b, ssm_dt_bias, ssm_a_log, ssm_d, ssm_norm_g, attn_sinks, w_out, ln1_g, ln1_b, peer_wq, peer_keys, peer_u, peer_v, ln2_g, ln2_b):
    bsz, seq, d = x.shape
    assert d == D_MODEL and seq % ROW_TILE == 0 and (bsz * seq) % ROUTE_TOKENS == 0
    assert rg_wa.shape[1:] == (LRU_BLOCKS, W_LRU // LRU_BLOCKS, W_LRU // LRU_BLOCKS)
    h = x.reshape(bsz * seq, d)
    for l in range(DEPTH):
        h, (rg, z, xbc, dt, q, kv) = _in_proj(h, _split_w_in(w_in[l]), (emb_ln_g, emb_ln_b) if l == 0 else None)
        y_a = _rg_lru(rg, seq, rg_conv_w[l], rg_conv_b[l], _block_diag(rg_wa[l]), rg_ba[l],
                      _block_diag(rg_wx[l]), rg_bx[l], rg_lambda[l])
        y_b = _ssd(z, xbc, dt, seq, ssm_conv_w[l], ssm_conv_b[l], ssm_dt_bias[l], ssm_a_log[l], ssm_d[l],
                   ssm_norm_g[l])
        y_c = _swa(q, kv, seq, attn_sinks[l])
        h = _out_proj(y_a, y_b, y_c, h, w_out[l], ln1_g[l], ln1_b[l])
        h = _peer_sublayer(h, peer_wq[l], peer_keys[l], peer_u[l], peer_v[l], ln2_g[l], ln2_b[l])
    return h.reshape(bsz, seq, d)
```

```python
import functools
import math

import jax
import jax.numpy as jnp
from jax import lax
from jax.experimental import pallas as pl
from jax.experimental.pallas import tpu as pltpu

D_MODEL = 1024
DEPTH = 2

W_LRU = D_MODEL // 2
LRU_BLOCKS = 8
LRU_C = 8.0
CONV_K = 4

SSM_HEAD_DIM = 64
SSM_D_INNER = D_MODEL
SSM_HEADS = SSM_D_INNER // SSM_HEAD_DIM
SSM_GROUPS = 2
SSM_STATE = 128
SSM_CHUNK = 128
SSM_STEP_CHUNKS = 2
SSM_CONV_CH = SSM_D_INNER + 2 * SSM_GROUPS * SSM_STATE
SSD_GROUP_W = SSM_D_INNER // SSM_GROUPS

ATTN_HEAD_DIM = 64
ATTN_Q_HEADS = (D_MODEL // 2) // ATTN_HEAD_DIM
ATTN_KV_HEADS = 2
ATTN_REP = ATTN_Q_HEADS // ATTN_KV_HEADS
ATTN_BLOCK = 128
ATTN_STEP_BLOCKS = 2
ATTN_Q_W = ATTN_Q_HEADS * ATTN_HEAD_DIM
ATTN_KV_W = 2 * ATTN_KV_HEADS * ATTN_HEAD_DIM

MIX_WIDTH = W_LRU + SSM_D_INNER + ATTN_Q_W

PEER_HEADS = 8
PEER_NKEYS = 128
PEER_DKEY = 128
PEER_TOPK = 16
PEER_PAIRS = PEER_HEADS * PEER_TOPK

DN_ALPHA = (2 * DEPTH) ** 0.25
LN_EPS = 1e-5

SUBLANES = 8
LANES = 128
HALO = SUBLANES
ROW_WORDS_SUBLANES = D_MODEL // 2 // LANES
TOKEN_SUBLANES = D_MODEL // LANES

ROW_TILE = 512
PAIR_LANES = 2 * ROW_WORDS_SUBLANES
GROUP_PAIRS = 32
GROUP_K = GROUP_PAIRS * PAIR_LANES
GATHER_TOKENS = 128
ROUTE_TOKENS = 256
VMEM_LIMIT = 48 * 1024 * 1024
PEER_VMEM_LIMIT = 56 * 1024 * 1024

BF = jnp.bfloat16
F32 = jnp.float32

PROJ_SEGS = (2 * W_LRU, SSM_D_INNER, SSM_CONV_CH, LANES, ATTN_Q_W, ATTN_KV_W)


def _cparams(sem):
    return pltpu.CompilerParams(dimension_semantics=sem, vmem_limit_bytes=VMEM_LIMIT)


def _softplus(x):
    return jnp.maximum(x, 0.0) + jnp.log1p(jnp.exp(-jnp.abs(x)))


def _sigmoid(x):
    return 1.0 / (1.0 + jnp.exp(-x))


def _silu(x):
    return x * _sigmoid(x)


def _split3(x):
    hi = x.astype(BF)
    rest = x - hi.astype(F32)
    mid = rest.astype(BF)
    return hi, mid, (rest - mid.astype(F32)).astype(BF)


def _gelu_tanh(x):
    return 0.5 * x * (1.0 + jnp.tanh(math.sqrt(2.0 / math.pi) * (x + 0.044715 * (x * x * x))))


def _ln_math(x, g, b):
    mu = jnp.mean(x, axis=-1, keepdims=True)
    xc = x - mu
    var = jnp.mean(xc * xc, axis=-1, keepdims=True)
    return xc * lax.rsqrt(var + LN_EPS) * g + b


def _inproj_kernel(*refs, entry_norm):
    n = len(PROJ_SEGS)
    if entry_norm:
        x_ref, g_ref, b_ref = refs[:3]
        w_refs, h_out_ref, o_refs = refs[3:3 + n], refs[3 + n], refs[4 + n:]
        h = _ln_math(x_ref[...], g_ref[...], b_ref[...])
        h_out_ref[...] = h
    else:
        w_refs, o_refs = refs[1:1 + n], refs[1 + n:]
        h = refs[0][...]
    hb = h.astype(BF)
    for w_ref, o_ref in zip(w_refs, o_refs):
        o_ref[...] = jnp.dot(hb, w_ref[...], preferred_element_type=F32)


def _in_proj(h, ws, entry_ln=None):
    t = h.shape[0]
    row = lambda w: pl.BlockSpec((ROW_TILE, w), lambda i: (i, 0))
    vec = pl.BlockSpec((1, D_MODEL), lambda i: (0, 0))
    norm = entry_ln is not None
    outs = pl.pallas_call(
        functools.partial(_inproj_kernel, entry_norm=norm),
        out_shape=tuple(jax.ShapeDtypeStruct((t, w), F32) for w in ((D_MODEL,) if norm else ()) + PROJ_SEGS),
        grid=(t // ROW_TILE,),
        in_specs=[row(D_MODEL)] + ([vec, vec] if norm else [])
        + [pl.BlockSpec((D_MODEL, w), lambda i: (0, 0)) for w in PROJ_SEGS],
        out_specs=tuple(row(w) for w in ((D_MODEL,) if norm else ()) + PROJ_SEGS),
        compiler_params=_cparams(("parallel",)),
        name="in_proj",
    )(h, *([v.reshape(1, D_MODEL) for v in entry_ln] if norm else []), *ws)
    return (outs[0], outs[1:]) if norm else (h, outs)


def _split_w_in(w_in):
    w = w_in.astype(BF)
    c = [0]
    for width in (2 * W_LRU, SSM_D_INNER, SSM_CONV_CH, SSM_HEADS, ATTN_Q_W, ATTN_KV_W):
        c.append(c[-1] + width)
    segs = [w[:, c[i]:c[i + 1]] for i in range(6)]
    segs[3] = jnp.pad(segs[3], ((0, 0), (0, LANES - SSM_HEADS)))
    return tuple(segs)


def _shift_rows(x, s, fill):
    n = x.shape[0]
    if s % SUBLANES == 0:
        return jnp.concatenate([jnp.full((s,) + x.shape[1:], fill, x.dtype), x[:n - s]], axis=0)
    rolled = pltpu.roll(x, s, axis=0)
    row = lax.broadcasted_iota(jnp.int32, x.shape, 0)
    return jnp.where(row < s, fill, rolled)


def _causal_conv(x, halo_ref, w_ref, b_ref):
    n = x.shape[0]
    halo_ref[pl.ds(HALO, n), :] = x
    out = b_ref[...] + w_ref[CONV_K - 1:CONV_K, :] * x
    for s in range(1, CONV_K):
        out = out + w_ref[CONV_K - 1 - s:CONV_K - s, :] * halo_ref[pl.ds(HALO - s, n), :]
    halo_ref[pl.ds(0, HALO), :] = x[n - HALO:n]
    return out


def _rglru_kernel(rg_ref, cw_ref, cb_ref, wa_ref, ba_ref, wx_ref, bx_ref, lam_ref, o_ref, halo_ref, carry_ref):
    n = rg_ref.shape[0]

    @pl.when(pl.program_id(1) == 0)
    def _():
        halo_ref[pl.ds(0, HALO), :] = jnp.zeros((HALO, halo_ref.shape[1]), F32)
        carry_ref[...] = jnp.zeros_like(carry_ref)

    x = rg_ref[:, 0:W_LRU]
    xc = _causal_conv(x, halo_ref, cw_ref, cb_ref)
    xb = xc.astype(BF)
    r = _sigmoid(jnp.dot(xb, wa_ref[...], preferred_element_type=F32) + ba_ref[...])
    i = _sigmoid(jnp.dot(xb, wx_ref[...], preferred_element_type=F32) + bx_ref[...])
    log_a = (-LRU_C * r) * _softplus(-lam_ref[...])
    a = jnp.exp(log_a)
    u = jnp.sqrt(-jnp.tanh(log_a) * (a * a + 1.0)) * (i * xc)
    k = 1
    while k < n:
        u = a * _shift_rows(u, k, 0.0) + u
        a = a * _shift_rows(a, k, 1.0)
        k *= 2
    h = u + a * carry_ref[0:1, :]
    carry_ref[...] = jnp.broadcast_to(h[n - 1:n, :], carry_ref.shape)
    o_ref[...] = (_gelu_tanh(rg_ref[:, W_LRU:2 * W_LRU]) * h).astype(BF)


def _rg_lru(rg, seq, conv_w, conv_b, wa, ba, wx, bx, lam):
    t = rg.shape[0]
    nb = seq // ROW_TILE
    row = lambda b, j: (b * nb + j, 0)
    const = lambda b, j: (0, 0)
    vec = pl.BlockSpec((1, W_LRU), const)
    return pl.pallas_call(
        _rglru_kernel,
        out_shape=jax.ShapeDtypeStruct((t, W_LRU), BF),
        grid=(t // seq, nb),
        in_specs=[pl.BlockSpec((ROW_TILE, 2 * W_LRU), row),
                  pl.BlockSpec((CONV_K, W_LRU), const), vec,
                  pl.BlockSpec((W_LRU, W_LRU), const), vec,
                  pl.BlockSpec((W_LRU, W_LRU), const), vec, vec],
        out_specs=pl.BlockSpec((ROW_TILE, W_LRU), row),
        scratch_shapes=[pltpu.VMEM((HALO + ROW_TILE, W_LRU), F32), pltpu.VMEM((SUBLANES, W_LRU), F32)],
        compiler_params=_cparams(("parallel", "arbitrary")),
        name="rg_lru",
    )(rg, conv_w, conv_b.reshape(1, -1), wa, ba.reshape(1, -1), wx, bx.reshape(1, -1), lam.reshape(1, -1))


def _block_diag(w):
    nb, c, _ = w.shape
    eye = jnp.eye(nb, dtype=w.dtype)
    return (eye[:, None, :, None] * w[:, :, None, :]).reshape(nb * c, nb * c).astype(BF)


def _ssd_kernel(z_ref, xbc_ref, dt_ref, cw_ref, cb_ref, dtb_ref, alog_ref, dskip_ref, ng_ref, expand_ref,
                o_ref, halo_ref, state_ref):
    @pl.when(pl.program_id(1) == 0)
    def _():
        halo_ref[pl.ds(0, HALO), :] = jnp.zeros((HALO, halo_ref.shape[1]), F32)
        state_ref[...] = jnp.zeros_like(state_ref)

    for ck in range(SSM_STEP_CHUNKS):
        _ssd_chunk(z_ref, xbc_ref, dt_ref, cw_ref, cb_ref, dtb_ref, alog_ref, dskip_ref, ng_ref, expand_ref,
                   o_ref, halo_ref, state_ref, pl.ds(ck * SSM_CHUNK, SSM_CHUNK))


def _ssd_chunk(z_ref, xbc_ref, dt_ref, cw_ref, cb_ref, dtb_ref, alog_ref, dskip_ref, ng_ref, expand_ref,
               o_ref, halo_ref, state_ref, rows):
    L = SSM_CHUNK
    xbc = xbc_ref[rows, :]
    conv = _silu(_causal_conv(xbc, halo_ref, cw_ref, cb_ref))
    xs = conv[:, 0:SSM_D_INNER]
    bm = conv[:, SSM_D_INNER:SSM_D_INNER + SSM_GROUPS * SSM_STATE]
    cm = conv[:, SSM_D_INNER + SSM_GROUPS * SSM_STATE:]

    dt = _softplus(dt_ref[rows, :] + dtb_ref[...])
    da = dt * (-jnp.exp(alog_ref[...]))
    row = lax.broadcasted_iota(jnp.int32, (L, L), 0)
    col = lax.broadcasted_iota(jnp.int32, (L, L), 1)
    causal = col <= row
    a_cs = sum(jnp.dot(causal.astype(BF), part, preferred_element_type=F32) for part in _split3(da))
    a_cs_t = a_cs.T
    expand = expand_ref[...]
    dt_full = sum(jnp.dot(part, expand, preferred_element_type=F32) for part in _split3(dt))
    acs_full = sum(jnp.dot(part, expand, preferred_element_type=F32) for part in _split3(a_cs))
    alast_full = acs_full[L - 1:L, :]
    xdt = xs * dt_full
    xdt_b = xdt.astype(BF)
    xst_b = (xdt * jnp.exp(alast_full - acs_full)).astype(BF)
    left = lax.broadcasted_iota(jnp.int32, (L, LANES), 1) < SSM_HEAD_DIM

    y_parts = []
    for g in range(SSM_GROUPS):
        cg = cm[:, g * SSM_STATE:(g + 1) * SSM_STATE].astype(BF)
        bg = bm[:, g * SSM_STATE:(g + 1) * SSM_STATE].astype(BF)
        cb = lax.dot_general(cg, bg, (((1,), (1,)), ((), ())), preferred_element_type=F32)
        st = state_ref[:, g * SSD_GROUP_W:(g + 1) * SSD_GROUP_W]
        y_off = jnp.dot(cg, st.astype(BF), preferred_element_type=F32)
        for j in range(SSD_GROUP_W // LANES):
            h0 = g * (SSM_HEADS // SSM_GROUPS) + 2 * j
            ms = []
            for h in (h0, h0 + 1):
                seg = a_cs[:, h:h + 1] - a_cs_t[h:h + 1, :]
                ms.append((cb * jnp.exp(jnp.where(causal, seg, -jnp.inf))).astype(BF))
            c0 = h0 * SSM_HEAD_DIM
            x2 = xdt_b[:, c0:c0 + LANES]
            zero = jnp.zeros_like(x2)
            xblk = jnp.concatenate([jnp.where(left, x2, zero), jnp.where(left, zero, x2)], axis=0)
            y_parts.append(jnp.dot(jnp.concatenate(ms, axis=1), xblk, preferred_element_type=F32)
                           + y_off[:, j * LANES:(j + 1) * LANES] * jnp.exp(acs_full[:, c0:c0 + LANES]))
        new = lax.dot_general(bg, xst_b[:, g * SSD_GROUP_W:(g + 1) * SSD_GROUP_W], (((0,), (0,)), ((), ())),
                              preferred_element_type=F32)
        state_ref[:, g * SSD_GROUP_W:(g + 1) * SSD_GROUP_W] = (
            st * jnp.exp(alast_full[:, g * SSD_GROUP_W:(g + 1) * SSD_GROUP_W]) + new)
    y = jnp.concatenate(y_parts, axis=1) + dskip_ref[...] * xs
    y = y * _silu(z_ref[rows, :])
    outs = []
    for g in range(SSM_GROUPS):
        yg = y[:, g * SSD_GROUP_W:(g + 1) * SSD_GROUP_W]
        outs.append(yg * lax.rsqrt(jnp.mean(yg * yg, axis=-1, keepdims=True) + LN_EPS))
    o_ref[rows, :] = (jnp.concatenate(outs, axis=1) * ng_ref[...]).astype(BF)


def _ssd(z, xbc, dt, seq, conv_w, conv_b, dt_bias, a_log, d_skip, norm_g):
    t = z.shape[0]
    step = SSM_STEP_CHUNKS * SSM_CHUNK
    nc = seq // step
    row = lambda b, c: (b * nc + c, 0)
    const = lambda b, c: (0, 0)
    pad = LANES - SSM_HEADS
    expand = (jnp.arange(LANES)[:, None] == (jnp.arange(SSM_D_INNER)[None, :] // SSM_HEAD_DIM)).astype(BF)
    return pl.pallas_call(
        _ssd_kernel,
        out_shape=jax.ShapeDtypeStruct((t, SSM_D_INNER), BF),
        grid=(t // seq, nc),
        in_specs=[pl.BlockSpec((step, SSM_D_INNER), row),
                  pl.BlockSpec((step, SSM_CONV_CH), row),
                  pl.BlockSpec((step, LANES), row),
                  pl.BlockSpec((CONV_K, SSM_CONV_CH), const),
                  pl.BlockSpec((1, SSM_CONV_CH), const),
                  pl.BlockSpec((1, LANES), const),
                  pl.BlockSpec((1, LANES), const),
                  pl.BlockSpec((1, SSM_D_INNER), const),
                  pl.BlockSpec((1, SSM_D_INNER), const),
                  pl.BlockSpec((LANES, SSM_D_INNER), const)],
        out_specs=pl.BlockSpec((step, SSM_D_INNER), row),
        scratch_shapes=[pltpu.VMEM((HALO + SSM_CHUNK, SSM_CONV_CH), F32), pltpu.VMEM((SSM_STATE, SSM_D_INNER), F32)],
        compiler_params=_cparams(("parallel", "arbitrary")),
        name="ssd",
    )(z, xbc, dt, conv_w, conv_b.reshape(1, -1), jnp.pad(dt_bias, (0, pad)).reshape(1, -1),
      jnp.pad(a_log, (0, pad)).reshape(1, -1), jnp.repeat(d_skip, SSM_HEAD_DIM).reshape(1, -1),
      norm_g.reshape(1, -1), expand)


def _swa_kernel(sink_ref, q_ref, kv_ref, kvp_ref, o_ref):
    L = ATTN_BLOCK
    lane_k = lax.broadcasted_iota(jnp.int32, (2 * L, LANES), 1)
    qi = lax.broadcasted_iota(jnp.int32, (2 * L, 2 * L), 0) % L
    kj = lax.broadcasted_iota(jnp.int32, (2 * L, 2 * L), 1)
    rel = qi + L - kj
    band = (rel >= 0) & (rel < L)
    for blk in range(ATTN_STEP_BLOCKS):
        rows = pl.ds(blk * L, L)
        prev = kvp_ref[...] if blk == 0 else kv_ref[pl.ds((blk - 1) * L, L), :]
        kv = jnp.concatenate([prev, kv_ref[rows, :]], axis=0)
        valid = band & (jnp.logical_not(pl.program_id(1) == 0) | (kj >= L)) if blk == 0 else band
        _swa_block(sink_ref, q_ref, o_ref, rows, kv, valid, lane_k)


def _swa_block(sink_ref, q_ref, o_ref, rows, kv, valid, lane_k):
    L = ATTN_BLOCK
    hd = ATTN_HEAD_DIM
    kk = kv[:, 0:LANES]
    vv = kv[:, LANES:2 * LANES].astype(BF)
    top = lax.broadcasted_iota(jnp.int32, (2 * L, 1), 0) < L
    lane_o = lax.broadcasted_iota(jnp.int32, (L, LANES), 1)
    scale = hd ** -0.5
    for j in range(ATTN_Q_HEADS // 2):
        g = (2 * j) // ATTN_REP
        q2 = q_ref[rows, j * LANES:(j + 1) * LANES]
        q2r = pltpu.roll(q2, hd, axis=1)
        kg = jnp.where((lane_k >= g * hd) & (lane_k < (g + 1) * hd), kk, 0.0).astype(BF)
        qa, qb = (q2, q2r) if g == 0 else (q2r, q2)
        qs = jnp.concatenate([qa, qb], axis=0).astype(BF)
        logits = lax.dot_general(qs, kg, (((1,), (1,)), ((), ())), preferred_element_type=F32) * scale
        logits = jnp.where(valid, logits, -jnp.inf)
        sink = jnp.where(top, sink_ref[2 * j], sink_ref[2 * j + 1])
        m = jnp.maximum(jnp.max(logits, axis=-1, keepdims=True), sink)
        p = jnp.exp(logits - m)
        probs = p * (1.0 / (jnp.sum(p, axis=-1, keepdims=True) + jnp.exp(sink - m)))
        o = jnp.dot(probs.astype(BF), vv, preferred_element_type=F32)
        oa, ob = o[0:L], o[L:2 * L]
        if g == 0:
            out2 = jnp.where(lane_o < hd, oa, pltpu.roll(ob, hd, axis=1))
        else:
            out2 = jnp.where(lane_o < hd, pltpu.roll(oa, hd, axis=1), ob)
        o_ref[rows, j * LANES:(j + 1) * LANES] = out2


def _swa(q, kv, seq, sinks):
    t = q.shape[0]
    sb = ATTN_STEP_BLOCKS
    rows = sb * ATTN_BLOCK
    ns = seq // rows
    tile = lambda b, n: (b * ns + n, 0)
    before = lambda b, n: (b * ns * sb + jnp.maximum(n * sb - 1, 0), 0)
    return pl.pallas_call(
        _swa_kernel,
        out_shape=jax.ShapeDtypeStruct((t, ATTN_Q_W), F32),
        grid=(t // seq, ns),
        in_specs=[pl.BlockSpec(memory_space=pltpu.SMEM),
                  pl.BlockSpec((rows, ATTN_Q_W), tile),
                  pl.BlockSpec((rows, ATTN_KV_W), tile),
                  pl.BlockSpec((ATTN_BLOCK, ATTN_KV_W), before)],
        out_specs=pl.BlockSpec((rows, ATTN_Q_W), tile),
        compiler_params=_cparams(("parallel", "parallel")),
        name="swa",
    )(sinks, q, kv, kv)


def _outproj_kernel(ya_ref, yb_ref, yc_ref, h_ref, wa_ref, wb_ref, wc_ref, g_ref, b_ref, o_ref):
    mix = (jnp.dot(ya_ref[...], wa_ref[...], preferred_element_type=F32)
           + jnp.dot(yb_ref[...], wb_ref[...], preferred_element_type=F32)
           + jnp.dot(yc_ref[...].astype(BF), wc_ref[...], preferred_element_type=F32))
    o_ref[...] = _ln_math(DN_ALPHA * h_ref[...] + mix, g_ref[...], b_ref[...])


def _out_proj(ya, yb, yc, h, w_out, g, b):
    t = h.shape[0]
    w = w_out.astype(BF)
    row = lambda i: (i, 0)
    const = lambda i: (0, 0)
    c1, c2 = W_LRU, W_LRU + SSM_D_INNER
    return pl.pallas_call(
        _outproj_kernel,
        out_shape=jax.ShapeDtypeStruct((t, D_MODEL), F32),
        grid=(t // ROW_TILE,),
        in_specs=[pl.BlockSpec((ROW_TILE, W_LRU), row), pl.BlockSpec((ROW_TILE, SSM_D_INNER), row),
                  pl.BlockSpec((ROW_TILE, ATTN_Q_W), row), pl.BlockSpec((ROW_TILE, D_MODEL), row),
                  pl.BlockSpec((W_LRU, D_MODEL), const), pl.BlockSpec((SSM_D_INNER, D_MODEL), const),
                  pl.BlockSpec((ATTN_Q_W, D_MODEL), const),
                  pl.BlockSpec((1, D_MODEL), const), pl.BlockSpec((1, D_MODEL), const)],
        out_specs=pl.BlockSpec((ROW_TILE, D_MODEL), row),
        compiler_params=_cparams(("parallel",)),
        name="out_proj",
    )(ya, yb, yc, h, w[0:c1], w[c1:c2], w[c2:MIX_WIDTH], g.reshape(1, -1), b.reshape(1, -1))


def _gather_group(idx_ref, tbl_ref, t, g):
    rs = ROW_WORDS_SUBLANES
    pieces = []
    for q in range(GROUP_PAIRS // 2):
        p = g * GROUP_PAIRS + 2 * q
        if p % SUBLANES == 0:
            window = idx_ref.at[t, pl.ds(p, SUBLANES)]
        ra = pl.multiple_of(window[p % SUBLANES], rs)
        rb = pl.multiple_of(window[p % SUBLANES + 1], rs)
        pieces.append(jnp.concatenate([tbl_ref[pl.ds(ra, rs), :], tbl_ref[pl.ds(rb, rs), :]], axis=0))
    return pltpu.bitcast(jnp.concatenate(pieces, axis=0), BF)


def _chunk_of_column(col):
    j = col % PAIR_LANES
    return j // 2 + ROW_WORDS_SUBLANES * (j % 2)


def _split_hi_lo(x):
    hi = x.astype(BF)
    lo = (x - hi.astype(F32)).astype(BF)
    return hi, lo


def _peer_in_kernel(idx_ref, x_ref, gate_ref, tbl_ref, o_ref, m_ref):
    n = x_ref.shape[0]
    row = lax.broadcasted_iota(jnp.int32, (2 * SUBLANES, GROUP_K), 0) % SUBLANES
    keep = row == _chunk_of_column(lax.broadcasted_iota(jnp.int32, (2 * SUBLANES, GROUP_K), 1))
    for t in range(n):
        x_row = x_ref[pl.ds(t, 1), :]
        x8 = jnp.concatenate([x_row[:, c * LANES:(c + 1) * LANES] for c in range(TOKEN_SUBLANES)], axis=0)
        x_hi, x_lo = _split_hi_lo(x8)
        x16 = jnp.concatenate([x_hi, x_lo], axis=0)
        for g in range(PEER_PAIRS // GROUP_PAIRS):
            r = lax.dot_general(x16, _gather_group(idx_ref, tbl_ref, t, g), (((1,), (1,)), ((), ())),
                                preferred_element_type=F32)
            m_ref[pl.ds(t, 1), g * GROUP_K:(g + 1) * GROUP_K] = jnp.sum(jnp.where(keep, r, 0.0), axis=0,
                                                                        keepdims=True)
    wide = PEER_PAIRS * PAIR_LANES
    fold = (lax.broadcasted_iota(jnp.int32, (wide, PEER_PAIRS), 0) // PAIR_LANES
            == lax.broadcasted_iota(jnp.int32, (wide, PEER_PAIRS), 1)).astype(BF)
    m_hi, m_lo = _split_hi_lo(m_ref[...])
    a = jnp.dot(m_hi, fold, preferred_element_type=F32) + jnp.dot(m_lo, fold, preferred_element_type=F32)
    o_ref[...] = gate_ref[...] * (0.5 * a * (1.0 + lax.erf(a * (2.0 ** -0.5))))


def _peer_in(rows, x, gate, tbl):
    t = x.shape[0]
    tt = GATHER_TOKENS
    pairs = pl.BlockSpec((tt, PEER_PAIRS), lambda i: (i, 0))
    return pl.pallas_call(
        _peer_in_kernel,
        out_shape=jax.ShapeDtypeStruct((t, PEER_PAIRS), F32),
        grid=(t // tt,),
        in_specs=[
            pl.BlockSpec((tt, PEER_PAIRS), lambda i: (i, 0), memory_space=pltpu.SMEM),
            pl.BlockSpec((tt, D_MODEL), lambda i: (i, 0)),
            pairs,
            pl.BlockSpec(memory_space=pltpu.VMEM),
        ],
        out_specs=pairs,
        scratch_shapes=[pltpu.VMEM((tt, PEER_PAIRS * PAIR_LANES), F32)],
        compiler_params=pltpu.CompilerParams(dimension_semantics=("parallel",), vmem_limit_bytes=PEER_VMEM_LIMIT),
        name="peer_in",
    )(rows, x, gate, tbl)


def _peer_out_kernel(idx_ref, w_ref, tbl_ref, h_ref, g_ref, b_ref, o_ref, wide_ref, ffn_ref):
    n = w_ref.shape[0]
    wide = PEER_PAIRS * PAIR_LANES
    rep = (lax.broadcasted_iota(jnp.int32, (PEER_PAIRS, wide), 1) // PAIR_LANES
           == lax.broadcasted_iota(jnp.int32, (PEER_PAIRS, wide), 0)).astype(BF)
    for k, wv in enumerate(_split_hi_lo(w_ref[...])):
        wide_ref[k] = jnp.dot(wv, rep, preferred_element_type=F32)
    keep = (lax.broadcasted_iota(jnp.int32, (SUBLANES, GROUP_K), 0)
            == _chunk_of_column(lax.broadcasted_iota(jnp.int32, (SUBLANES, GROUP_K), 1)))
    for t in range(n):
        acc = jnp.zeros((2 * SUBLANES, LANES), F32)
        for g in range(PEER_PAIRS // GROUP_PAIRS):
            cols = slice(g * GROUP_K, (g + 1) * GROUP_K)
            lhs = jnp.concatenate(
                [jnp.where(keep, jnp.broadcast_to(wide_ref[k, pl.ds(t, 1), cols], (SUBLANES, GROUP_K)), 0.0)
                 for k in range(2)], axis=0).astype(BF)
            acc = acc + jnp.dot(lhs, _gather_group(idx_ref, tbl_ref, t, g), preferred_element_type=F32)
        out8 = acc[0:SUBLANES] + acc[SUBLANES:]
        for c in range(TOKEN_SUBLANES):
            ffn_ref[pl.ds(t, 1), c * LANES:(c + 1) * LANES] = out8[c:c + 1, :]
    o_ref[...] = _ln_math(DN_ALPHA * h_ref[...] + ffn_ref[...], g_ref[...], b_ref[...])


def _peer_out(rows, w, tbl, h, g, b):
    t = h.shape[0]
    tt = GATHER_TOKENS
    row = pl.BlockSpec((tt, D_MODEL), lambda i: (i, 0))
    vec = pl.BlockSpec((1, D_MODEL), lambda i: (0, 0))
    return pl.pallas_call(
        _peer_out_kernel,
        out_shape=jax.ShapeDtypeStruct((t, D_MODEL), F32),
        grid=(t // tt,),
        in_specs=[
            pl.BlockSpec((tt, PEER_PAIRS), lambda i: (i, 0), memory_space=pltpu.SMEM),
            pl.BlockSpec((tt, PEER_PAIRS), lambda i: (i, 0)),
            pl.BlockSpec(memory_space=pltpu.VMEM),
            row, vec, vec,
        ],
        out_specs=row,
        scratch_shapes=[pltpu.VMEM((2, tt, PEER_PAIRS * PAIR_LANES), F32), pltpu.VMEM((tt, D_MODEL), F32)],
        compiler_params=pltpu.CompilerParams(vmem_limit_bytes=PEER_VMEM_LIMIT),
        name="peer_out",
    )(rows, w, tbl, h, g.reshape(1, D_MODEL), b.reshape(1, D_MODEL))


def _extract_topk(s, pos, k, payload=None):
    big = jnp.asarray(2 ** 30, pos.dtype)
    vals, poss, pays = [], [], []
    for _ in range(k):
        m = jnp.max(s, axis=0, keepdims=True)
        j = jnp.min(jnp.where(s == m, pos, big), axis=0, keepdims=True)
        sel = pos == j
        vals.append(m)
        poss.append(j)
        if payload is not None:
            pays.append(jnp.sum(jnp.where(sel, payload, jnp.zeros_like(payload)), axis=0, keepdims=True))
        s = jnp.where(sel, -jnp.inf, s)
    out = [jnp.concatenate(vals, axis=0), jnp.concatenate(poss, axis=0)]
    if payload is not None:
        out.append(jnp.concatenate(pays, axis=0))
    return out


def _topk_four_per_slot(s, k):
    m, n = s.shape[0] // 4, s.shape[1]
    big = jnp.asarray(2 ** 30, F32)
    base = lax.broadcasted_iota(jnp.int32, (m, n), 0).astype(F32)
    val = [s[i * m:(i + 1) * m] for i in range(4)]
    pos = [base + i * m for i in range(4)]

    def order(i, j):
        x, y, px, py = val[i], val[j], pos[i], pos[j]
        swap = (y > x) | ((y == x) & (py < px))
        val[i], val[j] = jnp.where(swap, y, x), jnp.where(swap, x, y)
        pos[i], pos[j] = jnp.where(swap, py, px), jnp.where(swap, px, py)

    for i, j in ((0, 1), (2, 3), (0, 2), (1, 3), (1, 2)):
        order(i, j)
    tops, top_pos = [], []
    for _ in range(k):
        top = jnp.max(val[0], axis=0, keepdims=True)
        jmin = jnp.min(jnp.where(val[0] == top, pos[0], big), axis=0, keepdims=True)
        sel = pos[0] == jmin
        tops.append(top)
        top_pos.append(jmin)
        for i in range(3):
            val[i] = jnp.where(sel, val[i + 1], val[i])
            pos[i] = jnp.where(sel, pos[i + 1], pos[i])
        val[3] = jnp.where(sel, -jnp.inf, val[3])
    return jnp.concatenate(tops, axis=0), jnp.concatenate(top_pos, axis=0)


def _peer_route_kernel(h_ref, wqt_ref, keys_ref, idx_ref, gate_ref):
    half_dim = PEER_DKEY // 2
    hb = h_ref[...].astype(BF)
    n = hb.shape[0]
    iota16 = lax.broadcasted_iota(jnp.int32, (PEER_TOPK, n), 0).astype(F32)
    iota8 = lax.broadcasted_iota(jnp.int32, (SUBLANES, n), 0).astype(F32)
    qt = lax.dot_general(wqt_ref[...], hb, (((1,), (1,)), ((), ())), preferred_element_type=F32)
    idx_rows, gate_rows = [], []
    for hd in range(PEER_HEADS):
        tv, ti = [], []
        for half in range(2):
            r0 = hd * PEER_DKEY + half * half_dim
            s = jnp.dot(keys_ref[half], qt[r0:r0 + half_dim, :], preferred_element_type=F32)
            v, i = _topk_four_per_slot(s, PEER_TOPK)
            tv.append(v)
            ti.append(i)
        v1_lo, i1_lo = tv[1][0:SUBLANES, :], ti[1][0:SUBLANES, :]

        def block(a, first_row):
            shift = (lambda x: x) if first_row == 0 else (lambda x: pltpu.roll(x, first_row, axis=0))
            return (tv[0][a:a + 1, :] + shift(v1_lo), iota8 + (a * PEER_TOPK - first_row),
                    ti[0][a:a + 1, :] * PEER_NKEYS + shift(i1_lo))

        def pick(cond, x, y):
            return tuple(jnp.where(cond, xi, yi) for xi, yi in zip(x, y))

        padding = (jnp.full_like(v1_lo, -jnp.inf), iota8 + 2 ** 20, jnp.zeros_like(i1_lo))
        tiles = [block(1, 0),
                 pick(iota8 < 5, block(2, 0), block(4, 5)),
                 pick(iota8 < 4, block(3, 0), pick(iota8 < 6, block(5, 4), block(6, 6))),
                 pick(iota8 < 2, block(7, 0), padding),
                 (tv[0][SUBLANES:, :] + tv[1][0:1, :], (iota8 + SUBLANES) * PEER_TOPK,
                  ti[0][SUBLANES:, :] * PEER_NKEYS + ti[1][0:1, :])]
        cs = [tv[0][0:1, :] + tv[1]] + [tile[0] for tile in tiles]
        cp = [iota16] + [tile[1] for tile in tiles]
        ce = [ti[0][0:1, :] * PEER_NKEYS + ti[1]] + [tile[2] for tile in tiles]
        best_s, _, best_e = _extract_topk(jnp.concatenate(cs, axis=0), jnp.concatenate(cp, axis=0), PEER_TOPK,
                                          payload=jnp.concatenate(ce, axis=0))
        ex = jnp.exp(best_s - best_s[0:1, :])
        gate_rows.append(ex / jnp.sum(ex, axis=0, keepdims=True))
        idx_rows.append(best_e)
    idx_ref[...] = (jnp.concatenate(idx_rows, axis=0) * ROW_WORDS_SUBLANES).astype(jnp.int32).T
    gate_ref[...] = jnp.concatenate(gate_rows, axis=0).T


def _peer_route(h, wqt, keys):
    t = h.shape[0]
    rt = ROUTE_TOKENS
    pairs = pl.BlockSpec((rt, PEER_PAIRS), lambda i: (i, 0))
    return pl.pallas_call(
        _peer_route_kernel,
        out_shape=(jax.ShapeDtypeStruct((t, PEER_PAIRS), jnp.int32), jax.ShapeDtypeStruct((t, PEER_PAIRS), F32)),
        grid=(t // rt,),
        in_specs=[
            pl.BlockSpec((rt, D_MODEL), lambda i: (i, 0)),
            pl.BlockSpec((D_MODEL, D_MODEL), lambda i: (0, 0)),
            pl.BlockSpec((2, PEER_NKEYS, PEER_DKEY // 2), lambda i: (0, 0, 0)),
        ],
        out_specs=(pairs, pairs),
        compiler_params=_cparams(("parallel",)),
        name="peer_route",
    )(h, wqt, keys)


def _pack_kernel(u_ref, o_ref):
    half = D_MODEL // 2
    n = u_ref.shape[0]
    rs = ROW_WORDS_SUBLANES
    lo = lax.bitcast_convert_type(u_ref[:, 0:half].astype(BF).astype(F32), jnp.uint32) >> 16
    hi = lax.bitcast_convert_type(u_ref[:, half:D_MODEL].astype(BF).astype(F32), jnp.uint32) & jnp.uint32(0xFFFF0000)
    w = lo | hi
    for s in range(rs):
        o_ref[pl.ds(s, n, stride=rs), :] = w[:, s * LANES:(s + 1) * LANES]


def _pack_table(u):
    e = u.shape[0]
    return pl.pallas_call(
        _pack_kernel,
        out_shape=jax.ShapeDtypeStruct((e * ROW_WORDS_SUBLANES, LANES), jnp.uint32),
        grid=(e // ROW_TILE,),
        in_specs=[pl.BlockSpec((ROW_TILE, D_MODEL), lambda i: (i, 0))],
        out_specs=pl.BlockSpec((ROW_TILE * ROW_WORDS_SUBLANES, LANES), lambda i: (i, 0)),
        compiler_params=_cparams(("parallel",)),
        name="pack_table",
    )(u)


def _peer_sublayer(h, wq, keys, u, v, ln_g, ln_b):
    rows, gate = _peer_route(h, wq.T.astype(BF), keys)
    w = _peer_in(rows, h, gate, _pack_table(u))
    return _peer_out(rows, w, _pack_table(v), h, ln_g, ln_b)


def kernel(x, emb_ln_g, emb_ln_b, w_in, rg_conv_w, rg_conv_b, rg_wa, rg_ba, rg_wx, rg_bx, rg_lambda, ssm_conv_w, ssm_conv_b, ssm_dt_bias, ssm_a_log, ssm_d, ssm_norm_g, attn_sinks, w_out, ln1_g, ln1_b, peer_wq, peer_keys, peer_u, peer_v, ln2_g, ln2_b):
    bsz, seq, d = x.shape
    assert d == D_MODEL and seq % ROW_TILE == 0 and (bsz * seq) % ROUTE_TOKENS == 0
    assert rg_wa.shape[1:] == (LRU_BLOCKS, W_LRU // LRU_BLOCKS, W_LRU // LRU_BLOCKS)
    h = x.reshape(bsz * seq, d)
    for l in range(DEPTH):
        h, (rg, z, xbc, dt, q, kv) = _in_proj(h, _split_w_in(w_in[l]), (emb_ln_g, emb_ln_b) if l == 0 else None)
        y_a = _rg_lru(rg, seq, rg_conv_w[l], rg_conv_b[l], _block_diag(rg_wa[l]), rg_ba[l],
                      _block_diag(rg_wx[l]), rg_bx[l], rg_lambda[l])
        y_b = _ssd(z, xbc, dt, seq, ssm_conv_w[l], ssm_conv_b[l], ssm_dt_bias[l], ssm_a_log[l], ssm_d[l],
                   ssm_norm_g[l])
        y_c = _swa(q, kv, seq, attn_sinks[l])
        h = _out_proj(y_a, y_b, y_c, h, w_out[l], ln1_g[l], ln1_b[l])
        h = _peer_sublayer(h, peer_wq[l], peer_keys[l], peer_u[l], peer_v[l], ln2_g[l], ln2_b[l])
    return h.reshape(bsz, seq, d)
```

```python
import functools
import math

import jax
import jax.numpy as jnp
from jax import lax
from jax.experimental import pallas as pl
from jax.experimental.pallas import tpu as pltpu

D_MODEL = 1024
DEPTH = 2

W_LRU = D_MODEL // 2
LRU_BLOCKS = 8
LRU_C = 8.0
CONV_K = 4

SSM_HEAD_DIM = 64
SSM_D_INNER = D_MODEL
SSM_HEADS = SSM_D_INNER // SSM_HEAD_DIM
SSM_GROUPS = 2
SSM_STATE = 128
SSM_CHUNK = 128
SSM_STEP_CHUNKS = 2
SSM_CONV_CH = SSM_D_INNER + 2 * SSM_GROUPS * SSM_STATE
SSD_GROUP_W = SSM_D_INNER // SSM_GROUPS

ATTN_HEAD_DIM = 64
ATTN_Q_HEADS = (D_MODEL // 2) // ATTN_HEAD_DIM
ATTN_KV_HEADS = 2
ATTN_REP = ATTN_Q_HEADS // ATTN_KV_HEADS
ATTN_BLOCK = 128
ATTN_STEP_BLOCKS = 2
ATTN_Q_W = ATTN_Q_HEADS * ATTN_HEAD_DIM
ATTN_KV_W = 2 * ATTN_KV_HEADS * ATTN_HEAD_DIM

MIX_WIDTH = W_LRU + SSM_D_INNER + ATTN_Q_W

PEER_HEADS = 8
PEER_NKEYS = 128
PEER_DKEY = 128
PEER_TOPK = 16
PEER_PAIRS = PEER_HEADS * PEER_TOPK

DN_ALPHA = (2 * DEPTH) ** 0.25
LN_EPS = 1e-5

SUBLANES = 8
LANES = 128
HALO = SUBLANES
ROW_WORDS_SUBLANES = D_MODEL // 2 // LANES
TOKEN_SUBLANES = D_MODEL // LANES

ROW_TILE = 512
PAIR_LANES = 2 * ROW_WORDS_SUBLANES
GROUP_PAIRS = 32
GROUP_K = GROUP_PAIRS * PAIR_LANES
GATHER_TOKENS = 128
ROUTE_TOKENS = 256
VMEM_LIMIT = 48 * 1024 * 1024
PEER_VMEM_LIMIT = 56 * 1024 * 1024

BF = jnp.bfloat16
F32 = jnp.float32

PROJ_SEGS = (2 * W_LRU, SSM_D_INNER, SSM_CONV_CH, LANES, ATTN_Q_W, ATTN_KV_W)


def _cparams(sem):
    return pltpu.CompilerParams(dimension_semantics=sem, vmem_limit_bytes=VMEM_LIMIT)


def _softplus(x):
    return jnp.maximum(x, 0.0) + jnp.log1p(jnp.exp(-jnp.abs(x)))


def _sigmoid(x):
    return 1.0 / (1.0 + jnp.exp(-x))


def _silu(x):
    return x * _sigmoid(x)


def _split3(x):
    hi = x.astype(BF)
    rest = x - hi.astype(F32)
    mid = rest.astype(BF)
    return hi, mid, (rest - mid.astype(F32)).astype(BF)


def _gelu_tanh(x):
    return 0.5 * x * (1.0 + jnp.tanh(math.sqrt(2.0 / math.pi) * (x + 0.044715 * (x * x * x))))


def _ln_math(x, g, b):
    mu = jnp.mean(x, axis=-1, keepdims=True)
    xc = x - mu
    var = jnp.mean(xc * xc, axis=-1, keepdims=True)
    return xc * lax.rsqrt(var + LN_EPS) * g + b


def _inproj_kernel(*refs, entry_norm):
    n = len(PROJ_SEGS)
    if entry_norm:
        x_ref, g_ref, b_ref = refs[:3]
        w_refs, h_out_ref, o_refs = refs[3:3 + n], refs[3 + n], refs[4 + n:]
        h = _ln_math(x_ref[...], g_ref[...], b_ref[...])
        h_out_ref[...] = h
    else:
        w_refs, o_refs = refs[1:1 + n], refs[1 + n:]
        h = refs[0][...]
    hb = h.astype(BF)
    for w_ref, o_ref in zip(w_refs, o_refs):
        o_ref[...] = jnp.dot(hb, w_ref[...], preferred_element_type=F32)


def _in_proj(h, ws, entry_ln=None):
    t = h.shape[0]
    row = lambda w: pl.BlockSpec((ROW_TILE, w), lambda i: (i, 0))
    vec = pl.BlockSpec((1, D_MODEL), lambda i: (0, 0))
    norm = entry_ln is not None
    outs = pl.pallas_call(
        functools.partial(_inproj_kernel, entry_norm=norm),
        out_shape=tuple(jax.ShapeDtypeStruct((t, w), F32) for w in ((D_MODEL,) if norm else ()) + PROJ_SEGS),
        grid=(t // ROW_TILE,),
        in_specs=[row(D_MODEL)] + ([vec, vec] if norm else [])
        + [pl.BlockSpec((D_MODEL, w), lambda i: (0, 0)) for w in PROJ_SEGS],
        out_specs=tuple(row(w) for w in ((D_MODEL,) if norm else ()) + PROJ_SEGS),
        compiler_params=_cparams(("parallel",)),
        name="in_proj",
    )(h, *([v.reshape(1, D_MODEL) for v in entry_ln] if norm else []), *ws)
    return (outs[0], outs[1:]) if norm else (h, outs)


def _split_w_in(w_in):
    w = w_in.astype(BF)
    c = [0]
    for width in (2 * W_LRU, SSM_D_INNER, SSM_CONV_CH, SSM_HEADS, ATTN_Q_W, ATTN_KV_W):
        c.append(c[-1] + width)
    segs = [w[:, c[i]:c[i + 1]] for i in range(6)]
    segs[3] = jnp.pad(segs[3], ((0, 0), (0, LANES - SSM_HEADS)))
    return tuple(segs)


def _shift_rows(x, s, fill):
    n = x.shape[0]
    if s % SUBLANES == 0:
        return jnp.concatenate([jnp.full((s,) + x.shape[1:], fill, x.dtype), x[:n - s]], axis=0)
    rolled = pltpu.roll(x, s, axis=0)
    row = lax.broadcasted_iota(jnp.int32, x.shape, 0)
    return jnp.where(row < s, fill, rolled)


def _causal_conv(x, halo_ref, w_ref, b_ref):
    n = x.shape[0]
    halo_ref[pl.ds(HALO, n), :] = x
    out = b_ref[...] + w_ref[CONV_K - 1:CONV_K, :] * x
    for s in range(1, CONV_K):
        out = out + w_ref[CONV_K - 1 - s:CONV_K - s, :] * halo_ref[pl.ds(HALO - s, n), :]
    halo_ref[pl.ds(0, HALO), :] = x[n - HALO:n]
    return out


def _rglru_kernel(rg_ref, cw_ref, cb_ref, wa_ref, ba_ref, wx_ref, bx_ref, lam_ref, o_ref, halo_ref, carry_ref):
    n = rg_ref.shape[0]

    @pl.when(pl.program_id(1) == 0)
    def _():
        halo_ref[pl.ds(0, HALO), :] = jnp.zeros((HALO, halo_ref.shape[1]), F32)
        carry_ref[...] = jnp.zeros_like(carry_ref)

    x = rg_ref[:, 0:W_LRU]
    xc = _causal_conv(x, halo_ref, cw_ref, cb_ref)
    xb = xc.astype(BF)
    r = _sigmoid(jnp.dot(xb, wa_ref[...], preferred_element_type=F32) + ba_ref[...])
    i = _sigmoid(jnp.dot(xb, wx_ref[...], preferred_element_type=F32) + bx_ref[...])
    log_a = (-LRU_C * r) * _softplus(-lam_ref[...])
    a = jnp.exp(log_a)
    u = jnp.sqrt(-jnp.tanh(log_a) * (a * a + 1.0)) * (i * xc)
    k = 1
    while k < n:
        u = a * _shift_rows(u, k, 0.0) + u
        a = a * _shift_rows(a, k, 1.0)
        k *= 2
    h = u + a * carry_ref[0:1, :]
    carry_ref[...] = jnp.broadcast_to(h[n - 1:n, :], carry_ref.shape)
    o_ref[...] = (_gelu_tanh(rg_ref[:, W_LRU:2 * W_LRU]) * h).astype(BF)


def _rg_lru(rg, seq, conv_w, conv_b, wa, ba, wx, bx, lam):
    t = rg.shape[0]
    nb = seq // ROW_TILE
    row = lambda b, j: (b * nb + j, 0)
    const = lambda b, j: (0, 0)
    vec = pl.BlockSpec((1, W_LRU), const)
    return pl.pallas_call(
        _rglru_kernel,
        out_shape=jax.ShapeDtypeStruct((t, W_LRU), BF),
        grid=(t // seq, nb),
        in_specs=[pl.BlockSpec((ROW_TILE, 2 * W_LRU), row),
                  pl.BlockSpec((CONV_K, W_LRU), const), vec,
                  pl.BlockSpec((W_LRU, W_LRU), const), vec,
                  pl.BlockSpec((W_LRU, W_LRU), const), vec, vec],
        out_specs=pl.BlockSpec((ROW_TILE, W_LRU), row),
        scratch_shapes=[pltpu.VMEM((HALO + ROW_TILE, W_LRU), F32), pltpu.VMEM((SUBLANES, W_LRU), F32)],
        compiler_params=_cparams(("parallel", "arbitrary")),
        name="rg_lru",
    )(rg, conv_w, conv_b.reshape(1, -1), wa, ba.reshape(1, -1), wx, bx.reshape(1, -1), lam.reshape(1, -1))


def _block_diag(w):
    nb, c, _ = w.shape
    eye = jnp.eye(nb, dtype=w.dtype)
    return (eye[:, None, :, None] * w[:, :, None, :]).reshape(nb * c, nb * c).astype(BF)


def _ssd_kernel(z_ref, xbc_ref, dt_ref, cw_ref, cb_ref, dtb_ref, alog_ref, dskip_ref, ng_ref, expand_ref,
                o_ref, halo_ref, state_ref):
    @pl.when(pl.program_id(1) == 0)
    def _():
        halo_ref[pl.ds(0, HALO), :] = jnp.zeros((HALO, halo_ref.shape[1]), F32)
        state_ref[...] = jnp.zeros_like(state_ref)

    for ck in range(SSM_STEP_CHUNKS):
        _ssd_chunk(z_ref, xbc_ref, dt_ref, cw_ref, cb_ref, dtb_ref, alog_ref, dskip_ref, ng_ref, expand_ref,
                   o_ref, halo_ref, state_ref, pl.ds(ck * SSM_CHUNK, SSM_CHUNK))


def _ssd_chunk(z_ref, xbc_ref, dt_ref, cw_ref, cb_ref, dtb_ref, alog_ref, dskip_ref, ng_ref, expand_ref,
               o_ref, halo_ref, state_ref, rows):
    L = SSM_CHUNK
    xbc = xbc_ref[rows, :]
    conv = _silu(_causal_conv(xbc, halo_ref, cw_ref, cb_ref))
    xs = conv[:, 0:SSM_D_INNER]
    bm = conv[:, SSM_D_INNER:SSM_D_INNER + SSM_GROUPS * SSM_STATE]
    cm = conv[:, SSM_D_INNER + SSM_GROUPS * SSM_STATE:]

    dt = _softplus(dt_ref[rows, :] + dtb_ref[...])
    da = dt * (-jnp.exp(alog_ref[...]))
    row = lax.broadcasted_iota(jnp.int32, (L, L), 0)
    col = lax.broadcasted_iota(jnp.int32, (L, L), 1)
    causal = col <= row
    a_cs = sum(jnp.dot(causal.astype(BF), part, preferred_element_type=F32) for part in _split3(da))
    a_cs_t = a_cs.T
    expand = expand_ref[...]
    dt_full = sum(jnp.dot(part, expand, preferred_element_type=F32) for part in _split3(dt))
    acs_full = sum(jnp.dot(part, expand, preferred_element_type=F32) for part in _split3(a_cs))
    alast_full = acs_full[L - 1:L, :]
    xdt = xs * dt_full
    xdt_b = xdt.astype(BF)
    xst_b = (xdt * jnp.exp(alast_full - acs_full)).astype(BF)
    left = lax.broadcasted_iota(jnp.int32, (L, LANES), 1) < SSM_HEAD_DIM

    y_parts = []
    for g in range(SSM_GROUPS):
        cg = cm[:, g * SSM_STATE:(g + 1) * SSM_STATE].astype(BF)
        bg = bm[:, g * SSM_STATE:(g + 1) * SSM_STATE].astype(BF)
        cb = lax.dot_general(cg, bg, (((1,), (1,)), ((), ())), preferred_element_type=F32)
        st = state_ref[:, g * SSD_GROUP_W:(g + 1) * SSD_GROUP_W]
        y_off = jnp.dot(cg, st.astype(BF), preferred_element_type=F32)
        for j in range(SSD_GROUP_W // LANES):
            h0 = g * (SSM_HEADS // SSM_GROUPS) + 2 * j
            ms = []
            for h in (h0, h0 + 1):
                seg = a_cs[:, h:h + 1] - a_cs_t[h:h + 1, :]
                ms.append((cb * jnp.exp(jnp.where(causal, seg, -jnp.inf))).astype(BF))
            c0 = h0 * SSM_HEAD_DIM
            x2 = xdt_b[:, c0:c0 + LANES]
            zero = jnp.zeros_like(x2)
            xblk = jnp.concatenate([jnp.where(left, x2, zero), jnp.where(left, zero, x2)], axis=0)
            y_parts.append(jnp.dot(jnp.concatenate(ms, axis=1), xblk, preferred_element_type=F32)
                           + y_off[:, j * LANES:(j + 1) * LANES] * jnp.exp(acs_full[:, c0:c0 + LANES]))
        new = lax.dot_general(bg, xst_b[:, g * SSD_GROUP_W:(g + 1) * SSD_GROUP_W], (((0,), (0,)), ((), ())),
                              preferred_element_type=F32)
        state_ref[:, g * SSD_GROUP_W:(g + 1) * SSD_GROUP_W] = (
            st * jnp.exp(alast_full[:, g * SSD_GROUP_W:(g + 1) * SSD_GROUP_W]) + new)
    y = jnp.concatenate(y_parts, axis=1) + dskip_ref[...] * xs
    y = y * _silu(z_ref[rows, :])
    outs = []
    for g in range(SSM_GROUPS):
        yg = y[:, g * SSD_GROUP_W:(g + 1) * SSD_GROUP_W]
        outs.append(yg * lax.rsqrt(jnp.mean(yg * yg, axis=-1, keepdims=True) + LN_EPS))
    o_ref[rows, :] = (jnp.concatenate(outs, axis=1) * ng_ref[...]).astype(BF)


def _ssd(z, xbc, dt, seq, conv_w, conv_b, dt_bias, a_log, d_skip, norm_g):
    t = z.shape[0]
    step = SSM_STEP_CHUNKS * SSM_CHUNK
    nc = seq // step
    row = lambda b, c: (b * nc + c, 0)
    const = lambda b, c: (0, 0)
    pad = LANES - SSM_HEADS
    expand = (jnp.arange(LANES)[:, None] == (jnp.arange(SSM_D_INNER)[None, :] // SSM_HEAD_DIM)).astype(BF)
    return pl.pallas_call(
        _ssd_kernel,
        out_shape=jax.ShapeDtypeStruct((t, SSM_D_INNER), BF),
        grid=(t // seq, nc),
        in_specs=[pl.BlockSpec((step, SSM_D_INNER), row),
                  pl.BlockSpec((step, SSM_CONV_CH), row),
                  pl.BlockSpec((step, LANES), row),
                  pl.BlockSpec((CONV_K, SSM_CONV_CH), const),
                  pl.BlockSpec((1, SSM_CONV_CH), const),
                  pl.BlockSpec((1, LANES), const),
                  pl.BlockSpec((1, LANES), const),
                  pl.BlockSpec((1, SSM_D_INNER), const),
                  pl.BlockSpec((1, SSM_D_INNER), const),
                  pl.BlockSpec((LANES, SSM_D_INNER), const)],
        out_specs=pl.BlockSpec((step, SSM_D_INNER), row),
        scratch_shapes=[pltpu.VMEM((HALO + SSM_CHUNK, SSM_CONV_CH), F32), pltpu.VMEM((SSM_STATE, SSM_D_INNER), F32)],
        compiler_params=_cparams(("parallel", "arbitrary")),
        name="ssd",
    )(z, xbc, dt, conv_w, conv_b.reshape(1, -1), jnp.pad(dt_bias, (0, pad)).reshape(1, -1),
      jnp.pad(a_log, (0, pad)).reshape(1, -1), jnp.repeat(d_skip, SSM_HEAD_DIM).reshape(1, -1),
      norm_g.reshape(1, -1), expand)


def _swa_kernel(sink_ref, q_ref, kv_ref, kvp_ref, o_ref):
    L = ATTN_BLOCK
    lane_k = lax.broadcasted_iota(jnp.int32, (2 * L, LANES), 1)
    qi = lax.broadcasted_iota(jnp.int32, (2 * L, 2 * L), 0) % L
    kj = lax.broadcasted_iota(jnp.int32, (2 * L, 2 * L), 1)
    rel = qi + L - kj
    band = (rel >= 0) & (rel < L)
    for blk in range(ATTN_STEP_BLOCKS):
        rows = pl.ds(blk * L, L)
        prev = kvp_ref[...] if blk == 0 else kv_ref[pl.ds((blk - 1) * L, L), :]
        kv = jnp.concatenate([prev, kv_ref[rows, :]], axis=0)
        valid = band & (jnp.logical_not(pl.program_id(1) == 0) | (kj >= L)) if blk == 0 else band
        _swa_block(sink_ref, q_ref, o_ref, rows, kv, valid, lane_k)


def _swa_block(sink_ref, q_ref, o_ref, rows, kv, valid, lane_k):
    L = ATTN_BLOCK
    hd = ATTN_HEAD_DIM
    kk = kv[:, 0:LANES]
    vv = kv[:, LANES:2 * LANES].astype(BF)
    top = lax.broadcasted_iota(jnp.int32, (2 * L, 1), 0) < L
    lane_o = lax.broadcasted_iota(jnp.int32, (L, LANES), 1)
    scale = hd ** -0.5
    for j in range(ATTN_Q_HEADS // 2):
        g = (2 * j) // ATTN_REP
        q2 = q_ref[rows, j * LANES:(j + 1) * LANES]
        q2r = pltpu.roll(q2, hd, axis=1)
        kg = jnp.where((lane_k >= g * hd) & (lane_k < (g + 1) * hd), kk, 0.0).astype(BF)
        qa, qb = (q2, q2r) if g == 0 else (q2r, q2)
        qs = jnp.concatenate([qa, qb], axis=0).astype(BF)
        logits = lax.dot_general(qs, kg, (((1,), (1,)), ((), ())), preferred_element_type=F32) * scale
        logits = jnp.where(valid, logits, -jnp.inf)
        sink = jnp.where(top, sink_ref[2 * j], sink_ref[2 * j + 1])
        m = jnp.maximum(jnp.max(logits, axis=-1, keepdims=True), sink)
        p = jnp.exp(logits - m)
        probs = p * (1.0 / (jnp.sum(p, axis=-1, keepdims=True) + jnp.exp(sink - m)))
        o = jnp.dot(probs.astype(BF), vv, preferred_element_type=F32)
        oa, ob = o[0:L], o[L:2 * L]
        if g == 0:
            out2 = jnp.where(lane_o < hd, oa, pltpu.roll(ob, hd, axis=1))
        else:
            out2 = jnp.where(lane_o < hd, pltpu.roll(oa, hd, axis=1), ob)
        o_ref[rows, j * LANES:(j + 1) * LANES] = out2


def _swa(q, kv, seq, sinks):
    t = q.shape[0]
    sb = ATTN_STEP_BLOCKS
    rows = sb * ATTN_BLOCK
    ns = seq // rows
    tile = lambda b, n: (b * ns + n, 0)
    before = lambda b, n: (b * ns * sb + jnp.maximum(n * sb - 1, 0), 0)
    return pl.pallas_call(
        _swa_kernel,
        out_shape=jax.ShapeDtypeStruct((t, ATTN_Q_W), F32),
        grid=(t // seq, ns),
        in_specs=[pl.BlockSpec(memory_space=pltpu.SMEM),
                  pl.BlockSpec((rows, ATTN_Q_W), tile),
                  pl.BlockSpec((rows, ATTN_KV_W), tile),
                  pl.BlockSpec((ATTN_BLOCK, ATTN_KV_W), before)],
        out_specs=pl.BlockSpec((rows, ATTN_Q_W), tile),
        compiler_params=_cparams(("parallel", "parallel")),
        name="swa",
    )(sinks, q, kv, kv)


def _outproj_kernel(ya_ref, yb_ref, yc_ref, h_ref, wa_ref, wb_ref, wc_ref, g_ref, b_ref, o_ref):
    mix = (jnp.dot(ya_ref[...], wa_ref[...], preferred_element_type=F32)
           + jnp.dot(yb_ref[...], wb_ref[...], preferred_element_type=F32)
           + jnp.dot(yc_ref[...].astype(BF), wc_ref[...], preferred_element_type=F32))
    o_ref[...] = _ln_math(DN_ALPHA * h_ref[...] + mix, g_ref[...], b_ref[...])


def _out_proj(ya, yb, yc, h, w_out, g, b):
    t = h.shape[0]
    w = w_out.astype(BF)
    row = lambda i: (i, 0)
    const = lambda i: (0, 0)
    c1, c2 = W_LRU, W_LRU + SSM_D_INNER
    return pl.pallas_call(
        _outproj_kernel,
        out_shape=jax.ShapeDtypeStruct((t, D_MODEL), F32),
        grid=(t // ROW_TILE,),
        in_specs=[pl.BlockSpec((ROW_TILE, W_LRU), row), pl.BlockSpec((ROW_TILE, SSM_D_INNER), row),
                  pl.BlockSpec((ROW_TILE, ATTN_Q_W), row), pl.BlockSpec((ROW_TILE, D_MODEL), row),
                  pl.BlockSpec((W_LRU, D_MODEL), const), pl.BlockSpec((SSM_D_INNER, D_MODEL), const),
                  pl.BlockSpec((ATTN_Q_W, D_MODEL), const),
                  pl.BlockSpec((1, D_MODEL), const), pl.BlockSpec((1, D_MODEL), const)],
        out_specs=pl.BlockSpec((ROW_TILE, D_MODEL), row),
        compiler_params=_cparams(("parallel",)),
        name="out_proj",
    )(ya, yb, yc, h, w[0:c1], w[c1:c2], w[c2:MIX_WIDTH], g.reshape(1, -1), b.reshape(1, -1))


def _gather_group(idx_ref, tbl_ref, t, g):
    rs = ROW_WORDS_SUBLANES
    pieces = []
    for q in range(GROUP_PAIRS // 2):
        p = g * GROUP_PAIRS + 2 * q
        if p % SUBLANES == 0:
            window = idx_ref.at[t, pl.ds(p, SUBLANES)]
        ra = pl.multiple_of(window[p % SUBLANES], rs)
        rb = pl.multiple_of(window[p % SUBLANES + 1], rs)
        pieces.append(jnp.concatenate([tbl_ref[pl.ds(ra, rs), :], tbl_ref[pl.ds(rb, rs), :]], axis=0))
    return pltpu.bitcast(jnp.concatenate(pieces, axis=0), BF)


def _chunk_of_column(col):
    j = col % PAIR_LANES
    return j // 2 + ROW_WORDS_SUBLANES * (j % 2)


def _split_hi_lo(x):
    hi = x.astype(BF)
    lo = (x - hi.astype(F32)).astype(BF)
    return hi, lo


def _peer_in_kernel(idx_ref, x_ref, gate_ref, tbl_ref, o_ref, m_ref):
    n = x_ref.shape[0]
    row = lax.broadcasted_iota(jnp.int32, (2 * SUBLANES, GROUP_K), 0) % SUBLANES
    keep = row == _chunk_of_column(lax.broadcasted_iota(jnp.int32, (2 * SUBLANES, GROUP_K), 1))
    for t in range(n):
        x_row = x_ref[pl.ds(t, 1), :]
        x8 = jnp.concatenate([x_row[:, c * LANES:(c + 1) * LANES] for c in range(TOKEN_SUBLANES)], axis=0)
        x_hi, x_lo = _split_hi_lo(x8)
        x16 = jnp.concatenate([x_hi, x_lo], axis=0)
        for g in range(PEER_PAIRS // GROUP_PAIRS):
            r = lax.dot_general(x16, _gather_group(idx_ref, tbl_ref, t, g), (((1,), (1,)), ((), ())),
                                preferred_element_type=F32)
            m_ref[pl.ds(t, 1), g * GROUP_K:(g + 1) * GROUP_K] = jnp.sum(jnp.where(keep, r, 0.0), axis=0,
                                                                        keepdims=True)
    wide = PEER_PAIRS * PAIR_LANES
    fold = (lax.broadcasted_iota(jnp.int32, (wide, PEER_PAIRS), 0) // PAIR_LANES
            == lax.broadcasted_iota(jnp.int32, (wide, PEER_PAIRS), 1)).astype(BF)
    m_hi, m_lo = _split_hi_lo(m_ref[...])
    a = jnp.dot(m_hi, fold, preferred_element_type=F32) + jnp.dot(m_lo, fold, preferred_element_type=F32)
    o_ref[...] = gate_ref[...] * (0.5 * a * (1.0 + lax.erf(a * (2.0 ** -0.5))))


def _peer_in(rows, x, gate, tbl):
    t = x.shape[0]
    tt = GATHER_TOKENS
    pairs = pl.BlockSpec((tt, PEER_PAIRS), lambda i: (i, 0))
    return pl.pallas_call(
        _peer_in_kernel,
        out_shape=jax.ShapeDtypeStruct((t, PEER_PAIRS), F32),
        grid=(t // tt,),
        in_specs=[
            pl.BlockSpec((tt, PEER_PAIRS), lambda i: (i, 0), memory_space=pltpu.SMEM),
            pl.BlockSpec((tt, D_MODEL), lambda i: (i, 0)),
            pairs,
            pl.BlockSpec(memory_space=pltpu.VMEM),
        ],
        out_specs=pairs,
        scratch_shapes=[pltpu.VMEM((tt, PEER_PAIRS * PAIR_LANES), F32)],
        compiler_params=pltpu.CompilerParams(dimension_semantics=("parallel",), vmem_limit_bytes=PEER_VMEM_LIMIT),
        name="peer_in",
    )(rows, x, gate, tbl)


def _peer_out_kernel(idx_ref, w_ref, tbl_ref, h_ref, g_ref, b_ref, o_ref, wide_ref, ffn_ref):
    n = w_ref.shape[0]
    wide = PEER_PAIRS * PAIR_LANES
    rep = (lax.broadcasted_iota(jnp.int32, (PEER_PAIRS, wide), 1) // PAIR_LANES
           == lax.broadcasted_iota(jnp.int32, (PEER_PAIRS, wide), 0)).astype(BF)
    for k, wv in enumerate(_split_hi_lo(w_ref[...])):
        wide_ref[k] = jnp.dot(wv, rep, preferred_element_type=F32)
    keep = (lax.broadcasted_iota(jnp.int32, (SUBLANES, GROUP_K), 0)
            == _chunk_of_column(lax.broadcasted_iota(jnp.int32, (SUBLANES, GROUP_K), 1)))
    for t in range(n):
        acc = jnp.zeros((2 * SUBLANES, LANES), F32)
        for g in range(PEER_PAIRS // GROUP_PAIRS):
            cols = slice(g * GROUP_K, (g + 1) * GROUP_K)
            lhs = jnp.concatenate(
                [jnp.where(keep, jnp.broadcast_to(wide_ref[k, pl.ds(t, 1), cols], (SUBLANES, GROUP_K)), 0.0)
                 for k in range(2)], axis=0).astype(BF)
            acc = acc + jnp.dot(lhs, _gather_group(idx_ref, tbl_ref, t, g), preferred_element_type=F32)
        out8 = acc[0:SUBLANES] + acc[SUBLANES:]
        for c in range(TOKEN_SUBLANES):
            ffn_ref[pl.ds(t, 1), c * LANES:(c + 1) * LANES] = out8[c:c + 1, :]
    o_ref[...] = _ln_math(DN_ALPHA * h_ref[...] + ffn_ref[...], g_ref[...], b_ref[...])


def _peer_out(rows, w, tbl, h, g, b):
    t = h.shape[0]
    tt = GATHER_TOKENS
    row = pl.BlockSpec((tt, D_MODEL), lambda i: (i, 0))
    vec = pl.BlockSpec((1, D_MODEL), lambda i: (0, 0))
    return pl.pallas_call(
        _peer_out_kernel,
        out_shape=jax.ShapeDtypeStruct((t, D_MODEL), F32),
        grid=(t // tt,),
        in_specs=[
            pl.BlockSpec((tt, PEER_PAIRS), lambda i: (i, 0), memory_space=pltpu.SMEM),
            pl.BlockSpec((tt, PEER_PAIRS), lambda i: (i, 0)),
            pl.BlockSpec(memory_space=pltpu.VMEM),
            row, vec, vec,
        ],
        out_specs=row,
        scratch_shapes=[pltpu.VMEM((2, tt, PEER_PAIRS * PAIR_LANES), F32), pltpu.VMEM((tt, D_MODEL), F32)],
        compiler_params=pltpu.CompilerParams(vmem_limit_bytes=PEER_VMEM_LIMIT),
        name="peer_out",
    )(rows, w, tbl, h, g.reshape(1, D_MODEL), b.reshape(1, D_MODEL))


def _topk_two_per_slot(s, pos, payload, k):
    m = s.shape[0] // 2
    big = jnp.asarray(2 ** 30, pos.dtype)
    a, b, pa, pb, ea, eb = s[:m], s[m:], pos[:m], pos[m:], payload[:m], payload[m:]
    b_wins = (b > a) | ((b == a) & (pb < pa))
    win, wait = jnp.where(b_wins, b, a), jnp.where(b_wins, a, b)
    win_pos, wait_pos = jnp.where(b_wins, pb, pa), jnp.where(b_wins, pa, pb)
    win_pay, wait_pay = jnp.where(b_wins, eb, ea), jnp.where(b_wins, ea, eb)
    vals, pays = [], []
    for _ in range(k):
        top = jnp.max(win, axis=0, keepdims=True)
        j = jnp.min(jnp.where(win == top, win_pos, big), axis=0, keepdims=True)
        sel = win_pos == j
        vals.append(top)
        pays.append(jnp.sum(jnp.where(sel, win_pay, jnp.zeros_like(win_pay)), axis=0, keepdims=True))
        win = jnp.where(sel, wait, win)
        win_pos = jnp.where(sel, wait_pos, win_pos)
        win_pay = jnp.where(sel, wait_pay, win_pay)
        wait = jnp.where(sel, -jnp.inf, wait)
    return jnp.concatenate(vals, axis=0), jnp.concatenate(pays, axis=0)


def _topk_four_per_slot(s, k):
    m, n = s.shape[0] // 4, s.shape[1]
    big = jnp.asarray(2 ** 30, F32)
    base = lax.broadcasted_iota(jnp.int32, (m, n), 0).astype(F32)
    val = [s[i * m:(i + 1) * m] for i in range(4)]
    pos = [base + i * m for i in range(4)]

    def order(i, j):
        x, y, px, py = val[i], val[j], pos[i], pos[j]
        swap = (y > x) | ((y == x) & (py < px))
        val[i], val[j] = jnp.where(swap, y, x), jnp.where(swap, x, y)
        pos[i], pos[j] = jnp.where(swap, py, px), jnp.where(swap, px, py)

    for i, j in ((0, 1), (2, 3), (0, 2), (1, 3), (1, 2)):
        order(i, j)
    tops, top_pos = [], []
    for _ in range(k):
        top = jnp.max(val[0], axis=0, keepdims=True)
        jmin = jnp.min(jnp.where(val[0] == top, pos[0], big), axis=0, keepdims=True)
        sel = pos[0] == jmin
        tops.append(top)
        top_pos.append(jmin)
        for i in range(3):
            val[i] = jnp.where(sel, val[i + 1], val[i])
            pos[i] = jnp.where(sel, pos[i + 1], pos[i])
        val[3] = jnp.where(sel, -jnp.inf, val[3])
    return jnp.concatenate(tops, axis=0), jnp.concatenate(top_pos, axis=0)


def _peer_route_kernel(h_ref, wqt_ref, keys_ref, idx_ref, gate_ref):
    half_dim = PEER_DKEY // 2
    hb = h_ref[...].astype(BF)
    n = hb.shape[0]
    iota16 = lax.broadcasted_iota(jnp.int32, (PEER_TOPK, n), 0).astype(F32)
    iota8 = lax.broadcasted_iota(jnp.int32, (SUBLANES, n), 0).astype(F32)
    qt = lax.dot_general(wqt_ref[...], hb, (((1,), (1,)), ((), ())), preferred_element_type=F32)
    idx_rows, gate_rows = [], []
    for hd in range(PEER_HEADS):
        tv, ti = [], []
        for half in range(2):
            r0 = hd * PEER_DKEY + half * half_dim
            s = jnp.dot(keys_ref[half], qt[r0:r0 + half_dim, :], preferred_element_type=F32)
            v, i = _topk_four_per_slot(s, PEER_TOPK)
            tv.append(v)
            ti.append(i)
        v1_lo, i1_lo = tv[1][0:SUBLANES, :], ti[1][0:SUBLANES, :]

        def block(a, first_row):
            shift = (lambda x: x) if first_row == 0 else (lambda x: pltpu.roll(x, first_row, axis=0))
            return (tv[0][a:a + 1, :] + shift(v1_lo), iota8 + (a * PEER_TOPK - first_row),
                    ti[0][a:a + 1, :] * PEER_NKEYS + shift(i1_lo))

        def pick(cond, x, y):
            return tuple(jnp.where(cond, xi, yi) for xi, yi in zip(x, y))

        padding = (jnp.full_like(v1_lo, -jnp.inf), iota8 + 2 ** 20, jnp.zeros_like(i1_lo))
        tiles = [block(1, 0),
                 pick(iota8 < 5, block(2, 0), block(4, 5)),
                 pick(iota8 < 4, block(3, 0), pick(iota8 < 6, block(5, 4), block(6, 6))),
                 pick(iota8 < 2, block(7, 0), padding),
                 (tv[0][SUBLANES:, :] + tv[1][0:1, :], (iota8 + SUBLANES) * PEER_TOPK,
                  ti[0][SUBLANES:, :] * PEER_NKEYS + ti[1][0:1, :]),
                 (padding[0], padding[1] + SUBLANES, padding[2])]
        cs = [tv[0][0:1, :] + tv[1]] + [tile[0] for tile in tiles]
        cp = [iota16] + [tile[1] for tile in tiles]
        ce = [ti[0][0:1, :] * PEER_NKEYS + ti[1]] + [tile[2] for tile in tiles]
        best_s, best_e = _topk_two_per_slot(jnp.concatenate(cs, axis=0), jnp.concatenate(cp, axis=0),
                                            jnp.concatenate(ce, axis=0), PEER_TOPK)
        ex = jnp.exp(best_s - best_s[0:1, :])
        gate_rows.append(ex / jnp.sum(ex, axis=0, keepdims=True))
        idx_rows.append(best_e)
    idx_ref[...] = (jnp.concatenate(idx_rows, axis=0) * ROW_WORDS_SUBLANES).astype(jnp.int32).T
    gate_ref[...] = jnp.concatenate(gate_rows, axis=0).T


def _peer_route(h, wqt, keys):
    t = h.shape[0]
    rt = ROUTE_TOKENS
    pairs = pl.BlockSpec((rt, PEER_PAIRS), lambda i: (i, 0))
    return pl.pallas_call(
        _peer_route_kernel,
        out_shape=(jax.ShapeDtypeStruct((t, PEER_PAIRS), jnp.int32), jax.ShapeDtypeStruct((t, PEER_PAIRS), F32)),
        grid=(t // rt,),
        in_specs=[
            pl.BlockSpec((rt, D_MODEL), lambda i: (i, 0)),
            pl.BlockSpec((D_MODEL, D_MODEL), lambda i: (0, 0)),
            pl.BlockSpec((2, PEER_NKEYS, PEER_DKEY // 2), lambda i: (0, 0, 0)),
        ],
        out_specs=(pairs, pairs),
        compiler_params=_cparams(("parallel",)),
        name="peer_route",
    )(h, wqt, keys)


def _pack_kernel(u_ref, o_ref):
    half = D_MODEL // 2
    n = u_ref.shape[0]
    rs = ROW_WORDS_SUBLANES
    lo = lax.bitcast_convert_type(u_ref[:, 0:half].astype(BF).astype(F32), jnp.uint32) >> 16
    hi = lax.bitcast_convert_type(u_ref[:, half:D_MODEL].astype(BF).astype(F32), jnp.uint32) & jnp.uint32(0xFFFF0000)
    w = lo | hi
    for s in range(rs):
        o_ref[pl.ds(s, n, stride=rs), :] = w[:, s * LANES:(s + 1) * LANES]


def _pack_table(u):
    e = u.shape[0]
    return pl.pallas_call(
        _pack_kernel,
        out_shape=jax.ShapeDtypeStruct((e * ROW_WORDS_SUBLANES, LANES), jnp.uint32),
        grid=(e // ROW_TILE,),
        in_specs=[pl.BlockSpec((ROW_TILE, D_MODEL), lambda i: (i, 0))],
        out_specs=pl.BlockSpec((ROW_TILE * ROW_WORDS_SUBLANES, LANES), lambda i: (i, 0)),
        compiler_params=_cparams(("parallel",)),
        name="pack_table",
    )(u)


def _peer_sublayer(h, wq, keys, u, v, ln_g, ln_b):
    rows, gate = _peer_route(h, wq.T.astype(BF), keys)
    w = _peer_in(rows, h, gate, _pack_table(u))
    return _peer_out(rows, w, _pack_table(v), h, ln_g, ln_b)


def kernel(x, emb_ln_g, emb_ln_b, w_in, rg_conv_w, rg_conv_b, rg_wa, rg_ba, rg_wx, rg_bx, rg_lambda, ssm_conv_w, ssm_conv_b, ssm_dt_bias, ssm_a_log, ssm_d, ssm_norm_g, attn_sinks, w_out, ln1_g, ln1_b, peer_wq, peer_keys, peer_u, peer_v, ln2_g, ln2_b):
    bsz, seq, d = x.shape
    assert d == D_MODEL and seq % ROW_TILE == 0 and (bsz * seq) % ROUTE_TOKENS == 0
    assert rg_wa.shape[1:] == (LRU_BLOCKS, W_LRU // LRU_BLOCKS, W_LRU // LRU_BLOCKS)
    h = x.reshape(bsz * seq, d)
    for l in range(DEPTH):
        h, (rg, z, xbc, dt, q, kv) = _in_proj(h, _split_w_in(w_in[l]), (emb_ln_g, emb_ln_b) if l == 0 else None)
        y_a = _rg_lru(rg, seq, rg_conv_w[l], rg_conv_b[l], _block_diag(rg_wa[l]), rg_ba[l],
                      _block_diag(rg_wx[l]), rg_bx[l], rg_lambda[l])
        y_b = _ssd(z, xbc, dt, seq, ssm_conv_w[l], ssm_conv_b[l], ssm_dt_bias[l], ssm_a_log[l], ssm_d[l],
                   ssm_norm_g[l])
        y_c = _swa(q, kv, seq, attn_sinks[l])
        h = _out_proj(y_a, y_b, y_c, h, w_out[l], ln1_g[l], ln1_b[l])
        h = _peer_sublayer(h, peer_wq[l], peer_keys[l], peer_u[l], peer_v[l], ln2_g[l], ln2_b[l])
    return h.reshape(bsz, seq, d)
```
